```python
import jax, jax.numpy as jnp
from jax import lax
import numpy as np

D_MODEL = 1024
BATCH = 16
SEQ = 2048
DEPTH = 1

D_MIX = D_MODEL
GDN_HEADS = 4
GDN_DK = 128
GDN_DV = 128
GDN_CONV = 4
GDN_CHUNK = 64
NSA_HEADS = 8
NSA_KV = 2
NSA_DH = 64
CMP_BLOCK = 32
CMP_STRIDE = 16
CMP_HIDDEN = 128
SEL_BLOCK = 64
SEL_TOPK = 16
WINDOW = 512
SEL_QCHUNK = 32
WIN_QBLOCK = 128
ROPE_THETA = 500000.0
ROPE_DIM = NSA_DH // 4
D_FF = 2816
EPS = 1e-6
NEG = -1e30
BIG = 1e30

W_GDN_QKV = GDN_HEADS * (2 * GDN_DK + GDN_DV)
W_GDN_Z = GDN_HEADS * GDN_DV
W_GDN_A = GDN_HEADS
W_GDN_B = GDN_HEADS
W_NSA_Q = NSA_HEADS * NSA_DH
W_NSA_KV = 2 * NSA_KV * NSA_DH
W_NSA_GATE = 3 * NSA_HEADS
D_IN = W_GDN_QKV + W_GDN_Z + W_GDN_A + W_GDN_B + W_NSA_Q + 3 * W_NSA_KV + W_NSA_GATE

kernel_name = "hybrid_gdn_nsa_macaron_block"


def rmsnorm(x, w):
    xf = x.astype(jnp.float32)
    y = xf * lax.rsqrt(jnp.mean(xf * xf, axis=-1, keepdims=True) + EPS)
    return (y * w.astype(jnp.float32)).astype(x.dtype)


def l2norm(x):
    xf = x.astype(jnp.float32)
    return xf * lax.rsqrt(jnp.sum(xf * xf, axis=-1, keepdims=True) + EPS)


def swiglu(x, w_gate, w_up, w_down):
    return (jax.nn.silu(x @ w_gate) * (x @ w_up)) @ w_down


def split_cols(y, widths):
    cuts = np.cumsum(widths)[:-1].tolist()
    return jnp.split(y, cuts, axis=-1)


def masked_softmax(s, mask):
    s = jnp.where(mask, s.astype(jnp.float32), NEG)
    m = jnp.max(s, axis=-1, keepdims=True)
    p = jnp.exp(s - m) * mask
    return p / jnp.maximum(jnp.sum(p, axis=-1, keepdims=True), 1e-30)


def partial_rope(x, pos):
    half = ROPE_DIM // 2
    inv = ROPE_THETA ** (-jnp.arange(half, dtype=jnp.float32) / half)
    ang = pos.astype(jnp.float32)[:, None] * inv[None, :]
    cos = jnp.cos(ang)[None, :, None, :]
    sin = jnp.sin(ang)[None, :, None, :]
    xr = x[..., :ROPE_DIM].astype(jnp.float32)
    x1, x2 = xr[..., :half], xr[..., half:]
    rot = jnp.concatenate([x1 * cos - x2 * sin, x2 * cos + x1 * sin], axis=-1).astype(x.dtype)
    return jnp.concatenate([rot, x[..., ROPE_DIM:]], axis=-1)


def causal_depthwise_conv(x, w):
    k = w.shape[0]
    return lax.conv_general_dilated(
        x, w[:, None, :].astype(x.dtype), window_strides=(1,), padding=[(k - 1, 0)],
        dimension_numbers=("NWC", "WIO", "NWC"), feature_group_count=x.shape[-1])


def gated_delta_chunked(q, k, v, g, beta):
    bsz, s_len, h, dk = q.shape
    dv = v.shape[-1]
    c = GDN_CHUNK
    n = s_len // c
    q = q.astype(jnp.float32) * (dk ** -0.5)
    k = k.astype(jnp.float32)
    v = v.astype(jnp.float32)

    def to_chunks(t):
        return t.reshape(bsz, n, c, h, -1).transpose(0, 3, 1, 2, 4)

    q, k, v = to_chunks(q), to_chunks(k), to_chunks(v)
    g = jnp.cumsum(g.astype(jnp.float32).reshape(bsz, n, c, h).transpose(0, 3, 1, 2), axis=-1)
    beta = beta.astype(jnp.float32).reshape(bsz, n, c, h).transpose(0, 3, 1, 2)
    tril = jnp.tril(jnp.ones((c, c), dtype=bool))
    strict = jnp.tril(jnp.ones((c, c), dtype=bool), -1)
    diff = g[..., :, None] - g[..., None, :]
    decay = jnp.where(tril, jnp.exp(jnp.where(tril, diff, 0.0)), 0.0)
    kb = k * beta[..., None]
    vb = v * beta[..., None]
    lower = jnp.where(strict, jnp.einsum('bhnid,bhnjd->bhnij', kb, k) * decay, 0.0)
    eye = jnp.eye(c, dtype=jnp.float32)
    a_mat = eye + lower
    t_mat = lax.linalg.triangular_solve(a_mat, jnp.broadcast_to(eye, a_mat.shape), left_side=True,
                                        lower=True, unit_diagonal=True)
    u = t_mat @ vb
    w = t_mat @ (kb * jnp.exp(g)[..., None])
    qk = jnp.where(tril, jnp.einsum('bhnid,bhnjd->bhnij', q, k) * decay, 0.0)

    def step(state, inp):
        q_c, k_c, u_c, w_c, g_c, qk_c = inp
        v_new = u_c - w_c @ state
        o = (q_c * jnp.exp(g_c)[..., None]) @ state + qk_c @ v_new
        g_last = g_c[..., -1]
        k_dec = k_c * jnp.exp(g_last[..., None] - g_c)[..., None]
        state = state * jnp.exp(g_last)[..., None, None] + jnp.einsum('bhcd,bhce->bhde', k_dec, v_new)
        return state, o

    xs = tuple(jnp.moveaxis(t, 2, 0) for t in (q, k, u, w, g, qk))
    state0 = jnp.zeros((bsz, h, dk, dv), jnp.float32)
    _, o = lax.scan(step, state0, xs)
    return o.transpose(1, 0, 3, 2, 4).reshape(bsz, s_len, h, dv)


def gdn_group(qkv, z, a, b, conv_w, a_log, dt_bias, out_norm):
    bsz, s_len, _ = qkv.shape
    qkv = jax.nn.silu(causal_depthwise_conv(qkv, conv_w))
    q, k, v = split_cols(qkv, [GDN_HEADS * GDN_DK, GDN_HEADS * GDN_DK, GDN_HEADS * GDN_DV])
    q = l2norm(q.reshape(bsz, s_len, GDN_HEADS, GDN_DK))
    k = l2norm(k.reshape(bsz, s_len, GDN_HEADS, GDN_DK))
    v = v.reshape(bsz, s_len, GDN_HEADS, GDN_DV)
    beta = jax.nn.sigmoid(b.astype(jnp.float32))
    g = -jnp.exp(a_log.astype(jnp.float32)) * jax.nn.softplus(a.astype(jnp.float32) + dt_bias.astype(jnp.float32))
    o = gated_delta_chunked(q, k, v, g, beta)
    o = rmsnorm(o, out_norm) * jax.nn.silu(z.reshape(bsz, s_len, GDN_HEADS, GDN_DV).astype(jnp.float32))
    return o.reshape(bsz, s_len, GDN_HEADS * GDN_DV).astype(qkv.dtype)


def compress_tokens(t, pe, w1, w2):
    bsz, s_len, g, dh = t.shape
    n_cmp = (s_len - CMP_BLOCK) // CMP_STRIDE + 1
    idx = jnp.arange(n_cmp)[:, None] * CMP_STRIDE + jnp.arange(CMP_BLOCK)[None, :]
    blocks = t[:, idx] + pe[None, None, :, None, :]
    flat = blocks.transpose(0, 1, 3, 2, 4).reshape(bsz, n_cmp, g, CMP_BLOCK * dh)
    return jax.nn.silu(flat @ w1) @ w2


def nsa_group(q, kv_cmp, kv_slc, kv_win, gate_logits, pe_k, w1_k, w2_k, pe_v, w1_v, w2_v):
    bsz, s_len, _ = q.shape
    g_n, hpg, dh = NSA_KV, NSA_HEADS // NSA_KV, NSA_DH
    scale = dh ** -0.5
    pos = jnp.arange(s_len)
    q = q.reshape(bsz, s_len, NSA_HEADS, dh)

    def kv_split(t):
        t = t.reshape(bsz, s_len, 2, g_n, dh)
        return t[:, :, 0], t[:, :, 1]

    k_c, v_c = kv_split(kv_cmp)
    k_s, v_s = kv_split(kv_slc)
    k_w, v_w = kv_split(kv_win)
    q_plain = q.reshape(bsz, s_len, g_n, hpg, dh)
    q_rot = partial_rope(q, pos).reshape(bsz, s_len, g_n, hpg, dh)
    k_s = partial_rope(k_s, pos)
    k_w = partial_rope(k_w, pos)

    kc = compress_tokens(k_c, pe_k, w1_k, w2_k)
    vc = compress_tokens(v_c, pe_v, w1_v, w2_v)
    n_cmp = kc.shape[1]
    s_cmp = jnp.einsum('bsghd,bngd->bghsn', q_plain, kc) * scale
    cmp_start = jnp.arange(n_cmp) * CMP_STRIDE
    p_cmp = masked_softmax(s_cmp, (cmp_start + CMP_BLOCK - 1)[None, :] <= pos[:, None])
    o_cmp = jnp.einsum('bghsn,bngd->bsghd', p_cmp, vc.astype(jnp.float32))

    nb = s_len // SEL_BLOCK
    sel_start = jnp.arange(nb) * SEL_BLOCK
    overlap = ((cmp_start[:, None] < sel_start[None, :] + SEL_BLOCK)
               & (cmp_start[:, None] + CMP_BLOCK > sel_start[None, :])).astype(jnp.float32)
    imp = jnp.einsum('bghsn,nj->bgsj', p_cmp, overlap)
    blk = jnp.arange(nb)[None, :]
    cur = (pos // SEL_BLOCK)[:, None]
    forced = (blk == 0) | (blk == cur) | (blk == cur - 1)
    score = jnp.where(blk <= cur, jnp.where(forced, BIG, imp), NEG)
    n_sel = min(SEL_TOPK, nb)
    _, top_idx = lax.top_k(score, n_sel)

    kb = k_s.reshape(bsz, nb, SEL_BLOCK, g_n, dh).transpose(0, 3, 1, 2, 4)
    vb = v_s.reshape(bsz, nb, SEL_BLOCK, g_n, dh).transpose(0, 3, 1, 2, 4)
    bi = jnp.arange(bsz)[:, None, None, None]
    gi = jnp.arange(g_n)[None, :, None, None]

    def sel_chunk(ci):
        s0 = ci * SEL_QCHUNK
        q_c = lax.dynamic_slice_in_dim(q_rot, s0, SEL_QCHUNK, axis=1)
        idx_c = lax.dynamic_slice_in_dim(top_idx, s0, SEL_QCHUNK, axis=2)
        t_c = s0 + jnp.arange(SEL_QCHUNK)
        kk = kb[bi, gi, idx_c]
        vv = vb[bi, gi, idx_c]
        s = jnp.einsum('bqghd,bgqnkd->bghqnk', q_c, kk) * scale
        key_pos = idx_c[..., None] * SEL_BLOCK + jnp.arange(SEL_BLOCK)
        mask = (key_pos <= t_c[None, None, :, None, None]).reshape(bsz, g_n, 1, SEL_QCHUNK, n_sel * SEL_BLOCK)
        p = masked_softmax(s.reshape(bsz, g_n, hpg, SEL_QCHUNK, n_sel * SEL_BLOCK), mask)
        p = p.reshape(bsz, g_n, hpg, SEL_QCHUNK, n_sel, SEL_BLOCK)
        return jnp.einsum('bghqnk,bgqnkd->bqghd', p, vv.astype(jnp.float32))

    o_slc = lax.map(sel_chunk, jnp.arange(s_len // SEL_QCHUNK))
    o_slc = jnp.moveaxis(o_slc, 0, 1).reshape(bsz, s_len, g_n, hpg, dh)

    kp = jnp.pad(k_w, ((0, 0), (WINDOW, 0), (0, 0), (0, 0)))
    vp = jnp.pad(v_w, ((0, 0), (WINDOW, 0), (0, 0), (0, 0)))

    def win_block(ci):
        s0 = ci * WIN_QBLOCK
        q_c = lax.dynamic_slice_in_dim(q_rot, s0, WIN_QBLOCK, axis=1)
        k_c2 = lax.dynamic_slice_in_dim(kp, s0, WINDOW + WIN_QBLOCK, axis=1)
        v_c2 = lax.dynamic_slice_in_dim(vp, s0, WINDOW + WIN_QBLOCK, axis=1)
        t_c = s0 + jnp.arange(WIN_QBLOCK)
        s_pos = s0 - WINDOW + jnp.arange(WINDOW + WIN_QBLOCK)
        mask = (s_pos[None, :] >= 0) & (s_pos[None, :] <= t_c[:, None]) & (t_c[:, None] - s_pos[None, :] < WINDOW)
        s = jnp.einsum('bqghd,bkgd->bghqk', q_c, k_c2) * scale
        p = masked_softmax(s, mask)
        return jnp.einsum('bghqk,bkgd->bqghd', p, v_c2.astype(jnp.float32))

    o_win = lax.map(win_block, jnp.arange(s_len // WIN_QBLOCK))
    o_win = jnp.moveaxis(o_win, 0, 1).reshape(bsz, s_len, g_n, hpg, dh)

    gates = jax.nn.sigmoid(gate_logits.astype(jnp.float32)).reshape(bsz, s_len, g_n, hpg, 3)
    o = gates[..., 0:1] * o_cmp + gates[..., 1:2] * o_slc + gates[..., 2:3] * o_win
    return o.reshape(bsz, s_len, NSA_HEADS * dh).astype(q.dtype)


def hybrid_mixer(h, w_in, gdn_conv_w, gdn_a_log, gdn_dt_bias, gdn_out_norm,
                 pe_k, w1_k, w2_k, pe_v, w1_v, w2_v, w_out):
    y = h @ w_in
    (g_qkv, g_z, g_a, g_b, n_q, n_kvc, n_kvs, n_kvw, n_gate) = split_cols(
        y, [W_GDN_QKV, W_GDN_Z, W_GDN_A, W_GDN_B, W_NSA_Q, W_NSA_KV, W_NSA_KV, W_NSA_KV, W_NSA_GATE])
    o_gdn = gdn_group(g_qkv, g_z, g_a, g_b, gdn_conv_w, gdn_a_log, gdn_dt_bias, gdn_out_norm)
    o_nsa = nsa_group(n_q, n_kvc, n_kvs, n_kvw, n_gate, pe_k, w1_k, w2_k, pe_v, w1_v, w2_v)
    return jnp.concatenate([o_gdn, o_nsa], axis=-1) @ w_out


def setup_inputs(seed: int = 0) -> dict:
    key = jax.random.key(seed)
    ks = jax.random.split(key, 24)
    f32 = jnp.float32

    def nrm(k, shape, scale):
        return jax.random.normal(k, shape, f32) * scale

    def gain(k, shape):
        return 1.0 + 0.02 * jax.random.normal(k, shape, f32)

    dt = jnp.exp(jax.random.uniform(ks[9], (DEPTH, GDN_HEADS), f32) * (jnp.log(0.1) - jnp.log(0.001)) + jnp.log(0.001))
    return {
        "x": jax.random.normal(ks[0], (BATCH, SEQ, D_MODEL), f32),
        "ffn1_norm": gain(ks[1], (DEPTH, D_MODEL)),
        "ffn1_w_gate": nrm(ks[2], (DEPTH, D_MODEL, D_FF), D_MODEL ** -0.5),
        "ffn1_w_up": nrm(ks[3], (DEPTH, D_MODEL, D_FF), D_MODEL ** -0.5),
        "ffn1_w_down": nrm(ks[4], (DEPTH, D_FF, D_MODEL), D_FF ** -0.5),
        "mix_norm": gain(ks[5], (DEPTH, D_MODEL)),
        "w_in": nrm(ks[6], (DEPTH, D_MODEL, D_IN), D_MODEL ** -0.5),
        "gdn_conv_w": nrm(ks[7], (DEPTH, GDN_CONV, W_GDN_QKV), GDN_CONV ** -0.5),
        "gdn_a_log": jnp.log(jax.random.uniform(ks[8], (DEPTH, GDN_HEADS), f32, 1.0, 16.0)),
        "gdn_dt_bias": dt + jnp.log(-jnp.expm1(-dt)),
        "gdn_out_norm": gain(ks[10], (DEPTH, GDN_DV)),
        "cmp_pe_k": nrm(ks[11], (DEPTH, CMP_BLOCK, NSA_DH), 0.1),
        "cmp_w1_k": nrm(ks[12], (DEPTH, CMP_BLOCK * NSA_DH, CMP_HIDDEN), (CMP_BLOCK * NSA_DH) ** -0.5),
        "cmp_w2_k": nrm(ks[13], (DEPTH, CMP_HIDDEN, NSA_DH), CMP_HIDDEN ** -0.5),
        "cmp_pe_v": nrm(ks[14], (DEPTH, CMP_BLOCK, NSA_DH), 0.1),
        "cmp_w1_v": nrm(ks[15], (DEPTH, CMP_BLOCK * NSA_DH, CMP_HIDDEN), (CMP_BLOCK * NSA_DH) ** -0.5),
        "cmp_w2_v": nrm(ks[16], (DEPTH, CMP_HIDDEN, NSA_DH), CMP_HIDDEN ** -0.5),
        "w_out": nrm(ks[17], (DEPTH, D_MIX, D_MODEL), D_MIX ** -0.5),
        "ffn2_norm": gain(ks[18], (DEPTH, D_MODEL)),
        "ffn2_w_gate": nrm(ks[19], (DEPTH, D_MODEL, D_FF), D_MODEL ** -0.5),
        "ffn2_w_up": nrm(ks[20], (DEPTH, D_MODEL, D_FF), D_MODEL ** -0.5),
        "ffn2_w_down": nrm(ks[21], (DEPTH, D_FF, D_MODEL), D_FF ** -0.5),
        "final_norm": gain(ks[22], (D_MODEL,)),
    }


def reference(x, ffn1_norm, ffn1_w_gate, ffn1_w_up, ffn1_w_down, mix_norm, w_in,
              gdn_conv_w, gdn_a_log, gdn_dt_bias, gdn_out_norm,
              cmp_pe_k, cmp_w1_k, cmp_w2_k, cmp_pe_v, cmp_w1_v, cmp_w2_v, w_out,
              ffn2_norm, ffn2_w_gate, ffn2_w_up, ffn2_w_down, final_norm):
    for l in range(DEPTH):
        h = rmsnorm(x, ffn1_norm[l])
        x = x + 0.5 * swiglu(h, ffn1_w_gate[l], ffn1_w_up[l], ffn1_w_down[l])
        h = rmsnorm(x, mix_norm[l])
        x = x + hybrid_mixer(h, w_in[l], gdn_conv_w[l], gdn_a_log[l], gdn_dt_bias[l], gdn_out_norm[l],
                             cmp_pe_k[l], cmp_w1_k[l], cmp_w2_k[l], cmp_pe_v[l], cmp_w1_v[l], cmp_w2_v[l],
                             w_out[l])
        h = rmsnorm(x, ffn2_norm[l])
        x = x + 0.5 * swiglu(h, ffn2_w_gate[l], ffn2_w_up[l], ffn2_w_down[l])
    return rmsnorm(x, final_norm)
```

```python
import functools

import numpy as np
import jax
import jax.numpy as jnp
from jax import lax
from jax.experimental import pallas as pl
from jax.experimental.pallas import tpu as pltpu

F32 = jnp.float32
BF16 = jnp.bfloat16

D_MODEL = 1024
D_FF = 2816
GDN_HEADS = 4
GDN_DK = 128
GDN_DV = 128
GDN_CONV = 4
GDN_CHUNK = 64
NSA_HEADS = 8
NSA_KV = 2
NSA_HPG = NSA_HEADS // NSA_KV
NSA_DH = 64
CMP_BLOCK = 32
CMP_STRIDE = 16
CMP_HIDDEN = 128
SEL_BLOCK = 64
SEL_TOPK = 16
WINDOW = 512
ROPE_THETA = 500000.0
ROPE_DIM = NSA_DH // 4
EPS = 1e-6
NEG = -1e30
BIG = 1e30

W_GDN_QKV = GDN_HEADS * (2 * GDN_DK + GDN_DV)
W_GDN_Z = GDN_HEADS * GDN_DV
W_NSA_Q = NSA_HEADS * NSA_DH
W_NSA_KV = 2 * NSA_KV * NSA_DH
W_NSA_GATE = 3 * NSA_HEADS
W_MAIN = W_GDN_QKV + W_GDN_Z + W_NSA_Q + 3 * W_NSA_KV
LANES = 128
GATE_LANE0 = 2 * GDN_HEADS

TOKEN_TILE = 512
FF_CHUNK = D_FF // 2
NSA_TQ = 128
SLC_KT = 512
VMEM_LIMIT = 56 * 1024 * 1024


def _dot(a, b):
    return jnp.dot(a, b, preferred_element_type=F32)


def _dot_nt(a, b):
    return lax.dot_general(a, b, (((1,), (1,)), ((), ())), preferred_element_type=F32)


def _dot_tn(a, b):
    return lax.dot_general(a, b, (((0,), (0,)), ((), ())), preferred_element_type=F32)


def _dot_hi(a, b):
    return jnp.dot(a, b, precision=lax.Precision.HIGHEST, preferred_element_type=F32)


def _rms(x, w):
    return x * lax.rsqrt(jnp.mean(x * x, axis=-1, keepdims=True) + EPS) * w


def _silu(x):
    return x * jax.nn.sigmoid(x)


def _swiglu_half_step(x, nw, wg_ref, wu_ref, wd_ref):
    hb = _rms(x, nw).astype(BF16)
    acc = jnp.zeros_like(x)
    for c in range(D_FF // FF_CHUNK):
        sl = slice(c * FF_CHUNK, (c + 1) * FF_CHUNK)
        g = _dot(hb, wg_ref[:, sl])
        u = _dot(hb, wu_ref[:, sl])
        acc = acc + _dot((_silu(g) * u).astype(BF16), wd_ref[sl, :])
    return x + 0.5 * acc


def _ffn_kernel(x_ref, nw_ref, wg_ref, wu_ref, wd_ref, o_ref):
    o_ref[...] = _swiglu_half_step(x_ref[...], nw_ref[...], wg_ref, wu_ref, wd_ref)


def _const_spec(shape):
    nd = len(shape)
    return pl.BlockSpec(shape, lambda *_: (0,) * nd, pipeline_mode=pl.Buffered(1))


def _ffn1(x2d, nw, wg, wu, wd):
    t = x2d.shape[0]
    row = pl.BlockSpec((TOKEN_TILE, D_MODEL), lambda i: (i, 0))
    return pl.pallas_call(
        _ffn_kernel,
        grid=(t // TOKEN_TILE,),
        in_specs=[row, _const_spec((1, D_MODEL)), _const_spec((D_MODEL, D_FF)),
                  _const_spec((D_MODEL, D_FF)), _const_spec((D_FF, D_MODEL))],
        out_specs=row,
        out_shape=jax.ShapeDtypeStruct((t, D_MODEL), F32),
        compiler_params=pltpu.CompilerParams(dimension_semantics=("arbitrary",),
                                             vmem_limit_bytes=VMEM_LIMIT),
        name="ffn1",
    )(x2d, nw, wg, wu, wd)


def _rope(x, cos_t, sin_a, sin_b):
    return (x * cos_t + pltpu.roll(x, LANES - ROPE_DIM // 2, axis=1) * sin_a
            + pltpu.roll(x, ROPE_DIM // 2, axis=1) * sin_b)


def _inproj_kernel(x_ref, nw_ref, wm_ref, ws_ref, cos_ref, sa_ref, sb_ref,
                   qkv_ref, z_ref, small_ref, qp_ref, qr_ref, kvc_ref,
                   ks_ref, vs_ref, kw_ref, vw_ref):
    hb = _rms(x_ref[...], nw_ref[...]).astype(BF16)
    y = _dot(hb, wm_ref[...])
    small_ref[...] = _dot(hb, ws_ref[...])
    qkv_ref[...] = y[:, :W_GDN_QKV]
    o = W_GDN_QKV
    z_ref[...] = y[:, o:o + W_GDN_Z]
    o += W_GDN_Z
    cos_t, sin_a, sin_b = cos_ref[...], sa_ref[...], sb_ref[...]
    scale = NSA_DH ** -0.5
    for s in range(W_NSA_Q // LANES):
        slab = y[:, o + s * LANES:o + (s + 1) * LANES]
        plain = (slab * scale).astype(BF16)
        rot = (_rope(slab, cos_t, sin_a, sin_b) * scale).astype(BF16)
        for j in range(2):
            qp_ref[2 * s + j] = plain[:, j * NSA_DH:(j + 1) * NSA_DH]
            qr_ref[2 * s + j] = rot[:, j * NSA_DH:(j + 1) * NSA_DH]
    o += W_NSA_Q
    kvc_ref[...] = y[:, o:o + W_NSA_KV]
    o += W_NSA_KV
    for k_ref, v_ref in ((ks_ref, vs_ref), (kw_ref, vw_ref)):
        kr = _rope(y[:, o:o + LANES], cos_t, sin_a, sin_b).astype(BF16)
        vv = y[:, o + LANES:o + 2 * LANES].astype(BF16)
        for j in range(NSA_KV):
            k_ref[j] = kr[:, j * NSA_DH:(j + 1) * NSA_DH]
            v_ref[j] = vv[:, j * NSA_DH:(j + 1) * NSA_DH]
        o += W_NSA_KV


def _in_proj(x3d, nw, w_main, w_small, cos_t, sin_a, sin_b):
    b, s, _ = x3d.shape
    tm = TOKEN_TILE
    row = lambda w: pl.BlockSpec((None, tm, w), lambda bi, i: (bi, i, 0))
    heads = lambda n: pl.BlockSpec((None, n, tm, NSA_DH), lambda bi, i: (bi, 0, i, 0))
    tab = pl.BlockSpec((tm, LANES), lambda bi, i: (i, 0))
    out_shape = (
        jax.ShapeDtypeStruct((b, s, W_GDN_QKV), F32),
        jax.ShapeDtypeStruct((b, s, W_GDN_Z), F32),
        jax.ShapeDtypeStruct((b, s, LANES), F32),
        jax.ShapeDtypeStruct((b, NSA_HEADS, s, NSA_DH), BF16),
        jax.ShapeDtypeStruct((b, NSA_HEADS, s, NSA_DH), BF16),
        jax.ShapeDtypeStruct((b, s, W_NSA_KV), F32),
        jax.ShapeDtypeStruct((b, NSA_KV, s, NSA_DH), BF16),
        jax.ShapeDtypeStruct((b, NSA_KV, s, NSA_DH), BF16),
        jax.ShapeDtypeStruct((b, NSA_KV, s, NSA_DH), BF16),
        jax.ShapeDtypeStruct((b, NSA_KV, s, NSA_DH), BF16),
    )
    return pl.pallas_call(
        _inproj_kernel,
        grid=(b, s // tm),
        in_specs=[row(D_MODEL), _const_spec((1, D_MODEL)), _const_spec((D_MODEL, W_MAIN)),
                  _const_spec((D_MODEL, LANES)), tab, tab, tab],
        out_specs=(row(W_GDN_QKV), row(W_GDN_Z), row(LANES), heads(NSA_HEADS), heads(NSA_HEADS),
                   row(W_NSA_KV), heads(NSA_KV), heads(NSA_KV), heads(NSA_KV), heads(NSA_KV)),
        out_shape=out_shape,
        compiler_params=pltpu.CompilerParams(dimension_semantics=("arbitrary", "arbitrary"),
                                             vmem_limit_bytes=VMEM_LIMIT),
        name="in_proj",
    )(x3d, nw, w_main, w_small, cos_t, sin_a, sin_b)


def _unit_lower_inverse(low, ii, jj):
    bi, bj = ii // 16, jj // 16
    eye = (ii == jj).astype(F32)
    ld = jnp.where(bi == bj, low, 0.0)
    x = eye - ld
    p = _dot_hi(ld, ld)
    x = x + _dot_hi(x, p)
    p = _dot_hi(p, p)
    x = x + _dot_hi(x, p)
    p = _dot_hi(p, p)
    x = x + _dot_hi(x, p)
    c1 = jnp.where((bi // 2 == bj // 2) & (bi != bj), low, 0.0)
    x = x - _dot_hi(_dot_hi(x, c1), x)
    c2 = jnp.where(bi // 2 != bj // 2, low, 0.0)
    x = x - _dot_hi(_dot_hi(x, c2), x)
    return x


def _gdn_kernel(xq_ref, xk_ref, xv_ref, wq_ref, wk_ref, wv_ref, small_ref, z_ref,
                alog_ref, dt_ref, onorm_ref, o_ref,
                q_s, k_s, kb_s, vb_s, kbg_s, gc_s, u_s, w_s, qk_s, kd_s, out_s):
    h = pl.program_id(1)
    s_len = xq_ref.shape[0]
    c_len = GDN_CHUNK
    n_chunks = s_len // c_len
    row = lax.broadcasted_iota(jnp.int32, (s_len, LANES), 0)
    lane = lax.broadcasted_iota(jnp.int32, (s_len, LANES), 1)

    def conv_silu(x_ref, w_ref):
        x = x_ref[...]
        w = w_ref[...]
        acc = x * w[GDN_CONV - 1:GDN_CONV, :]
        for sh in range(1, GDN_CONV):
            xs = jnp.where(row >= sh, pltpu.roll(x, sh, axis=0), 0.0)
            acc = acc + xs * w[GDN_CONV - 1 - sh:GDN_CONV - sh, :]
        return _silu(acc)

    def l2n(x):
        return x * lax.rsqrt(jnp.sum(x * x, axis=-1, keepdims=True) + EPS)

    q = l2n(conv_silu(xq_ref, wq_ref)) * (GDN_DK ** -0.5)
    k = l2n(conv_silu(xk_ref, wk_ref))
    v = conv_silu(xv_ref, wv_ref)

    small = small_ref[...]
    lane1 = lax.broadcasted_iota(jnp.int32, (1, LANES), 1)
    pick1 = lambda r: jnp.sum(jnp.where(lane1 == h, r[...], 0.0), axis=-1, keepdims=True)
    a_col = jnp.sum(jnp.where(lane == h, small, 0.0), axis=-1, keepdims=True)
    b_col = jnp.sum(jnp.where(lane == GDN_HEADS + h, small, 0.0), axis=-1, keepdims=True)
    beta = jnp.broadcast_to(jax.nn.sigmoid(b_col), (s_len, LANES))
    xg = a_col + pick1(dt_ref)
    softplus = jnp.maximum(xg, 0.0) + jnp.log1p(jnp.exp(-jnp.abs(xg)))
    g = jnp.broadcast_to(-jnp.exp(pick1(alog_ref)) * softplus, (s_len, LANES))
    pos = row % c_len
    sh = 1
    while sh < c_len:
        g = g + jnp.where(pos >= sh, pltpu.roll(g, sh, axis=0), 0.0)
        sh *= 2
    gc = g
    gc3 = gc.reshape(n_chunks, c_len, LANES)
    gl = jnp.broadcast_to(gc3[:, c_len - 1:c_len, :], gc3.shape).reshape(s_len, LANES)
    eg = jnp.exp(gc)
    kb = k * beta
    q_s[...] = q
    k_s[...] = k
    kb_s[...] = kb
    vb_s[...] = v * beta
    kbg_s[...] = kb * eg
    gc_s[...] = gc
    kd_s[...] = k * jnp.exp(gl - gc)
    out_s[...] = q * eg

    ii = lax.broadcasted_iota(jnp.int32, (c_len, c_len), 0)
    jj = lax.broadcasted_iota(jnp.int32, (c_len, c_len), 1)
    tril = ii >= jj

    def pre(c, carry):
        rows = pl.ds(pl.multiple_of(c * c_len, c_len), c_len)
        gcc = gc_s[rows, :]
        diff = gcc[:, :c_len] - gcc.T[:c_len, :]
        decay = jnp.where(tril, jnp.exp(jnp.where(tril, diff, 0.0)), 0.0)
        k_c = k_s[rows, :].astype(BF16)
        kk = _dot_nt(kb_s[rows, :].astype(BF16), k_c)
        low = jnp.where(ii > jj, kk * decay, 0.0)
        t_b = _unit_lower_inverse(low, ii, jj).astype(BF16)
        u_s[rows, :] = _dot(t_b, vb_s[rows, :].astype(BF16))
        w_s[rows, :] = _dot(t_b, kbg_s[rows, :].astype(BF16))
        qk_s[rows, :] = jnp.where(tril, _dot_nt(q_s[rows, :].astype(BF16), k_c) * decay, 0.0)
        return carry

    lax.fori_loop(0, n_chunks, pre, 0)

    def step(c, state):
        rows = pl.ds(pl.multiple_of(c * c_len, c_len), c_len)
        sb = state.astype(BF16)
        v_new = u_s[rows, :] - _dot(w_s[rows, :].astype(BF16), sb)
        vnb = v_new.astype(BF16)
        o = _dot(out_s[rows, :].astype(BF16), sb) + _dot(qk_s[rows, :].astype(BF16), vnb)
        g_last = gc_s[pl.ds(c * c_len + c_len - 1, 1), :]
        state = state * jnp.exp(g_last) + _dot_tn(kd_s[rows, :].astype(BF16), vnb)
        out_s[rows, :] = o
        return state

    lax.fori_loop(0, n_chunks, step, jnp.zeros((GDN_DK, GDN_DV), F32))

    o_ref[...] = _rms(out_s[...], onorm_ref[...]) * _silu(z_ref[...])


def _gdn(qkv, z, small, conv_w, alog_p, dt_p, onorm):
    b, s, _ = qkv.shape
    hd = GDN_DK
    col = lambda off: pl.BlockSpec((None, s, hd), lambda bi, h: (bi, 0, off + h))
    wcol = lambda off: pl.BlockSpec((GDN_CONV, hd), lambda bi, h: (0, off + h))
    vec = pl.BlockSpec((1, LANES), lambda bi, h: (0, 0))
    seq = lambda w: pltpu.VMEM((s, w), F32)
    return pl.pallas_call(
        _gdn_kernel,
        grid=(b, GDN_HEADS),
        in_specs=[col(0), col(GDN_HEADS), col(2 * GDN_HEADS),
                  wcol(0), wcol(GDN_HEADS), wcol(2 * GDN_HEADS),
                  pl.BlockSpec((None, s, LANES), lambda bi, h: (bi, 0, 0)),
                  col(0), vec, vec, vec],
        out_specs=col(0),
        out_shape=jax.ShapeDtypeStruct((b, s, GDN_HEADS * GDN_DV), F32),
        scratch_shapes=[seq(hd)] * 8 + [seq(GDN_CHUNK), seq(hd), seq(hd)],
        compiler_params=pltpu.CompilerParams(dimension_semantics=("arbitrary", "arbitrary"),
                                             vmem_limit_bytes=VMEM_LIMIT),
        name="gdn",
    )(qkv, qkv, qkv, conv_w, conv_w, conv_w, small, z, alog_p, dt_p, onorm)


def _compress_kernel(t_ref, pet_ref, peb_ref, wt_ref, wb_ref, w2_ref, o_ref):
    t = t_ref[...]
    a = _dot((t + pet_ref[...]).astype(BF16), wt_ref[...])
    bm = _dot((t + peb_ref[...]).astype(BF16), wb_ref[...])
    n_rows = t.shape[0]
    hid = a + pltpu.roll(bm, n_rows - 1, axis=0)
    out = _dot(_silu(hid).astype(BF16), w2_ref[...])
    for j in range(2 * NSA_KV):
        o_ref[j] = out[:, j * NSA_DH:(j + 1) * NSA_DH]


def _compress(tm, pe_top, pe_bot, w_top, w_bot, w2):
    b, n_rows, width = tm.shape
    nh = 2 * NSA_KV * CMP_HIDDEN
    return pl.pallas_call(
        _compress_kernel,
        grid=(b,),
        in_specs=[pl.BlockSpec((None, n_rows, width), lambda bi: (bi, 0, 0)),
                  _const_spec((1, width)), _const_spec((1, width)),
                  _const_spec((width, nh)), _const_spec((width, nh)),
                  _const_spec((nh, 2 * NSA_KV * NSA_DH))],
        out_specs=pl.BlockSpec((None, 2 * NSA_KV, n_rows, NSA_DH), lambda bi: (bi, 0, 0, 0)),
        out_shape=jax.ShapeDtypeStruct((b, 2 * NSA_KV, n_rows, NSA_DH), F32),
        compiler_params=pltpu.CompilerParams(dimension_semantics=("arbitrary",),
                                             vmem_limit_bytes=VMEM_LIMIT),
        name="compress",
    )(tm, pe_top, pe_bot, w_top, w_bot, w2)


def _nsa_kernel(qp_ref, qr_ref, kc_ref, vc_ref, ks_ref, vs_ref, kw_ref, vw_ref, small_ref, o_ref,
                m_s, l_s, acc_s):
    grp = pl.program_id(1)
    q0 = pl.program_id(2) * NSA_TQ
    tq, hpg, dh = NSA_TQ, NSA_HPG, NSA_DH
    rows = hpg * tq
    n_cmp = kc_ref.shape[0]
    qp = qp_ref[...].reshape(rows, dh)
    qr = qr_ref[...].reshape(rows, dh)

    def softmax_rows(s3, mask):
        s3 = jnp.where(mask[None], s3, NEG)
        m = jnp.max(s3, axis=-1, keepdims=True)
        p = jnp.exp(s3 - m) * mask[None].astype(F32)
        return p / jnp.maximum(jnp.sum(p, axis=-1, keepdims=True), 1e-30)

    t_c = q0 + lax.broadcasted_iota(jnp.int32, (tq, n_cmp), 0)
    n_c = lax.broadcasted_iota(jnp.int32, (tq, n_cmp), 1)
    s_cmp = _dot_nt(qp, kc_ref[...].astype(BF16)).reshape(hpg, tq, n_cmp)
    p_cmp = softmax_rows(s_cmp, n_c * CMP_STRIDE + (CMP_BLOCK - 1) <= t_c)
    o_cmp = _dot(p_cmp.reshape(rows, n_cmp).astype(BF16), vc_ref[...].astype(BF16))

    p_sum = p_cmp[0]
    for hh in range(1, hpg):
        p_sum = p_sum + p_cmp[hh]
    on = lax.broadcasted_iota(jnp.int32, (n_cmp, LANES), 0) * CMP_STRIDE
    oj = lax.broadcasted_iota(jnp.int32, (n_cmp, LANES), 1) * SEL_BLOCK
    overlap = ((on < oj + SEL_BLOCK) & (on + CMP_BLOCK > oj)).astype(F32)
    imp = _dot_hi(p_sum, overlap)
    blk = lax.broadcasted_iota(jnp.int32, (tq, LANES), 1)
    cur = (q0 + lax.broadcasted_iota(jnp.int32, (tq, LANES), 0)) // SEL_BLOCK
    forced = (blk == 0) | (blk == cur) | (blk == cur - 1)
    score = jnp.where(blk <= cur, jnp.where(forced, BIG, imp), NEG)
    rank = jnp.zeros((tq, LANES), F32)
    for i in range(ks_ref.shape[0] // SEL_BLOCK):
        ci = jnp.broadcast_to(score[:, i:i + 1], (tq, LANES))
        beats = (ci > score) | ((ci == score) & (blk > i))
        rank = rank + beats.astype(F32)
    sel = (rank < SEL_TOPK).astype(BF16)

    m_s[...] = jnp.full(m_s.shape, NEG, F32)
    l_s[...] = jnp.zeros(l_s.shape, F32)
    acc_s[...] = jnp.zeros(acc_s.shape, F32)
    kt = SLC_KT
    ej = lax.broadcasted_iota(jnp.int32, (LANES, kt), 0)
    em = lax.broadcasted_iota(jnp.int32, (LANES, kt), 1) // SEL_BLOCK
    t_k = q0 + lax.broadcasted_iota(jnp.int32, (tq, kt), 0)
    lane_k = lax.broadcasted_iota(jnp.int32, (tq, kt), 1)

    def slc_step(c, carry):
        k0 = pl.multiple_of(c * kt, kt)
        s3 = _dot_nt(qr, ks_ref[pl.ds(k0, kt), :]).reshape(hpg, tq, kt)
        expand = (ej == em + c * (kt // SEL_BLOCK)).astype(BF16)
        chosen = _dot(sel, expand) > 0.5
        mask = chosen & (k0 + lane_k <= t_k)
        maskf = mask[None].astype(F32)
        s3 = jnp.where(mask[None], s3, NEG)
        m_old = m_s[...]
        m_new = jnp.maximum(m_old, jnp.max(s3, axis=-1, keepdims=True))
        p = jnp.exp(s3 - m_new) * maskf
        alpha = jnp.exp(m_old - m_new)
        l_s[...] = alpha * l_s[...] + jnp.sum(p, axis=-1, keepdims=True)
        pv = _dot(p.reshape(rows, kt).astype(BF16), vs_ref[pl.ds(k0, kt), :])
        acc_s[...] = alpha * acc_s[...] + pv.reshape(hpg, tq, dh)
        m_s[...] = m_new
        return carry

    lax.fori_loop(0, (q0 + tq - 1) // kt + 1, slc_step, 0)
    o_slc = acc_s[...] / jnp.maximum(l_s[...], 1e-30)

    wk = WINDOW + tq
    w0 = pl.multiple_of(jnp.maximum(q0 - WINDOW, 0), tq)
    t_w = q0 + lax.broadcasted_iota(jnp.int32, (tq, wk), 0)
    p_w = w0 + lax.broadcasted_iota(jnp.int32, (tq, wk), 1)
    s_win = _dot_nt(qr, kw_ref[pl.ds(w0, wk), :]).reshape(hpg, tq, wk)
    p_win = softmax_rows(s_win, (p_w <= t_w) & (t_w - p_w < WINDOW))
    o_win = _dot(p_win.reshape(rows, wk).astype(BF16), vw_ref[pl.ds(w0, wk), :])

    gates = jax.nn.sigmoid(small_ref[...])
    lane_g = lax.broadcasted_iota(jnp.int32, (tq, LANES), 1)
    o_cmp = o_cmp.reshape(hpg, tq, dh)
    o_win = o_win.reshape(hpg, tq, dh)
    for hh in range(hpg):
        base = GATE_LANE0 + (grp * hpg + hh) * 3
        gate = lambda r: jnp.sum(jnp.where(lane_g == base + r, gates, 0.0), axis=-1, keepdims=True)
        o_ref[hh] = (gate(0) * o_cmp[hh] + gate(1) * o_slc[hh] + gate(2) * o_win[hh]).astype(o_ref.dtype)


def _nsa_attention(qp, qr, kcvc, ks, vs, kw, vw, small):
    b, _, s, dh = qp.shape
    tq = NSA_TQ
    n_cmp = kcvc.shape[2]
    qspec = pl.BlockSpec((None, NSA_HPG, tq, dh), lambda bi, g, i: (bi, g, i, 0))
    cspec = lambda off: pl.BlockSpec((None, None, n_cmp, dh), lambda bi, g, i: (bi, off + g, 0, 0))
    kspec = pl.BlockSpec((None, None, s, dh), lambda bi, g, i: (bi, g, 0, 0))
    return pl.pallas_call(
        _nsa_kernel,
        grid=(b, NSA_KV, s // tq),
        in_specs=[qspec, qspec, cspec(0), cspec(NSA_KV), kspec, kspec, kspec, kspec,
                  pl.BlockSpec((None, tq, LANES), lambda bi, g, i: (bi, i, 0))],
        out_specs=qspec,
        out_shape=jax.ShapeDtypeStruct((b, NSA_HEADS, s, dh), BF16),
        scratch_shapes=[pltpu.VMEM((NSA_HPG, tq, 1), F32), pltpu.VMEM((NSA_HPG, tq, 1), F32),
                        pltpu.VMEM((NSA_HPG, tq, dh), F32)],
        compiler_params=pltpu.CompilerParams(
            dimension_semantics=("arbitrary", "arbitrary", "arbitrary"),
            vmem_limit_bytes=VMEM_LIMIT),
        name="nsa_attn",
    )(qp, qr, kcvc, kcvc, ks, vs, kw, vw, small)


def _out_ffn2_kernel(x_ref, og_ref, on_ref, wo_ref, nw_ref, wg_ref, wu_ref, wd_ref, fw_ref, o_ref):
    half = W_GDN_Z
    x = (x_ref[...] + _dot(og_ref[...].astype(BF16), wo_ref[:half, :])
         + _dot(on_ref[...], wo_ref[half:, :]))
    y = _swiglu_half_step(x, nw_ref[...], wg_ref, wu_ref, wd_ref)
    o_ref[...] = _rms(y, fw_ref[...])


def _out_ffn2(x2d, o_gdn, o_nsa, w_out, nw, wg, wu, wd, fw):
    t = x2d.shape[0]
    row = lambda w: pl.BlockSpec((TOKEN_TILE, w), lambda i: (i, 0))
    return pl.pallas_call(
        _out_ffn2_kernel,
        grid=(t // TOKEN_TILE,),
        in_specs=[row(D_MODEL), row(W_GDN_Z), row(W_NSA_Q), _const_spec((D_MODEL, D_MODEL)),
                  _const_spec((1, D_MODEL)), _const_spec((D_MODEL, D_FF)),
                  _const_spec((D_MODEL, D_FF)), _const_spec((D_FF, D_MODEL)),
                  _const_spec((1, D_MODEL))],
        out_specs=row(D_MODEL),
        out_shape=jax.ShapeDtypeStruct((t, D_MODEL), F32),
        compiler_params=pltpu.CompilerParams(dimension_semantics=("arbitrary",),
                                             vmem_limit_bytes=VMEM_LIMIT),
        name="out_ffn2",
    )(x2d, o_gdn, o_nsa, w_out, nw, wg, wu, wd, fw)


def _rope_tables(s_len):
    half = ROPE_DIM // 2
    inv = ROPE_THETA ** (-jnp.arange(half, dtype=F32) / half)
    ang = jnp.arange(s_len, dtype=F32)[:, None] * inv[None, :]
    cos, sin = jnp.cos(ang), jnp.sin(ang)
    pad = jnp.zeros((s_len, NSA_DH - ROPE_DIM), F32)
    zero = jnp.zeros_like(sin)
    cos_h = jnp.concatenate([cos, cos, pad + 1.0], axis=1)
    sa_h = jnp.concatenate([-sin, zero, pad], axis=1)
    sb_h = jnp.concatenate([zero, sin, pad], axis=1)
    rep = lambda a: jnp.tile(a, (1, LANES // NSA_DH))
    return rep(cos_h), rep(sa_h), rep(sb_h)


def _split_w_in(w_in):
    cuts = np.cumsum([W_GDN_QKV, W_GDN_Z, GDN_HEADS, GDN_HEADS, W_NSA_Q,
                      W_NSA_KV, W_NSA_KV, W_NSA_KV]).tolist()
    qkv, z, a, bb, nq, kvc, kvs, kvw, gate = jnp.split(w_in, cuts, axis=1)
    w_main = jnp.concatenate([qkv, z, nq, kvc, kvs, kvw], axis=1).astype(BF16)
    small = jnp.concatenate([a, bb, gate], axis=1)
    w_small = jnp.pad(small, ((0, 0), (0, LANES - small.shape[1]))).astype(BF16)
    return w_main, w_small


def _compress_weights(pe_k, w1_k, w2_k, pe_v, w1_v, w2_v):
    half = CMP_BLOCK // 2
    eye = jnp.eye(2 * NSA_KV, dtype=F32)

    def first_layer(lo):
        w1 = jnp.stack([w1_k, w1_v]).reshape(2, CMP_BLOCK, NSA_DH, CMP_HIDDEN)[:, lo:lo + half]
        w1 = jnp.repeat(w1, NSA_KV, axis=0)
        w = jnp.einsum('crdj,ce->rcdej', w1, eye)
        return w.reshape(half * 2 * NSA_KV * NSA_DH, 2 * NSA_KV * CMP_HIDDEN).astype(BF16)

    def pe_row(lo):
        pe = jnp.stack([pe_k, pe_v])[:, lo:lo + half]
        pe = jnp.repeat(pe, NSA_KV, axis=0)
        return pe.transpose(1, 0, 2).reshape(1, half * 2 * NSA_KV * NSA_DH)

    w2 = jnp.repeat(jnp.stack([w2_k, w2_v]), NSA_KV, axis=0)
    w2 = jnp.einsum('cjd,ce->cjed', w2, eye).reshape(2 * NSA_KV * CMP_HIDDEN, 2 * NSA_KV * NSA_DH)
    return pe_row(0), pe_row(half), first_layer(0), first_layer(half), w2.astype(BF16)


def kernel(x, ffn1_norm, ffn1_w_gate, ffn1_w_up, ffn1_w_down, mix_norm, w_in, gdn_conv_w, gdn_a_log,
           gdn_dt_bias, gdn_out_norm, cmp_pe_k, cmp_w1_k, cmp_w2_k, cmp_pe_v, cmp_w1_v, cmp_w2_v,
           w_out, ffn2_norm, ffn2_w_gate, ffn2_w_up, ffn2_w_down, final_norm):
    b, s, d = x.shape
    assert d == D_MODEL and s % TOKEN_TILE == 0 and ffn1_norm.shape[0] == 1
    vec = lambda a: a.reshape(1, -1)
    pad_lanes = lambda a: jnp.pad(a.reshape(1, -1), ((0, 0), (0, LANES - a.size)))
    bf = lambda a: a.astype(BF16)

    x1 = _ffn1(x.reshape(b * s, d), vec(ffn1_norm[0]), bf(ffn1_w_gate[0]), bf(ffn1_w_up[0]),
               bf(ffn1_w_down[0]))

    w_main, w_small = _split_w_in(w_in[0])
    cos_t, sin_a, sin_b = _rope_tables(s)
    qkv, z, small, qp, qr, kvc, ks, vs, kw, vw = _in_proj(
        x1.reshape(b, s, d), vec(mix_norm[0]), w_main, w_small, cos_t, sin_a, sin_b)

    o_gdn = _gdn(qkv, z, small, gdn_conv_w[0], pad_lanes(gdn_a_log[0]), pad_lanes(gdn_dt_bias[0]),
                 vec(gdn_out_norm[0]))

    pe_top, pe_bot, w_top, w_bot, w2 = _compress_weights(
        cmp_pe_k[0], cmp_w1_k[0], cmp_w2_k[0], cmp_pe_v[0], cmp_w1_v[0], cmp_w2_v[0])
    tokens16 = kvc.reshape(b, s // CMP_STRIDE, CMP_STRIDE * W_NSA_KV)
    kcvc = _compress(tokens16, pe_top, pe_bot, w_top, w_bot, w2)

    o_nsa = _nsa_attention(qp, qr, kcvc, ks, vs, kw, vw, small)
    o_nsa = o_nsa.transpose(0, 2, 1, 3).reshape(b * s, W_NSA_Q)

    out = _out_ffn2(x1, o_gdn.reshape(b * s, W_GDN_Z), o_nsa, bf(w_out[0]), vec(ffn2_norm[0]),
                    bf(ffn2_w_gate[0]), bf(ffn2_w_up[0]), bf(ffn2_w_down[0]), vec(final_norm))
    return out.reshape(b, s, d)
```

```python
import functools

import numpy as np
import jax
import jax.numpy as jnp
from jax import lax
from jax.experimental import pallas as pl
from jax.experimental.pallas import tpu as pltpu

F32 = jnp.float32
BF16 = jnp.bfloat16

D_MODEL = 1024
D_FF = 2816
GDN_HEADS = 4
GDN_DK = 128
GDN_DV = 128
GDN_CONV = 4
GDN_CHUNK = 64
NSA_HEADS = 8
NSA_KV = 2
NSA_HPG = NSA_HEADS // NSA_KV
NSA_DH = 64
CMP_BLOCK = 32
CMP_STRIDE = 16
CMP_HIDDEN = 128
SEL_BLOCK = 64
SEL_TOPK = 16
WINDOW = 512
ROPE_THETA = 500000.0
ROPE_DIM = NSA_DH // 4
EPS = 1e-6
NEG = -1e30
BIG = 1e30

W_GDN_QKV = GDN_HEADS * (2 * GDN_DK + GDN_DV)
W_GDN_Z = GDN_HEADS * GDN_DV
W_NSA_Q = NSA_HEADS * NSA_DH
W_NSA_KV = 2 * NSA_KV * NSA_DH
W_NSA_GATE = 3 * NSA_HEADS
W_MAIN = W_GDN_QKV + W_GDN_Z + W_NSA_Q + 3 * W_NSA_KV
LANES = 128
GATE_LANE0 = 2 * GDN_HEADS

TOKEN_TILE = 512
FF_CHUNK = D_FF // 2
GDN_HPS = 2
GDN_GROUP = 2 * GDN_CHUNK
NSA_TQ = 128
SLC_KT = 512
VMEM_LIMIT = 56 * 1024 * 1024


def _dot(a, b):
    return jnp.dot(a, b, preferred_element_type=F32)


def _dot_nt(a, b):
    return lax.dot_general(a, b, (((1,), (1,)), ((), ())), preferred_element_type=F32)


def _dot_tn(a, b):
    return lax.dot_general(a, b, (((0,), (0,)), ((), ())), preferred_element_type=F32)


def _dot_hi(a, b):
    return jnp.dot(a, b, precision=lax.Precision.HIGHEST, preferred_element_type=F32)


def _rms(x, w):
    return x * lax.rsqrt(jnp.mean(x * x, axis=-1, keepdims=True) + EPS) * w


def _silu(x):
    return x * jax.nn.sigmoid(x)


def _swiglu_half_step(x, nw, wg_ref, wu_ref, wd_ref):
    hb = _rms(x, nw).astype(BF16)
    acc = jnp.zeros_like(x)
    for c in range(D_FF // FF_CHUNK):
        sl = slice(c * FF_CHUNK, (c + 1) * FF_CHUNK)
        g = _dot(hb, wg_ref[:, sl])
        u = _dot(hb, wu_ref[:, sl])
        acc = acc + _dot((_silu(g) * u).astype(BF16), wd_ref[sl, :])
    return x + 0.5 * acc


def _ffn_kernel(x_ref, nw_ref, wg_ref, wu_ref, wd_ref, o_ref):
    o_ref[...] = _swiglu_half_step(x_ref[...], nw_ref[...], wg_ref, wu_ref, wd_ref)


def _const_spec(shape):
    nd = len(shape)
    return pl.BlockSpec(shape, lambda *_: (0,) * nd, pipeline_mode=pl.Buffered(1))


def _ffn1(x2d, nw, wg, wu, wd):
    t = x2d.shape[0]
    row = pl.BlockSpec((TOKEN_TILE, D_MODEL), lambda i: (i, 0))
    return pl.pallas_call(
        _ffn_kernel,
        grid=(t // TOKEN_TILE,),
        in_specs=[row, _const_spec((1, D_MODEL)), _const_spec((D_MODEL, D_FF)),
                  _const_spec((D_MODEL, D_FF)), _const_spec((D_FF, D_MODEL))],
        out_specs=row,
        out_shape=jax.ShapeDtypeStruct((t, D_MODEL), F32),
        compiler_params=pltpu.CompilerParams(dimension_semantics=("arbitrary",),
                                             vmem_limit_bytes=VMEM_LIMIT),
        name="ffn1",
    )(x2d, nw, wg, wu, wd)


def _rope(x, cos_t, sin_a, sin_b):
    return (x * cos_t + pltpu.roll(x, LANES - ROPE_DIM // 2, axis=1) * sin_a
            + pltpu.roll(x, ROPE_DIM // 2, axis=1) * sin_b)


def _inproj_kernel(x_ref, nw_ref, wm_ref, ws_ref, cos_ref, sa_ref, sb_ref,
                   qkv_ref, z_ref, small_ref, qp_ref, qr_ref, kvc_ref,
                   ks_ref, vs_ref, kw_ref, vw_ref):
    hb = _rms(x_ref[...], nw_ref[...]).astype(BF16)
    y = _dot(hb, wm_ref[...])
    small_ref[...] = _dot(hb, ws_ref[...])
    qkv_ref[...] = y[:, :W_GDN_QKV]
    o = W_GDN_QKV
    z_ref[...] = y[:, o:o + W_GDN_Z]
    o += W_GDN_Z
    cos_t, sin_a, sin_b = cos_ref[...], sa_ref[...], sb_ref[...]
    scale = NSA_DH ** -0.5
    for s in range(W_NSA_Q // LANES):
        slab = y[:, o + s * LANES:o + (s + 1) * LANES]
        plain = (slab * scale).astype(BF16)
        rot = (_rope(slab, cos_t, sin_a, sin_b) * scale).astype(BF16)
        for j in range(2):
            qp_ref[2 * s + j] = plain[:, j * NSA_DH:(j + 1) * NSA_DH]
            qr_ref[2 * s + j] = rot[:, j * NSA_DH:(j + 1) * NSA_DH]
    o += W_NSA_Q
    kvc_ref[...] = y[:, o:o + W_NSA_KV]
    o += W_NSA_KV
    for k_ref, v_ref in ((ks_ref, vs_ref), (kw_ref, vw_ref)):
        kr = _rope(y[:, o:o + LANES], cos_t, sin_a, sin_b).astype(BF16)
        vv = y[:, o + LANES:o + 2 * LANES].astype(BF16)
        for j in range(NSA_KV):
            k_ref[j] = kr[:, j * NSA_DH:(j + 1) * NSA_DH]
            v_ref[j] = vv[:, j * NSA_DH:(j + 1) * NSA_DH]
        o += W_NSA_KV


def _in_proj(x3d, nw, w_main, w_small, cos_t, sin_a, sin_b):
    b, s, _ = x3d.shape
    tm = TOKEN_TILE
    row = lambda w: pl.BlockSpec((None, tm, w), lambda bi, i: (bi, i, 0))
    heads = lambda n: pl.BlockSpec((None, n, tm, NSA_DH), lambda bi, i: (bi, 0, i, 0))
    tab = pl.BlockSpec((tm, LANES), lambda bi, i: (i, 0))
    out_shape = (
        jax.ShapeDtypeStruct((b, s, W_GDN_QKV), F32),
        jax.ShapeDtypeStruct((b, s, W_GDN_Z), F32),
        jax.ShapeDtypeStruct((b, s, LANES), F32),
        jax.ShapeDtypeStruct((b, NSA_HEADS, s, NSA_DH), BF16),
        jax.ShapeDtypeStruct((b, NSA_HEADS, s, NSA_DH), BF16),
        jax.ShapeDtypeStruct((b, s, W_NSA_KV), F32),
        jax.ShapeDtypeStruct((b, NSA_KV, s, NSA_DH), BF16),
        jax.ShapeDtypeStruct((b, NSA_KV, s, NSA_DH), BF16),
        jax.ShapeDtypeStruct((b, NSA_KV, s, NSA_DH), BF16),
        jax.ShapeDtypeStruct((b, NSA_KV, s, NSA_DH), BF16),
    )
    return pl.pallas_call(
        _inproj_kernel,
        grid=(b, s // tm),
        in_specs=[row(D_MODEL), _const_spec((1, D_MODEL)), _const_spec((D_MODEL, W_MAIN)),
                  _const_spec((D_MODEL, LANES)), tab, tab, tab],
        out_specs=(row(W_GDN_QKV), row(W_GDN_Z), row(LANES), heads(NSA_HEADS), heads(NSA_HEADS),
                   row(W_NSA_KV), heads(NSA_KV), heads(NSA_KV), heads(NSA_KV), heads(NSA_KV)),
        out_shape=out_shape,
        compiler_params=pltpu.CompilerParams(dimension_semantics=("arbitrary", "arbitrary"),
                                             vmem_limit_bytes=VMEM_LIMIT),
        name="in_proj",
    )(x3d, nw, w_main, w_small, cos_t, sin_a, sin_b)


def _bmm(a, b):
    return jnp.einsum('nij,njk->nik', a, b, preferred_element_type=F32)


def _bmm_nt(a, b):
    return jnp.einsum('nid,njd->nij', a, b, preferred_element_type=F32)


def _unit_lower_inverse_minus_eye(low, ii, jj):
    mm = lambda a, b: _bmm(a.astype(BF16), b.astype(BF16))
    same16 = (ii // 16) == (jj // 16)
    same32 = (ii // 32) == (jj // 32)
    ld = jnp.where(same16, low, 0.0)
    e = -ld
    p = mm(ld, ld)
    for level in range(3):
        e = e + p + mm(e, p)
        if level < 2:
            p = mm(p, p)
    for within in (same32 & jnp.logical_not(same16), jnp.logical_not(same32)):
        c = jnp.where(within, low, 0.0)
        y = c + mm(e, c)
        e = e - y - mm(y, e)
    return e


def _gdn_kernel(xq_ref, xk_ref, xv_ref, wq_ref, wk_ref, wv_ref, small_ref, z_ref,
                alog_ref, dt_ref, onorm_ref, o_ref,
                mp_s, n_s, qt_s, o0_s, egl_s, sall_s):
    hp = pl.program_id(1)
    s_len = xq_ref.shape[0]
    c_len = GDN_CHUNK
    grp = GDN_GROUP
    n_chunks = s_len // c_len
    n_grp = s_len // grp
    hd = GDN_DK
    row = lax.broadcasted_iota(jnp.int32, (s_len, LANES), 0)
    lane = lax.broadcasted_iota(jnp.int32, (s_len, LANES), 1)
    lane1 = lax.broadcasted_iota(jnp.int32, (1, LANES), 1)
    pos = row % c_len
    ii = lax.broadcasted_iota(jnp.int32, (grp, grp), 0)
    jj = lax.broadcasted_iota(jnp.int32, (grp, grp), 1)
    same_chunk = (ii // c_len) == (jj // c_len)
    tril = same_chunk & (ii >= jj)
    strict = same_chunk & (ii > jj)
    first = lax.broadcasted_iota(jnp.int32, (grp, hd), 0) < c_len
    small = small_ref[...]
    g3 = lambda a: a.reshape(n_grp, grp, hd)

    def conv_silu(x, w):
        acc = x * w[GDN_CONV - 1:GDN_CONV, :]
        for sh in range(1, GDN_CONV):
            xs = jnp.where(row >= sh, pltpu.roll(x, sh, axis=0), 0.0)
            acc = acc + xs * w[GDN_CONV - 1 - sh:GDN_CONV - sh, :]
        return _silu(acc)

    def l2n(x):
        return x * lax.rsqrt(jnp.sum(x * x, axis=-1, keepdims=True) + EPS)

    for j in range(GDN_HPS):
        h = hp * GDN_HPS + j
        cols = slice(j * hd, (j + 1) * hd)
        q = l2n(conv_silu(xq_ref[:, cols], wq_ref[:, cols])) * (GDN_DK ** -0.5)
        k = l2n(conv_silu(xk_ref[:, cols], wk_ref[:, cols]))
        v = conv_silu(xv_ref[:, cols], wv_ref[:, cols])

        pick1 = lambda r: jnp.sum(jnp.where(lane1 == h, r[...], 0.0), axis=-1, keepdims=True)
        a_col = jnp.sum(jnp.where(lane == h, small, 0.0), axis=-1, keepdims=True)
        b_col = jnp.sum(jnp.where(lane == GDN_HEADS + h, small, 0.0), axis=-1, keepdims=True)
        beta = jnp.broadcast_to(jax.nn.sigmoid(b_col), (s_len, LANES))
        xg = a_col + pick1(dt_ref)
        softplus = jnp.maximum(xg, 0.0) + jnp.log1p(jnp.exp(-jnp.abs(xg)))
        g = jnp.broadcast_to(-jnp.exp(pick1(alog_ref)) * softplus, (s_len, LANES))
        sh = 1
        while sh < c_len:
            g = g + jnp.where(pos >= sh, pltpu.roll(g, sh, axis=0), 0.0)
            sh *= 2
        gc = g
        gc3 = gc.reshape(n_chunks, c_len, LANES)
        gl3 = gc3[:, c_len - 1:c_len, :]
        egl_s[j] = jnp.exp(gl3)
        gl = jnp.broadcast_to(gl3, gc3.shape).reshape(s_len, LANES)
        eg = jnp.exp(gc)
        kb = k * beta
        vb = g3(v * beta)
        kbg = g3(kb * eg)

        k3 = g3(k).astype(BF16)
        kk = _bmm_nt(g3(kb).astype(BF16), k3)
        qk = _bmm_nt(g3(q).astype(BF16), k3)
        gcg = g3(gc)
        diff = gcg - jnp.swapaxes(gcg, 1, 2)
        decay = jnp.where(tril, jnp.exp(jnp.where(tril, diff, 0.0)), 0.0)
        e_b = _unit_lower_inverse_minus_eye(jnp.where(strict, kk * decay, 0.0), ii, jj).astype(BF16)
        u = vb + _bmm(e_b, vb.astype(BF16))
        w = kbg + _bmm(e_b, kbg.astype(BF16))
        qkm = jnp.where(tril, qk * decay, 0.0).astype(BF16)
        r = _bmm(qkm, jnp.concatenate([w, u], axis=-1).astype(BF16))
        qt_s[j] = (g3(q * eg) - r[..., :hd]).astype(BF16)
        o0_s[j] = r[..., hd:]
        kd_t = jnp.swapaxes(g3(k * jnp.exp(gl - gc)), 1, 2).astype(BF16)
        rhs = jnp.concatenate([jnp.where(first, w, 0.0), jnp.where(first, 0.0, w),
                               jnp.where(first, u, 0.0), jnp.where(first, 0.0, u)], axis=-1)
        mn = _bmm(kd_t, rhs.astype(BF16))
        mp_s[j] = (-mn[..., :2 * hd]).astype(BF16)
        n_s[j] = mn[..., 2 * hd:]

    def scan(n, states):
        out = []
        for j in range(GDN_HPS):
            st = states[j]
            for half in range(grp // c_len):
                c = n * (grp // c_len) + half
                lanes = slice(half * hd, (half + 1) * hd)
                sb = st.astype(BF16)
                sall_s[j, c] = sb
                st = egl_s[j, c] * st + _dot(mp_s[j, n, :, lanes], sb) + n_s[j, n, :, lanes]
            out.append(st)
        return tuple(out)

    lax.fori_loop(0, n_grp, scan, tuple(jnp.zeros((GDN_DK, GDN_DV), F32) for _ in range(GDN_HPS)))

    for j in range(GDN_HPS):
        cols = slice(j * hd, (j + 1) * hd)
        qt = qt_s[j].reshape(n_chunks, c_len, hd)
        o = _bmm(qt, sall_s[j]) + o0_s[j].reshape(n_chunks, c_len, hd)
        o_ref[:, cols] = _rms(o.reshape(s_len, hd), onorm_ref[...]) * _silu(z_ref[:, cols])


def _gdn(qkv, z, small, conv_w, alog_p, dt_p, onorm):
    b, s, _ = qkv.shape
    wid = GDN_HPS * GDN_DK
    n_hp = GDN_HEADS // GDN_HPS
    n_grp, n_chunks = s // GDN_GROUP, s // GDN_CHUNK
    col = lambda off: pl.BlockSpec((None, s, wid), lambda bi, h: (bi, 0, off + h))
    wcol = lambda off: pl.BlockSpec((GDN_CONV, wid), lambda bi, h: (0, off + h))
    vec = pl.BlockSpec((1, LANES), lambda bi, h: (0, 0))
    return pl.pallas_call(
        _gdn_kernel,
        grid=(b, n_hp),
        in_specs=[col(0), col(n_hp), col(2 * n_hp),
                  wcol(0), wcol(n_hp), wcol(2 * n_hp),
                  pl.BlockSpec((None, s, LANES), lambda bi, h: (bi, 0, 0)),
                  col(0), vec, vec, vec],
        out_specs=col(0),
        out_shape=jax.ShapeDtypeStruct((b, s, GDN_HEADS * GDN_DV), F32),
        scratch_shapes=[pltpu.VMEM((GDN_HPS, n_grp, GDN_DK, 2 * GDN_DV), BF16),
                        pltpu.VMEM((GDN_HPS, n_grp, GDN_DK, 2 * GDN_DV), F32),
                        pltpu.VMEM((GDN_HPS, n_grp, GDN_GROUP, GDN_DK), BF16),
                        pltpu.VMEM((GDN_HPS, n_grp, GDN_GROUP, GDN_DV), F32),
                        pltpu.VMEM((GDN_HPS, n_chunks, 1, LANES), F32),
                        pltpu.VMEM((GDN_HPS, n_chunks, GDN_DK, GDN_DV), BF16)],
        compiler_params=pltpu.CompilerParams(dimension_semantics=("arbitrary", "arbitrary"),
                                             vmem_limit_bytes=VMEM_LIMIT),
        name="gdn",
    )(qkv, qkv, qkv, conv_w, conv_w, conv_w, small, z, alog_p, dt_p, onorm)


def _compress_kernel(t_ref, pet_ref, peb_ref, wt_ref, wb_ref, w2_ref, o_ref):
    t = t_ref[...]
    a = _dot((t + pet_ref[...]).astype(BF16), wt_ref[...])
    bm = _dot((t + peb_ref[...]).astype(BF16), wb_ref[...])
    n_rows = t.shape[0]
    hid = a + pltpu.roll(bm, n_rows - 1, axis=0)
    out = _dot(_silu(hid).astype(BF16), w2_ref[...])
    for j in range(2 * NSA_KV):
        o_ref[j] = out[:, j * NSA_DH:(j + 1) * NSA_DH]


def _compress(tm, pe_top, pe_bot, w_top, w_bot, w2):
    b, n_rows, width = tm.shape
    nh = 2 * NSA_KV * CMP_HIDDEN
    return pl.pallas_call(
        _compress_kernel,
        grid=(b,),
        in_specs=[pl.BlockSpec((None, n_rows, width), lambda bi: (bi, 0, 0)),
                  _const_spec((1, width)), _const_spec((1, width)),
                  _const_spec((width, nh)), _const_spec((width, nh)),
                  _const_spec((nh, 2 * NSA_KV * NSA_DH))],
        out_specs=pl.BlockSpec((None, 2 * NSA_KV, n_rows, NSA_DH), lambda bi: (bi, 0, 0, 0)),
        out_shape=jax.ShapeDtypeStruct((b, 2 * NSA_KV, n_rows, NSA_DH), F32),
        compiler_params=pltpu.CompilerParams(dimension_semantics=("arbitrary",),
                                             vmem_limit_bytes=VMEM_LIMIT),
        name="compress",
    )(tm, pe_top, pe_bot, w_top, w_bot, w2)


def _nsa_kernel(qp_ref, qr_ref, kc_ref, vc_ref, ks_ref, vs_ref, kw_ref, vw_ref, small_ref, o_ref,
                m_s, l_s, acc_s):
    grp = pl.program_id(1)
    q0 = pl.program_id(2) * NSA_TQ
    tq, hpg, dh = NSA_TQ, NSA_HPG, NSA_DH
    rows = hpg * tq
    n_cmp = kc_ref.shape[0]
    qp = qp_ref[...].reshape(rows, dh)
    qr = qr_ref[...].reshape(rows, dh)

    def softmax_rows(s3, mask):
        s3 = jnp.where(mask[None], s3, NEG)
        m = jnp.max(s3, axis=-1, keepdims=True)
        p = jnp.exp(s3 - m) * mask[None].astype(F32)
        return p / jnp.maximum(jnp.sum(p, axis=-1, keepdims=True), 1e-30)

    t_c = q0 + lax.broadcasted_iota(jnp.int32, (tq, n_cmp), 0)
    n_c = lax.broadcasted_iota(jnp.int32, (tq, n_cmp), 1)
    s_cmp = _dot_nt(qp, kc_ref[...].astype(BF16)).reshape(hpg, tq, n_cmp)
    p_cmp = softmax_rows(s_cmp, n_c * CMP_STRIDE + (CMP_BLOCK - 1) <= t_c)
    o_cmp = _dot(p_cmp.reshape(rows, n_cmp).astype(BF16), vc_ref[...].astype(BF16))

    p_sum = p_cmp[0]
    for hh in range(1, hpg):
        p_sum = p_sum + p_cmp[hh]
    on = lax.broadcasted_iota(jnp.int32, (n_cmp, LANES), 0) * CMP_STRIDE
    oj = lax.broadcasted_iota(jnp.int32, (n_cmp, LANES), 1) * SEL_BLOCK
    overlap = ((on < oj + SEL_BLOCK) & (on + CMP_BLOCK > oj)).astype(F32)
    imp = _dot_hi(p_sum, overlap)
    blk = lax.broadcasted_iota(jnp.int32, (tq, LANES), 1)
    cur = (q0 + lax.broadcasted_iota(jnp.int32, (tq, LANES), 0)) // SEL_BLOCK
    forced = (blk == 0) | (blk == cur) | (blk == cur - 1)
    score = jnp.where(blk <= cur, jnp.where(forced, BIG, imp), NEG)
    rank = jnp.zeros((tq, LANES), F32)
    for i in range(ks_ref.shape[0] // SEL_BLOCK):
        ci = jnp.broadcast_to(score[:, i:i + 1], (tq, LANES))
        beats = (ci > score) | ((ci == score) & (blk > i))
        rank = rank + beats.astype(F32)
    sel = (rank < SEL_TOPK).astype(BF16)

    m_s[...] = jnp.full(m_s.shape, NEG, F32)
    l_s[...] = jnp.zeros(l_s.shape, F32)
    acc_s[...] = jnp.zeros(acc_s.shape, F32)
    kt = SLC_KT
    ej = lax.broadcasted_iota(jnp.int32, (LANES, kt), 0)
    em = lax.broadcasted_iota(jnp.int32, (LANES, kt), 1) // SEL_BLOCK
    t_k = q0 + lax.broadcasted_iota(jnp.int32, (tq, kt), 0)
    lane_k = lax.broadcasted_iota(jnp.int32, (tq, kt), 1)

    def slc_step(c, carry):
        k0 = pl.multiple_of(c * kt, kt)
        s3 = _dot_nt(qr, ks_ref[pl.ds(k0, kt), :]).reshape(hpg, tq, kt)
        expand = (ej == em + c * (kt // SEL_BLOCK)).astype(BF16)
        chosen = _dot(sel, expand) > 0.5
        mask = chosen & (k0 + lane_k <= t_k)
        maskf = mask[None].astype(F32)
        s3 = jnp.where(mask[None], s3, NEG)
        m_old = m_s[...]
        m_new = jnp.maximum(m_old, jnp.max(s3, axis=-1, keepdims=True))
        p = jnp.exp(s3 - m_new) * maskf
        alpha = jnp.exp(m_old - m_new)
        l_s[...] = alpha * l_s[...] + jnp.sum(p, axis=-1, keepdims=True)
        pv = _dot(p.reshape(rows, kt).astype(BF16), vs_ref[pl.ds(k0, kt), :])
        acc_s[...] = alpha * acc_s[...] + pv.reshape(hpg, tq, dh)
        m_s[...] = m_new
        return carry

    lax.fori_loop(0, (q0 + tq - 1) // kt + 1, slc_step, 0)
    o_slc = acc_s[...] / jnp.maximum(l_s[...], 1e-30)

    wk = WINDOW + tq
    w0 = pl.multiple_of(jnp.maximum(q0 - WINDOW, 0), tq)
    t_w = q0 + lax.broadcasted_iota(jnp.int32, (tq, wk), 0)
    p_w = w0 + lax.broadcasted_iota(jnp.int32, (tq, wk), 1)
    s_win = _dot_nt(qr, kw_ref[pl.ds(w0, wk), :]).reshape(hpg, tq, wk)
    p_win = softmax_rows(s_win, (p_w <= t_w) & (t_w - p_w < WINDOW))
    o_win = _dot(p_win.reshape(rows, wk).astype(BF16), vw_ref[pl.ds(w0, wk), :])

    gates = jax.nn.sigmoid(small_ref[...])
    lane_g = lax.broadcasted_iota(jnp.int32, (tq, LANES), 1)
    o_cmp = o_cmp.reshape(hpg, tq, dh)
    o_win = o_win.reshape(hpg, tq, dh)
    for hh in range(hpg):
        base = GATE_LANE0 + (grp * hpg + hh) * 3
        gate = lambda r: jnp.sum(jnp.where(lane_g == base + r, gates, 0.0), axis=-1, keepdims=True)
        o_ref[hh] = (gate(0) * o_cmp[hh] + gate(1) * o_slc[hh] + gate(2) * o_win[hh]).astype(o_ref.dtype)


def _nsa_attention(qp, qr, kcvc, ks, vs, kw, vw, small):
    b, _, s, dh = qp.shape
    tq = NSA_TQ
    n_cmp = kcvc.shape[2]
    qspec = pl.BlockSpec((None, NSA_HPG, tq, dh), lambda bi, g, i: (bi, g, i, 0))
    cspec = lambda off: pl.BlockSpec((None, None, n_cmp, dh), lambda bi, g, i: (bi, off + g, 0, 0))
    kspec = pl.BlockSpec((None, None, s, dh), lambda bi, g, i: (bi, g, 0, 0))
    return pl.pallas_call(
        _nsa_kernel,
        grid=(b, NSA_KV, s // tq),
        in_specs=[qspec, qspec, cspec(0), cspec(NSA_KV), kspec, kspec, kspec, kspec,
                  pl.BlockSpec((None, tq, LANES), lambda bi, g, i: (bi, i, 0))],
        out_specs=qspec,
        out_shape=jax.ShapeDtypeStruct((b, NSA_HEADS, s, dh), BF16),
        scratch_shapes=[pltpu.VMEM((NSA_HPG, tq, 1), F32), pltpu.VMEM((NSA_HPG, tq, 1), F32),
                        pltpu.VMEM((NSA_HPG, tq, dh), F32)],
        compiler_params=pltpu.CompilerParams(
            dimension_semantics=("arbitrary", "arbitrary", "arbitrary"),
            vmem_limit_bytes=VMEM_LIMIT),
        name="nsa_attn",
    )(qp, qr, kcvc, kcvc, ks, vs, kw, vw, small)


def _out_ffn2_kernel(x_ref, og_ref, on_ref, wo_ref, nw_ref, wg_ref, wu_ref, wd_ref, fw_ref, o_ref):
    half = W_GDN_Z
    x = (x_ref[...] + _dot(og_ref[...].astype(BF16), wo_ref[:half, :])
         + _dot(on_ref[...], wo_ref[half:, :]))
    y = _swiglu_half_step(x, nw_ref[...], wg_ref, wu_ref, wd_ref)
    o_ref[...] = _rms(y, fw_ref[...])


def _out_ffn2(x2d, o_gdn, o_nsa, w_out, nw, wg, wu, wd, fw):
    t = x2d.shape[0]
    row = lambda w: pl.BlockSpec((TOKEN_TILE, w), lambda i: (i, 0))
    return pl.pallas_call(
        _out_ffn2_kernel,
        grid=(t // TOKEN_TILE,),
        in_specs=[row(D_MODEL), row(W_GDN_Z), row(W_NSA_Q), _const_spec((D_MODEL, D_MODEL)),
                  _const_spec((1, D_MODEL)), _const_spec((D_MODEL, D_FF)),
                  _const_spec((D_MODEL, D_FF)), _const_spec((D_FF, D_MODEL)),
                  _const_spec((1, D_MODEL))],
        out_specs=row(D_MODEL),
        out_shape=jax.ShapeDtypeStruct((t, D_MODEL), F32),
        compiler_params=pltpu.CompilerParams(dimension_semantics=("arbitrary",),
                                             vmem_limit_bytes=VMEM_LIMIT),
        name="out_ffn2",
    )(x2d, o_gdn, o_nsa, w_out, nw, wg, wu, wd, fw)


def _rope_tables(s_len):
    half = ROPE_DIM // 2
    inv = ROPE_THETA ** (-jnp.arange(half, dtype=F32) / half)
    ang = jnp.arange(s_len, dtype=F32)[:, None] * inv[None, :]
    cos, sin = jnp.cos(ang), jnp.sin(ang)
    pad = jnp.zeros((s_len, NSA_DH - ROPE_DIM), F32)
    zero = jnp.zeros_like(sin)
    cos_h = jnp.concatenate([cos, cos, pad + 1.0], axis=1)
    sa_h = jnp.concatenate([-sin, zero, pad], axis=1)
    sb_h = jnp.concatenate([zero, sin, pad], axis=1)
    rep = lambda a: jnp.tile(a, (1, LANES // NSA_DH))
    return rep(cos_h), rep(sa_h), rep(sb_h)


def _split_w_in(w_in):
    cuts = np.cumsum([W_GDN_QKV, W_GDN_Z, GDN_HEADS, GDN_HEADS, W_NSA_Q,
                      W_NSA_KV, W_NSA_KV, W_NSA_KV]).tolist()
    qkv, z, a, bb, nq, kvc, kvs, kvw, gate = jnp.split(w_in, cuts, axis=1)
    w_main = jnp.concatenate([qkv, z, nq, kvc, kvs, kvw], axis=1).astype(BF16)
    small = jnp.concatenate([a, bb, gate], axis=1)
    w_small = jnp.pad(small, ((0, 0), (0, LANES - small.shape[1]))).astype(BF16)
    return w_main, w_small


def _compress_weights(pe_k, w1_k, w2_k, pe_v, w1_v, w2_v):
    half = CMP_BLOCK // 2
    eye = jnp.eye(2 * NSA_KV, dtype=F32)

    def first_layer(lo):
        w1 = jnp.stack([w1_k, w1_v]).reshape(2, CMP_BLOCK, NSA_DH, CMP_HIDDEN)[:, lo:lo + half]
        w1 = jnp.repeat(w1, NSA_KV, axis=0)
        w = jnp.einsum('crdj,ce->rcdej', w1, eye)
        return w.reshape(half * 2 * NSA_KV * NSA_DH, 2 * NSA_KV * CMP_HIDDEN).astype(BF16)

    def pe_row(lo):
        pe = jnp.stack([pe_k, pe_v])[:, lo:lo + half]
        pe = jnp.repeat(pe, NSA_KV, axis=0)
        return pe.transpose(1, 0, 2).reshape(1, half * 2 * NSA_KV * NSA_DH)

    w2 = jnp.repeat(jnp.stack([w2_k, w2_v]), NSA_KV, axis=0)
    w2 = jnp.einsum('cjd,ce->cjed', w2, eye).reshape(2 * NSA_KV * CMP_HIDDEN, 2 * NSA_KV * NSA_DH)
    return pe_row(0), pe_row(half), first_layer(0), first_layer(half), w2.astype(BF16)


def kernel(x, ffn1_norm, ffn1_w_gate, ffn1_w_up, ffn1_w_down, mix_norm, w_in, gdn_conv_w, gdn_a_log,
           gdn_dt_bias, gdn_out_norm, cmp_pe_k, cmp_w1_k, cmp_w2_k, cmp_pe_v, cmp_w1_v, cmp_w2_v,
           w_out, ffn2_norm, ffn2_w_gate, ffn2_w_up, ffn2_w_down, final_norm):
    b, s, d = x.shape
    assert d == D_MODEL and s % TOKEN_TILE == 0 and ffn1_norm.shape[0] == 1
    vec = lambda a: a.reshape(1, -1)
    pad_lanes = lambda a: jnp.pad(a.reshape(1, -1), ((0, 0), (0, LANES - a.size)))
    bf = lambda a: a.astype(BF16)

    x1 = _ffn1(x.reshape(b * s, d), vec(ffn1_norm[0]), bf(ffn1_w_gate[0]), bf(ffn1_w_up[0]),
               bf(ffn1_w_down[0]))

    w_main, w_small = _split_w_in(w_in[0])
    cos_t, sin_a, sin_b = _rope_tables(s)
    qkv, z, small, qp, qr, kvc, ks, vs, kw, vw = _in_proj(
        x1.reshape(b, s, d), vec(mix_norm[0]), w_main, w_small, cos_t, sin_a, sin_b)

    o_gdn = _gdn(qkv, z, small, gdn_conv_w[0], pad_lanes(gdn_a_log[0]), pad_lanes(gdn_dt_bias[0]),
                 vec(gdn_out_norm[0]))

    pe_top, pe_bot, w_top, w_bot, w2 = _compress_weights(
        cmp_pe_k[0], cmp_w1_k[0], cmp_w2_k[0], cmp_pe_v[0], cmp_w1_v[0], cmp_w2_v[0])
    tokens16 = kvc.reshape(b, s // CMP_STRIDE, CMP_STRIDE * W_NSA_KV)
    kcvc = _compress(tokens16, pe_top, pe_bot, w_top, w_bot, w2)

    o_nsa = _nsa_attention(qp, qr, kcvc, ks, vs, kw, vw, small)
    o_nsa = o_nsa.transpose(0, 2, 1, 3).reshape(b * s, W_NSA_Q)

    out = _out_ffn2(x1, o_gdn.reshape(b * s, W_GDN_Z), o_nsa, bf(w_out[0]), vec(ffn2_norm[0]),
                    bf(ffn2_w_gate[0]), bf(ffn2_w_up[0]), bf(ffn2_w_down[0]), vec(final_norm))
    return out.reshape(b, s, d)
```

```python
import functools

import numpy as np
import jax
import jax.numpy as jnp
from jax import lax
from jax.experimental import pallas as pl
from jax.experimental.pallas import tpu as pltpu

F32 = jnp.float32
BF16 = jnp.bfloat16

D_MODEL = 1024
D_FF = 2816
GDN_HEADS = 4
GDN_DK = 128
GDN_DV = 128
GDN_CONV = 4
GDN_CHUNK = 64
NSA_HEADS = 8
NSA_KV = 2
NSA_HPG = NSA_HEADS // NSA_KV
NSA_DH = 64
CMP_BLOCK = 32
CMP_STRIDE = 16
CMP_HIDDEN = 128
SEL_BLOCK = 64
SEL_TOPK = 16
WINDOW = 512
ROPE_THETA = 500000.0
ROPE_DIM = NSA_DH // 4
EPS = 1e-6
NEG = -1e30
BIG = 1e30

W_GDN_QKV = GDN_HEADS * (2 * GDN_DK + GDN_DV)
W_GDN_Z = GDN_HEADS * GDN_DV
W_NSA_Q = NSA_HEADS * NSA_DH
W_NSA_KV = 2 * NSA_KV * NSA_DH
W_NSA_GATE = 3 * NSA_HEADS
W_MAIN = W_GDN_QKV + W_GDN_Z + W_NSA_Q + 3 * W_NSA_KV
LANES = 128
GATE_LANE0 = 2 * GDN_HEADS

TOKEN_TILE = 512
FF_CHUNK = D_FF // 2
GDN_HPS = 2
GDN_GROUP = 2 * GDN_CHUNK
NSA_TQ = 128
SLC_KT = 512
VMEM_LIMIT = 56 * 1024 * 1024


def _dot(a, b):
    return jnp.dot(a, b, preferred_element_type=F32)


def _dot_nt(a, b):
    return lax.dot_general(a, b, (((1,), (1,)), ((), ())), preferred_element_type=F32)


def _dot_tn(a, b):
    return lax.dot_general(a, b, (((0,), (0,)), ((), ())), preferred_element_type=F32)


def _dot_hi(a, b):
    return jnp.dot(a, b, precision=lax.Precision.HIGHEST, preferred_element_type=F32)


def _rms(x, w):
    return x * lax.rsqrt(jnp.mean(x * x, axis=-1, keepdims=True) + EPS) * w


def _silu(x):
    return x * jax.nn.sigmoid(x)


def _swiglu_half_step(x, nw, wg_ref, wu_ref, wd_ref):
    hb = _rms(x, nw).astype(BF16)
    acc = jnp.zeros_like(x)
    for c in range(D_FF // FF_CHUNK):
        sl = slice(c * FF_CHUNK, (c + 1) * FF_CHUNK)
        g = _dot(hb, wg_ref[:, sl])
        u = _dot(hb, wu_ref[:, sl])
        acc = acc + _dot((_silu(g) * u).astype(BF16), wd_ref[sl, :])
    return x + 0.5 * acc


def _ffn_kernel(x_ref, nw_ref, wg_ref, wu_ref, wd_ref, o_ref):
    o_ref[...] = _swiglu_half_step(x_ref[...], nw_ref[...], wg_ref, wu_ref, wd_ref)


def _const_spec(shape):
    nd = len(shape)
    return pl.BlockSpec(shape, lambda *_: (0,) * nd, pipeline_mode=pl.Buffered(1))


def _ffn1(x2d, nw, wg, wu, wd):
    t = x2d.shape[0]
    row = pl.BlockSpec((TOKEN_TILE, D_MODEL), lambda i: (i, 0))
    return pl.pallas_call(
        _ffn_kernel,
        grid=(t // TOKEN_TILE,),
        in_specs=[row, _const_spec((1, D_MODEL)), _const_spec((D_MODEL, D_FF)),
                  _const_spec((D_MODEL, D_FF)), _const_spec((D_FF, D_MODEL))],
        out_specs=row,
        out_shape=jax.ShapeDtypeStruct((t, D_MODEL), F32),
        compiler_params=pltpu.CompilerParams(dimension_semantics=("arbitrary",),
                                             vmem_limit_bytes=VMEM_LIMIT),
        name="ffn1",
    )(x2d, nw, wg, wu, wd)


def _rope(x, cos_t, sin_a, sin_b):
    return (x * cos_t + pltpu.roll(x, LANES - ROPE_DIM // 2, axis=1) * sin_a
            + pltpu.roll(x, ROPE_DIM // 2, axis=1) * sin_b)


def _inproj_kernel(x_ref, nw_ref, wm_ref, ws_ref, cos_ref, sa_ref, sb_ref,
                   qkv_ref, z_ref, small_ref, qp_ref, qr_ref, kvc_ref,
                   ks_ref, vs_ref, kw_ref, vw_ref):
    hb = _rms(x_ref[...], nw_ref[...]).astype(BF16)
    y = _dot(hb, wm_ref[...])
    small_ref[...] = _dot(hb, ws_ref[...])
    qkv_ref[...] = y[:, :W_GDN_QKV]
    o = W_GDN_QKV
    z_ref[...] = y[:, o:o + W_GDN_Z]
    o += W_GDN_Z
    cos_t, sin_a, sin_b = cos_ref[...], sa_ref[...], sb_ref[...]
    scale = NSA_DH ** -0.5
    tm = x_ref.shape[0]
    lane = lax.broadcasted_iota(jnp.int32, (tm, LANES), 1)
    pos = pl.program_id(1) * tm + lax.broadcasted_iota(jnp.int32, (tm, LANES), 0)
    low_half = lane < NSA_DH
    block_onehot = jnp.where(lane - NSA_DH == pos // SEL_BLOCK, 1.0, 0.0)
    ones_col = jnp.where(lane == NSA_DH, 1.0, 0.0)

    def halves(slab):
        return slab, pltpu.roll(slab, NSA_DH, axis=1)

    for s in range(W_NSA_Q // LANES):
        slab = y[:, o + s * LANES:o + (s + 1) * LANES]
        plain = (slab * scale).astype(BF16)
        for j, rot in enumerate(halves(_rope(slab, cos_t, sin_a, sin_b) * scale)):
            qp_ref[2 * s + j] = plain[:, j * NSA_DH:(j + 1) * NSA_DH]
            qr_ref[2 * s + j] = jnp.where(low_half, rot, 0.0).astype(BF16)
    o += W_NSA_Q
    kvc_ref[...] = y[:, o:o + W_NSA_KV]
    o += W_NSA_KV
    for k_ref, v_ref, k_pad in ((ks_ref, vs_ref, block_onehot), (kw_ref, vw_ref, 0.0)):
        kr = halves(_rope(y[:, o:o + LANES], cos_t, sin_a, sin_b))
        vv = halves(y[:, o + LANES:o + 2 * LANES])
        for j in range(NSA_KV):
            k_ref[j] = jnp.where(low_half, kr[j], k_pad).astype(BF16)
            v_ref[j] = jnp.where(low_half, vv[j], ones_col).astype(BF16)
        o += W_NSA_KV


def _in_proj(x3d, nw, w_main, w_small, cos_t, sin_a, sin_b):
    b, s, _ = x3d.shape
    tm = TOKEN_TILE
    row = lambda w: pl.BlockSpec((None, tm, w), lambda bi, i: (bi, i, 0))
    heads = lambda n, w: pl.BlockSpec((None, n, tm, w), lambda bi, i: (bi, 0, i, 0))
    tab = pl.BlockSpec((tm, LANES), lambda bi, i: (i, 0))
    out_shape = (
        jax.ShapeDtypeStruct((b, s, W_GDN_QKV), F32),
        jax.ShapeDtypeStruct((b, s, W_GDN_Z), F32),
        jax.ShapeDtypeStruct((b, s, LANES), F32),
        jax.ShapeDtypeStruct((b, NSA_HEADS, s, NSA_DH), BF16),
        jax.ShapeDtypeStruct((b, NSA_HEADS, s, LANES), BF16),
        jax.ShapeDtypeStruct((b, s, W_NSA_KV), F32),
        jax.ShapeDtypeStruct((b, NSA_KV, s, LANES), BF16),
        jax.ShapeDtypeStruct((b, NSA_KV, s, LANES), BF16),
        jax.ShapeDtypeStruct((b, NSA_KV, s, LANES), BF16),
        jax.ShapeDtypeStruct((b, NSA_KV, s, LANES), BF16),
    )
    kv_aug = heads(NSA_KV, LANES)
    return pl.pallas_call(
        _inproj_kernel,
        grid=(b, s // tm),
        in_specs=[row(D_MODEL), _const_spec((1, D_MODEL)), _const_spec((D_MODEL, W_MAIN)),
                  _const_spec((D_MODEL, LANES)), tab, tab, tab],
        out_specs=(row(W_GDN_QKV), row(W_GDN_Z), row(LANES), heads(NSA_HEADS, NSA_DH),
                   heads(NSA_HEADS, LANES), row(W_NSA_KV), kv_aug, kv_aug, kv_aug, kv_aug),
        out_shape=out_shape,
        compiler_params=pltpu.CompilerParams(dimension_semantics=("arbitrary", "arbitrary"),
                                             vmem_limit_bytes=VMEM_LIMIT),
        name="in_proj",
    )(x3d, nw, w_main, w_small, cos_t, sin_a, sin_b)


def _bmm(a, b):
    return jnp.einsum('nij,njk->nik', a, b, preferred_element_type=F32)


def _bmm_nt(a, b):
    return jnp.einsum('nid,njd->nij', a, b, preferred_element_type=F32)


def _unit_lower_inverse_minus_eye(low, ii, jj):
    mm = lambda a, b: _bmm(a.astype(BF16), b.astype(BF16))
    same16 = (ii // 16) == (jj // 16)
    same32 = (ii // 32) == (jj // 32)
    ld = jnp.where(same16, low, 0.0)
    e = -ld
    p = mm(ld, ld)
    for level in range(3):
        e = e + p + mm(e, p)
        if level < 2:
            p = mm(p, p)
    for within in (same32 & jnp.logical_not(same16), jnp.logical_not(same32)):
        c = jnp.where(within, low, 0.0)
        y = c + mm(e, c)
        e = e - y - mm(y, e)
    return e


def _gdn_kernel(xq_ref, xk_ref, xv_ref, wq_ref, wk_ref, wv_ref, small_ref, z_ref,
                alog_ref, dt_ref, onorm_ref, o_ref,
                mp_s, n_s, qt_s, o0_s, egl_s, sall_s):
    hp = pl.program_id(1)
    s_len = xq_ref.shape[0]
    c_len = GDN_CHUNK
    grp = GDN_GROUP
    n_chunks = s_len // c_len
    n_grp = s_len // grp
    hd = GDN_DK
    row = lax.broadcasted_iota(jnp.int32, (s_len, LANES), 0)
    lane = lax.broadcasted_iota(jnp.int32, (s_len, LANES), 1)
    lane1 = lax.broadcasted_iota(jnp.int32, (1, LANES), 1)
    pos = row % c_len
    ii = lax.broadcasted_iota(jnp.int32, (grp, grp), 0)
    jj = lax.broadcasted_iota(jnp.int32, (grp, grp), 1)
    same_chunk = (ii // c_len) == (jj // c_len)
    tril = same_chunk & (ii >= jj)
    strict = same_chunk & (ii > jj)
    first = lax.broadcasted_iota(jnp.int32, (grp, hd), 0) < c_len
    small = small_ref[...]
    g3 = lambda a: a.reshape(n_grp, grp, hd)

    def conv_silu(x, w):
        acc = x * w[GDN_CONV - 1:GDN_CONV, :]
        for sh in range(1, GDN_CONV):
            xs = jnp.where(row >= sh, pltpu.roll(x, sh, axis=0), 0.0)
            acc = acc + xs * w[GDN_CONV - 1 - sh:GDN_CONV - sh, :]
        return _silu(acc)

    def l2n(x):
        return x * lax.rsqrt(jnp.sum(x * x, axis=-1, keepdims=True) + EPS)

    for j in range(GDN_HPS):
        h = hp * GDN_HPS + j
        cols = slice(j * hd, (j + 1) * hd)
        q = l2n(conv_silu(xq_ref[:, cols], wq_ref[:, cols])) * (GDN_DK ** -0.5)
        k = l2n(conv_silu(xk_ref[:, cols], wk_ref[:, cols]))
        v = conv_silu(xv_ref[:, cols], wv_ref[:, cols])

        pick1 = lambda r: jnp.sum(jnp.where(lane1 == h, r[...], 0.0), axis=-1, keepdims=True)
        a_col = jnp.sum(jnp.where(lane == h, small, 0.0), axis=-1, keepdims=True)
        b_col = jnp.sum(jnp.where(lane == GDN_HEADS + h, small, 0.0), axis=-1, keepdims=True)
        beta = jnp.broadcast_to(jax.nn.sigmoid(b_col), (s_len, LANES))
        xg = a_col + pick1(dt_ref)
        softplus = jnp.maximum(xg, 0.0) + jnp.log1p(jnp.exp(-jnp.abs(xg)))
        g = jnp.broadcast_to(-jnp.exp(pick1(alog_ref)) * softplus, (s_len, LANES))
        sh = 1
        while sh < c_len:
            g = g + jnp.where(pos >= sh, pltpu.roll(g, sh, axis=0), 0.0)
            sh *= 2
        gc = g
        gc3 = gc.reshape(n_chunks, c_len, LANES)
        gl3 = gc3[:, c_len - 1:c_len, :]
        egl_s[j] = jnp.exp(gl3)
        gl = jnp.broadcast_to(gl3, gc3.shape).reshape(s_len, LANES)
        eg = jnp.exp(gc)
        kb = k * beta
        vb = g3(v * beta)
        kbg = g3(kb * eg)

        k3 = g3(k).astype(BF16)
        kk = _bmm_nt(g3(kb).astype(BF16), k3)
        qk = _bmm_nt(g3(q).astype(BF16), k3)
        gcg = g3(gc)
        diff = gcg - jnp.swapaxes(gcg, 1, 2)
        decay = jnp.where(tril, jnp.exp(jnp.where(tril, diff, 0.0)), 0.0)
        e_b = _unit_lower_inverse_minus_eye(jnp.where(strict, kk * decay, 0.0), ii, jj).astype(BF16)
        u = vb + _bmm(e_b, vb.astype(BF16))
        w = kbg + _bmm(e_b, kbg.astype(BF16))
        qkm = jnp.where(tril, qk * decay, 0.0).astype(BF16)
        r = _bmm(qkm, jnp.concatenate([w, u], axis=-1).astype(BF16))
        qt_s[j] = (g3(q * eg) - r[..., :hd]).astype(BF16)
        o0_s[j] = r[..., hd:]
        kd_t = jnp.swapaxes(g3(k * jnp.exp(gl - gc)), 1, 2).astype(BF16)
        rhs = jnp.concatenate([jnp.where(first, w, 0.0), jnp.where(first, 0.0, w),
                               jnp.where(first, u, 0.0), jnp.where(first, 0.0, u)], axis=-1)
        mn = _bmm(kd_t, rhs.astype(BF16))
        mp_s[j] = (-mn[..., :2 * hd]).astype(BF16)
        n_s[j] = mn[..., 2 * hd:]

    def scan(n, states):
        out = []
        for j in range(GDN_HPS):
            st = states[j]
            for half in range(grp // c_len):
                c = n * (grp // c_len) + half
                lanes = slice(half * hd, (half + 1) * hd)
                sb = st.astype(BF16)
                sall_s[j, c] = sb
                st = egl_s[j, c] * st + _dot(mp_s[j, n, :, lanes], sb) + n_s[j, n, :, lanes]
            out.append(st)
        return tuple(out)

    lax.fori_loop(0, n_grp, scan, tuple(jnp.zeros((GDN_DK, GDN_DV), F32) for _ in range(GDN_HPS)))

    for j in range(GDN_HPS):
        cols = slice(j * hd, (j + 1) * hd)
        qt = qt_s[j].reshape(n_chunks, c_len, hd)
        o = _bmm(qt, sall_s[j]) + o0_s[j].reshape(n_chunks, c_len, hd)
        o_ref[:, cols] = _rms(o.reshape(s_len, hd), onorm_ref[...]) * _silu(z_ref[:, cols])


def _gdn(qkv, z, small, conv_w, alog_p, dt_p, onorm):
    b, s, _ = qkv.shape
    wid = GDN_HPS * GDN_DK
    n_hp = GDN_HEADS // GDN_HPS
    n_grp, n_chunks = s // GDN_GROUP, s // GDN_CHUNK
    col = lambda off: pl.BlockSpec((None, s, wid), lambda bi, h: (bi, 0, off + h))
    wcol = lambda off: pl.BlockSpec((GDN_CONV, wid), lambda bi, h: (0, off + h))
    vec = pl.BlockSpec((1, LANES), lambda bi, h: (0, 0))
    return pl.pallas_call(
        _gdn_kernel,
        grid=(b, n_hp),
        in_specs=[col(0), col(n_hp), col(2 * n_hp),
                  wcol(0), wcol(n_hp), wcol(2 * n_hp),
                  pl.BlockSpec((None, s, LANES), lambda bi, h: (bi, 0, 0)),
                  col(0), vec, vec, vec],
        out_specs=col(0),
        out_shape=jax.ShapeDtypeStruct((b, s, GDN_HEADS * GDN_DV), F32),
        scratch_shapes=[pltpu.VMEM((GDN_HPS, n_grp, GDN_DK, 2 * GDN_DV), BF16),
                        pltpu.VMEM((GDN_HPS, n_grp, GDN_DK, 2 * GDN_DV), F32),
                        pltpu.VMEM((GDN_HPS, n_grp, GDN_GROUP, GDN_DK), BF16),
                        pltpu.VMEM((GDN_HPS, n_grp, GDN_GROUP, GDN_DV), F32),
                        pltpu.VMEM((GDN_HPS, n_chunks, 1, LANES), F32),
                        pltpu.VMEM((GDN_HPS, n_chunks, GDN_DK, GDN_DV), BF16)],
        compiler_params=pltpu.CompilerParams(dimension_semantics=("arbitrary", "arbitrary"),
                                             vmem_limit_bytes=VMEM_LIMIT),
        name="gdn",
    )(qkv, qkv, qkv, conv_w, conv_w, conv_w, small, z, alog_p, dt_p, onorm)


def _compress_kernel(t_ref, pet_ref, peb_ref, wt_ref, wb_ref, w2_ref, o_ref):
    t = t_ref[...]
    a = _dot((t + pet_ref[...]).astype(BF16), wt_ref[...])
    bm = _dot((t + peb_ref[...]).astype(BF16), wb_ref[...])
    n_rows = t.shape[0]
    hid = a + pltpu.roll(bm, n_rows - 1, axis=0)
    out = _dot(_silu(hid).astype(BF16), w2_ref[...])
    for j in range(2 * NSA_KV):
        o_ref[j] = out[:, j * NSA_DH:(j + 1) * NSA_DH]


def _compress(tm, pe_top, pe_bot, w_top, w_bot, w2):
    b, n_rows, width = tm.shape
    nh = 2 * NSA_KV * CMP_HIDDEN
    return pl.pallas_call(
        _compress_kernel,
        grid=(b,),
        in_specs=[pl.BlockSpec((None, n_rows, width), lambda bi: (bi, 0, 0)),
                  _const_spec((1, width)), _const_spec((1, width)),
                  _const_spec((width, nh)), _const_spec((width, nh)),
                  _const_spec((nh, 2 * NSA_KV * NSA_DH))],
        out_specs=pl.BlockSpec((None, 2 * NSA_KV, n_rows, NSA_DH), lambda bi: (bi, 0, 0, 0)),
        out_shape=jax.ShapeDtypeStruct((b, 2 * NSA_KV, n_rows, NSA_DH), F32),
        compiler_params=pltpu.CompilerParams(dimension_semantics=("arbitrary",),
                                             vmem_limit_bytes=VMEM_LIMIT),
        name="compress",
    )(tm, pe_top, pe_bot, w_top, w_bot, w2)


def _nsa_kernel(qp_ref, qr_ref, kc_ref, vc_ref, ks_ref, vs_ref, kw_ref, vw_ref, small_ref, o_ref,
                acc_s):
    grp = pl.program_id(1)
    q0 = pl.program_id(2) * NSA_TQ
    tq, hpg, dh = NSA_TQ, NSA_HPG, NSA_DH
    rows = hpg * tq
    n_cmp = kc_ref.shape[0]
    n_blk = ks_ref.shape[0] // SEL_BLOCK
    qp = qp_ref[...].reshape(rows, dh)

    t_c = q0 + lax.broadcasted_iota(jnp.int32, (tq, n_cmp), 0)
    n_c = lax.broadcasted_iota(jnp.int32, (tq, n_cmp), 1)
    vis = n_c * CMP_STRIDE + (CMP_BLOCK - 1) <= t_c
    s_cmp = _dot_nt(qp, kc_ref[...].astype(BF16)).reshape(hpg, tq, n_cmp)
    s_cmp = jnp.where(vis[None], s_cmp, NEG)
    p_cmp = jnp.exp(s_cmp - jnp.max(s_cmp, axis=-1, keepdims=True)) * vis[None].astype(F32)
    p_cmp = p_cmp * (1.0 / jnp.maximum(jnp.sum(p_cmp, axis=-1, keepdims=True), 1e-30))
    o_cmp = _dot(p_cmp.reshape(rows, n_cmp).astype(BF16), vc_ref[...].astype(BF16))

    p_sum = p_cmp[0]
    for hh in range(1, hpg):
        p_sum = p_sum + p_cmp[hh]
    oj = lax.broadcasted_iota(jnp.int32, (n_blk, n_cmp), 0) * SEL_BLOCK
    on = lax.broadcasted_iota(jnp.int32, (n_blk, n_cmp), 1) * CMP_STRIDE
    overlap_t = jnp.where(on < oj + SEL_BLOCK, jnp.where(on + CMP_BLOCK > oj, 1.0, 0.0), 0.0)
    imp_t = lax.dot_general(overlap_t, p_sum, (((1,), (1,)), ((), ())),
                            precision=lax.Precision.HIGHEST, preferred_element_type=F32)
    blk = lax.broadcasted_iota(jnp.int32, (n_blk, tq), 0)
    ahead = (q0 + lax.broadcasted_iota(jnp.int32, (n_blk, tq), 1)) // SEL_BLOCK - blk
    score = jnp.where(ahead < 0, NEG, jnp.where(blk == 0, BIG, jnp.where(ahead <= 1, BIG, imp_t)))
    rank = jnp.zeros((n_blk, tq), F32)
    for i in range(n_blk):
        ci = jnp.broadcast_to(score[i:i + 1, :], (n_blk, tq))
        tie = jnp.where(blk > i, 1.0, 0.0)
        rank = rank + jnp.where(ci > score, 1.0, jnp.where(ci == score, tie, 0.0))
    unselected = jnp.where(rank >= SEL_TOPK, 1.0, 0.0).astype(BF16)
    place = (lax.broadcasted_iota(jnp.int32, (n_blk, LANES), 1) - NSA_DH
             == lax.broadcasted_iota(jnp.int32, (n_blk, LANES), 0)).astype(BF16)
    sel_bias = _dot_tn(unselected, place) * NEG
    q_aug = (qr_ref[...].astype(F32) + sel_bias[None]).astype(BF16).reshape(rows, LANES)

    qr = qr_ref[...].reshape(rows, LANES)
    wk = WINDOW + tq
    w0 = pl.multiple_of(jnp.maximum(q0 - WINDOW, 0), tq)
    back = (q0 + lax.broadcasted_iota(jnp.int32, (tq, wk), 0)
            - w0 - lax.broadcasted_iota(jnp.int32, (tq, wk), 1))
    band = jnp.where(back < 0, NEG, jnp.where(back < WINDOW, 0.0, NEG))
    s_win = _dot_nt(qr, kw_ref[pl.ds(w0, wk), :]).reshape(hpg, tq, wk) + band[None]
    p_win = jnp.exp(s_win - jnp.max(s_win, axis=-1, keepdims=True))
    pv = _dot(p_win.reshape(rows, wk).astype(BF16), vw_ref[pl.ds(w0, wk), :])
    o_win = pv[:, :dh] * (1.0 / jnp.maximum(pv[:, dh:dh + 1], 1e-30))

    kt = SLC_KT
    last = (q0 + tq - 1) // kt
    t_k = q0 + lax.broadcasted_iota(jnp.int32, (tq, kt), 0)
    lane_k = lax.broadcasted_iota(jnp.int32, (tq, kt), 1)

    def slc_prefix(n_chunks):
        below = (n_chunks - 1) * kt
        diag = pl.ds(below, kt)
        s_d = (_dot_nt(q_aug, ks_ref[diag, :]).reshape(hpg, tq, kt)
               + jnp.where(below + lane_k <= t_k, 0.0, NEG)[None])
        m = jnp.max(s_d, axis=-1, keepdims=True)
        if below:
            s_b = _dot_nt(q_aug, ks_ref[:below, :]).reshape(hpg, tq, below)
            m = jnp.maximum(m, jnp.max(s_b, axis=-1, keepdims=True))
        pv = _dot(jnp.exp(s_d - m).reshape(rows, kt).astype(BF16), vs_ref[diag, :])
        if below:
            pv = pv + _dot(jnp.exp(s_b - m).reshape(rows, below).astype(BF16), vs_ref[:below, :])
        acc_s[...] = pv.reshape(hpg, tq, LANES)

    for n_chunks in range(1, ks_ref.shape[0] // kt + 1):
        pl.when(last == n_chunks - 1)(functools.partial(slc_prefix, n_chunks))

    gates = jax.nn.sigmoid(small_ref[...])
    lane_g = lax.broadcasted_iota(jnp.int32, (tq, LANES), 1)
    o_cmp = o_cmp.reshape(hpg, tq, dh)
    o_win = o_win.reshape(hpg, tq, dh)
    acc = acc_s[...]
    o_slc = acc[..., :dh] * (1.0 / jnp.maximum(acc[..., dh:dh + 1], 1e-30))
    for hh in range(hpg):
        base = GATE_LANE0 + (grp * hpg + hh) * 3
        gate = lambda r: jnp.sum(jnp.where(lane_g == base + r, gates, 0.0), axis=-1, keepdims=True)
        o_ref[hh] = (gate(0) * o_cmp[hh] + gate(1) * o_slc[hh] + gate(2) * o_win[hh]).astype(o_ref.dtype)


def _nsa_attention(qp, qr, kcvc, ks, vs, kw, vw, small):
    b, _, s, dh = qp.shape
    tq = NSA_TQ
    n_cmp = kcvc.shape[2]
    qspec = lambda w: pl.BlockSpec((None, NSA_HPG, tq, w), lambda bi, g, i: (bi, g, i, 0))
    cspec = lambda off: pl.BlockSpec((None, None, n_cmp, dh), lambda bi, g, i: (bi, off + g, 0, 0))
    kspec = pl.BlockSpec((None, None, s, LANES), lambda bi, g, i: (bi, g, 0, 0))
    return pl.pallas_call(
        _nsa_kernel,
        grid=(b, NSA_KV, s // tq),
        in_specs=[qspec(dh), qspec(LANES), cspec(0), cspec(NSA_KV), kspec, kspec, kspec, kspec,
                  pl.BlockSpec((None, tq, LANES), lambda bi, g, i: (bi, i, 0))],
        out_specs=qspec(dh),
        out_shape=jax.ShapeDtypeStruct((b, NSA_HEADS, s, dh), BF16),
        scratch_shapes=[pltpu.VMEM((NSA_HPG, tq, LANES), F32)],
        compiler_params=pltpu.CompilerParams(
            dimension_semantics=("arbitrary", "arbitrary", "arbitrary"),
            vmem_limit_bytes=VMEM_LIMIT),
        name="nsa_attn",
    )(qp, qr, kcvc, kcvc, ks, vs, kw, vw, small)


def _out_ffn2_kernel(x_ref, og_ref, on_ref, wo_ref, nw_ref, wg_ref, wu_ref, wd_ref, fw_ref, o_ref):
    half = W_GDN_Z
    x = (x_ref[...] + _dot(og_ref[...].astype(BF16), wo_ref[:half, :])
         + _dot(on_ref[...], wo_ref[half:, :]))
    y = _swiglu_half_step(x, nw_ref[...], wg_ref, wu_ref, wd_ref)
    o_ref[...] = _rms(y, fw_ref[...])


def _out_ffn2(x2d, o_gdn, o_nsa, w_out, nw, wg, wu, wd, fw):
    t = x2d.shape[0]
    row = lambda w: pl.BlockSpec((TOKEN_TILE, w), lambda i: (i, 0))
    return pl.pallas_call(
        _out_ffn2_kernel,
        grid=(t // TOKEN_TILE,),
        in_specs=[row(D_MODEL), row(W_GDN_Z), row(W_NSA_Q), _const_spec((D_MODEL, D_MODEL)),
                  _const_spec((1, D_MODEL)), _const_spec((D_MODEL, D_FF)),
                  _const_spec((D_MODEL, D_FF)), _const_spec((D_FF, D_MODEL)),
                  _const_spec((1, D_MODEL))],
        out_specs=row(D_MODEL),
        out_shape=jax.ShapeDtypeStruct((t, D_MODEL), F32),
        compiler_params=pltpu.CompilerParams(dimension_semantics=("arbitrary",),
                                             vmem_limit_bytes=VMEM_LIMIT),
        name="out_ffn2",
    )(x2d, o_gdn, o_nsa, w_out, nw, wg, wu, wd, fw)


def _rope_tables(s_len):
    half = ROPE_DIM // 2
    inv = ROPE_THETA ** (-jnp.arange(half, dtype=F32) / half)
    ang = jnp.arange(s_len, dtype=F32)[:, None] * inv[None, :]
    cos, sin = jnp.cos(ang), jnp.sin(ang)
    pad = jnp.zeros((s_len, NSA_DH - ROPE_DIM), F32)
    zero = jnp.zeros_like(sin)
    cos_h = jnp.concatenate([cos, cos, pad + 1.0], axis=1)
    sa_h = jnp.concatenate([-sin, zero, pad], axis=1)
    sb_h = jnp.concatenate([zero, sin, pad], axis=1)
    rep = lambda a: jnp.tile(a, (1, LANES // NSA_DH))
    return rep(cos_h), rep(sa_h), rep(sb_h)


def _split_w_in(w_in):
    cuts = np.cumsum([W_GDN_QKV, W_GDN_Z, GDN_HEADS, GDN_HEADS, W_NSA_Q,
                      W_NSA_KV, W_NSA_KV, W_NSA_KV]).tolist()
    qkv, z, a, bb, nq, kvc, kvs, kvw, gate = jnp.split(w_in, cuts, axis=1)
    w_main = jnp.concatenate([qkv, z, nq, kvc, kvs, kvw], axis=1).astype(BF16)
    small = jnp.concatenate([a, bb, gate], axis=1)
    w_small = jnp.pad(small, ((0, 0), (0, LANES - small.shape[1]))).astype(BF16)
    return w_main, w_small


def _compress_weights(pe_k, w1_k, w2_k, pe_v, w1_v, w2_v):
    half = CMP_BLOCK // 2
    eye = jnp.eye(2 * NSA_KV, dtype=F32)

    def first_layer(lo):
        w1 = jnp.stack([w1_k, w1_v]).reshape(2, CMP_BLOCK, NSA_DH, CMP_HIDDEN)[:, lo:lo + half]
        w1 = jnp.repeat(w1, NSA_KV, axis=0)
        w = jnp.einsum('crdj,ce->rcdej', w1, eye)
        return w.reshape(half * 2 * NSA_KV * NSA_DH, 2 * NSA_KV * CMP_HIDDEN).astype(BF16)

    def pe_row(lo):
        pe = jnp.stack([pe_k, pe_v])[:, lo:lo + half]
        pe = jnp.repeat(pe, NSA_KV, axis=0)
        return pe.transpose(1, 0, 2).reshape(1, half * 2 * NSA_KV * NSA_DH)

    w2 = jnp.repeat(jnp.stack([w2_k, w2_v]), NSA_KV, axis=0)
    w2 = jnp.einsum('cjd,ce->cjed', w2, eye).reshape(2 * NSA_KV * CMP_HIDDEN, 2 * NSA_KV * NSA_DH)
    return pe_row(0), pe_row(half), first_layer(0), first_layer(half), w2.astype(BF16)


def kernel(x, ffn1_norm, ffn1_w_gate, ffn1_w_up, ffn1_w_down, mix_norm, w_in, gdn_conv_w, gdn_a_log,
           gdn_dt_bias, gdn_out_norm, cmp_pe_k, cmp_w1_k, cmp_w2_k, cmp_pe_v, cmp_w1_v, cmp_w2_v,
           w_out, ffn2_norm, ffn2_w_gate, ffn2_w_up, ffn2_w_down, final_norm):
    b, s, d = x.shape
    assert d == D_MODEL and s % TOKEN_TILE == 0 and ffn1_norm.shape[0] == 1
    vec = lambda a: a.reshape(1, -1)
    pad_lanes = lambda a: jnp.pad(a.reshape(1, -1), ((0, 0), (0, LANES - a.size)))
    bf = lambda a: a.astype(BF16)

    x1 = _ffn1(x.reshape(b * s, d), vec(ffn1_norm[0]), bf(ffn1_w_gate[0]), bf(ffn1_w_up[0]),
               bf(ffn1_w_down[0]))

    w_main, w_small = _split_w_in(w_in[0])
    cos_t, sin_a, sin_b = _rope_tables(s)
    qkv, z, small, qp, qr, kvc, ks, vs, kw, vw = _in_proj(
        x1.reshape(b, s, d), vec(mix_norm[0]), w_main, w_small, cos_t, sin_a, sin_b)

    o_gdn = _gdn(qkv, z, small, gdn_conv_w[0], pad_lanes(gdn_a_log[0]), pad_lanes(gdn_dt_bias[0]),
                 vec(gdn_out_norm[0]))

    pe_top, pe_bot, w_top, w_bot, w2 = _compress_weights(
        cmp_pe_k[0], cmp_w1_k[0], cmp_w2_k[0], cmp_pe_v[0], cmp_w1_v[0], cmp_w2_v[0])
    tokens16 = kvc.reshape(b, s // CMP_STRIDE, CMP_STRIDE * W_NSA_KV)
    kcvc = _compress(tokens16, pe_top, pe_bot, w_top, w_bot, w2)

    o_nsa = _nsa_attention(qp, qr, kcvc, ks, vs, kw, vw, small)
    o_nsa = o_nsa.transpose(0, 2, 1, 3).reshape(b * s, W_NSA_Q)

    out = _out_ffn2(x1, o_gdn.reshape(b * s, W_GDN_Z), o_nsa, bf(w_out[0]), vec(ffn2_norm[0]),
                    bf(ffn2_w_gate[0]), bf(ffn2_w_up[0]), bf(ffn2_w_down[0]), vec(final_norm))
    return out.reshape(b, s, d)
```

```python
import functools

import numpy as np
import jax
import jax.numpy as jnp
from jax import lax
from jax.experimental import pallas as pl
from jax.experimental.pallas import tpu as pltpu

F32 = jnp.float32
BF16 = jnp.bfloat16

D_MODEL = 1024
D_FF = 2816
GDN_HEADS = 4
GDN_DK = 128
GDN_DV = 128
GDN_CONV = 4
GDN_CHUNK = 64
NSA_HEADS = 8
NSA_KV = 2
NSA_HPG = NSA_HEADS // NSA_KV
NSA_DH = 64
CMP_BLOCK = 32
CMP_STRIDE = 16
CMP_HIDDEN = 128
SEL_BLOCK = 64
SEL_TOPK = 16
WINDOW = 512
ROPE_THETA = 500000.0
ROPE_DIM = NSA_DH // 4
EPS = 1e-6
NEG = -1e30
BIG = 1e30

W_GDN_QKV = GDN_HEADS * (2 * GDN_DK + GDN_DV)
W_GDN_Z = GDN_HEADS * GDN_DV
W_NSA_Q = NSA_HEADS * NSA_DH
W_NSA_KV = 2 * NSA_KV * NSA_DH
W_NSA_GATE = 3 * NSA_HEADS
W_MAIN = W_GDN_QKV + W_GDN_Z + W_NSA_Q + 3 * W_NSA_KV
LANES = 128
GATE_LANE0 = 2 * GDN_HEADS

TOKEN_TILE = 512
FF_CHUNK = D_FF // 2
GDN_HPS = 2
GDN_GROUP = 2 * GDN_CHUNK
NSA_TQ = 128
NSA_TS = 512
SLC_KT = 512
VMEM_LIMIT = 56 * 1024 * 1024


def _dot(a, b):
    return jnp.dot(a, b, preferred_element_type=F32)


def _dot_nt(a, b):
    return lax.dot_general(a, b, (((1,), (1,)), ((), ())), preferred_element_type=F32)


def _dot_tn(a, b):
    return lax.dot_general(a, b, (((0,), (0,)), ((), ())), preferred_element_type=F32)


def _rms(x, w):
    return x * lax.rsqrt(jnp.mean(x * x, axis=-1, keepdims=True) + EPS) * w


def _sigmoid(x):
    return 0.5 * jnp.tanh(0.5 * x) + 0.5


def _silu(x):
    return x * _sigmoid(x)


def _swiglu_half_step(x, nw, wg_ref, wu_ref, wd_ref):
    hb = _rms(x, nw).astype(BF16)
    acc = jnp.zeros_like(x)
    for c in range(D_FF // FF_CHUNK):
        sl = slice(c * FF_CHUNK, (c + 1) * FF_CHUNK)
        g = _dot(hb, wg_ref[:, sl])
        u = _dot(hb, wu_ref[:, sl])
        acc = acc + _dot((_silu(g) * u).astype(BF16), wd_ref[sl, :])
    return x + 0.5 * acc


def _ffn_kernel(x_ref, nw_ref, wg_ref, wu_ref, wd_ref, o_ref):
    o_ref[...] = _swiglu_half_step(x_ref[...], nw_ref[...], wg_ref, wu_ref, wd_ref)


def _const_spec(shape):
    nd = len(shape)
    return pl.BlockSpec(shape, lambda *_: (0,) * nd, pipeline_mode=pl.Buffered(1))


def _ffn1(x2d, nw, wg, wu, wd):
    t = x2d.shape[0]
    row = pl.BlockSpec((TOKEN_TILE, D_MODEL), lambda i: (i, 0))
    return pl.pallas_call(
        _ffn_kernel,
        grid=(t // TOKEN_TILE,),
        in_specs=[row, _const_spec((1, D_MODEL)), _const_spec((D_MODEL, D_FF)),
                  _const_spec((D_MODEL, D_FF)), _const_spec((D_FF, D_MODEL))],
        out_specs=row,
        out_shape=jax.ShapeDtypeStruct((t, D_MODEL), F32),
        compiler_params=pltpu.CompilerParams(dimension_semantics=("arbitrary",),
                                             vmem_limit_bytes=VMEM_LIMIT),
        name="ffn1",
    )(x2d, nw, wg, wu, wd)


def _rope(x, cos_t, sin_a, sin_b):
    return (x * cos_t + pltpu.roll(x, LANES - ROPE_DIM // 2, axis=1) * sin_a
            + pltpu.roll(x, ROPE_DIM // 2, axis=1) * sin_b)


def _inproj_kernel(x_ref, nw_ref, wm_ref, ws_ref, cos_ref, sa_ref, sb_ref, alog_ref, dt_ref,
                   qkv_ref, z_ref, small_ref, qp_ref, qr_ref, kvc_ref,
                   ks_ref, vs_ref, kw_ref, vw_ref):
    hb = _rms(x_ref[...], nw_ref[...]).astype(BF16)
    y = _dot(hb, wm_ref[...])
    tm = x_ref.shape[0]
    lane = lax.broadcasted_iota(jnp.int32, (tm, LANES), 1)
    sm = _dot(hb, ws_ref[...])
    xg = sm + dt_ref[...]
    softplus = jnp.maximum(xg, 0.0) + jnp.log1p(jnp.exp(-jnp.abs(xg)))
    small_ref[...] = jnp.where(lane < GDN_HEADS, -jnp.exp(alog_ref[...]) * softplus, _sigmoid(sm))
    qkv_ref[...] = y[:, :W_GDN_QKV]
    o = W_GDN_QKV
    z_ref[...] = y[:, o:o + W_GDN_Z]
    o += W_GDN_Z
    cos_t, sin_a, sin_b = cos_ref[...], sa_ref[...], sb_ref[...]
    scale = NSA_DH ** -0.5
    pos = pl.program_id(1) * tm + lax.broadcasted_iota(jnp.int32, (tm, LANES), 0)
    low_half = lane < NSA_DH
    block_onehot = jnp.where(lane - NSA_DH == pos // SEL_BLOCK, 1.0, 0.0)
    ones_col = jnp.where(lane == NSA_DH, 1.0, 0.0)

    def halves(slab):
        return slab, pltpu.roll(slab, NSA_DH, axis=1)

    for s in range(W_NSA_Q // LANES):
        slab = y[:, o + s * LANES:o + (s + 1) * LANES]
        plain = halves(slab * scale)
        rot = halves(_rope(slab, cos_t, sin_a, sin_b) * scale)
        for j in range(2):
            qp_ref[2 * s + j] = jnp.where(low_half, plain[j], 0.0).astype(BF16)
            qr_ref[2 * s + j] = jnp.where(low_half, rot[j], 0.0).astype(BF16)
    o += W_NSA_Q
    kvc_ref[...] = y[:, o:o + W_NSA_KV]
    o += W_NSA_KV
    for k_ref, v_ref, k_pad in ((ks_ref, vs_ref, block_onehot), (kw_ref, vw_ref, 0.0)):
        kr = halves(_rope(y[:, o:o + LANES], cos_t, sin_a, sin_b))
        vv = halves(y[:, o + LANES:o + 2 * LANES])
        for j in range(NSA_KV):
            k_ref[j] = jnp.where(low_half, kr[j], k_pad).astype(BF16)
            v_ref[j] = jnp.where(low_half, vv[j], ones_col).astype(BF16)
        o += W_NSA_KV


def _in_proj(x3d, nw, w_main, w_small, cos_t, sin_a, sin_b, alog_p, dt_p):
    b, s, _ = x3d.shape
    tm = TOKEN_TILE
    row = lambda w: pl.BlockSpec((None, tm, w), lambda bi, i: (bi, i, 0))
    heads = lambda n, w: pl.BlockSpec((None, n, tm, w), lambda bi, i: (bi, 0, i, 0))
    tab = pl.BlockSpec((tm, LANES), lambda bi, i: (i, 0))
    out_shape = (
        jax.ShapeDtypeStruct((b, s, W_GDN_QKV), F32),
        jax.ShapeDtypeStruct((b, s, W_GDN_Z), F32),
        jax.ShapeDtypeStruct((b, s, LANES), F32),
        jax.ShapeDtypeStruct((b, NSA_HEADS, s, LANES), BF16),
        jax.ShapeDtypeStruct((b, NSA_HEADS, s, LANES), BF16),
        jax.ShapeDtypeStruct((b, s, W_NSA_KV), F32),
        jax.ShapeDtypeStruct((b, NSA_KV, s, LANES), BF16),
        jax.ShapeDtypeStruct((b, NSA_KV, s, LANES), BF16),
        jax.ShapeDtypeStruct((b, NSA_KV, s, LANES), BF16),
        jax.ShapeDtypeStruct((b, NSA_KV, s, LANES), BF16),
    )
    kv_aug = heads(NSA_KV, LANES)
    return pl.pallas_call(
        _inproj_kernel,
        grid=(b, s // tm),
        in_specs=[row(D_MODEL), _const_spec((1, D_MODEL)), _const_spec((D_MODEL, W_MAIN)),
                  _const_spec((D_MODEL, LANES)), tab, tab, tab,
                  _const_spec((1, LANES)), _const_spec((1, LANES))],
        out_specs=(row(W_GDN_QKV), row(W_GDN_Z), row(LANES), heads(NSA_HEADS, LANES),
                   heads(NSA_HEADS, LANES), row(W_NSA_KV), kv_aug, kv_aug, kv_aug, kv_aug),
        out_shape=out_shape,
        compiler_params=pltpu.CompilerParams(dimension_semantics=("arbitrary", "arbitrary"),
                                             vmem_limit_bytes=VMEM_LIMIT),
        name="in_proj",
    )(x3d, nw, w_main, w_small, cos_t, sin_a, sin_b, alog_p, dt_p)


def _bmm(a, b):
    return jnp.einsum('nij,njk->nik', a, b, preferred_element_type=F32)


def _bmm_nt(a, b):
    return jnp.einsum('nid,njd->nij', a, b, preferred_element_type=F32)


def _unit_lower_inverse_minus_eye(low, ii, jj):
    mm = lambda a, b: _bmm(a.astype(BF16), b.astype(BF16))
    same16 = (ii // 16) == (jj // 16)
    same32 = (ii // 32) == (jj // 32)
    ld = jnp.where(same16, low, 0.0)
    e = -ld
    p = mm(ld, ld)
    for level in range(3):
        e = e + p + mm(e, p)
        if level < 2:
            p = mm(p, p)
    for within in (same32 & jnp.logical_not(same16), jnp.logical_not(same32)):
        c = jnp.where(within, low, 0.0)
        y = c + mm(e, c)
        e = e - y - mm(y, e)
    return e


def _gdn_kernel(xq_ref, xk_ref, xv_ref, wq_ref, wk_ref, wv_ref, small_ref, z_ref,
                onorm_ref, o_ref,
                mp_s, n_s, qt_s, o0_s, egl_s, sall_s):
    hp = pl.program_id(1)
    s_len = xq_ref.shape[0]
    c_len = GDN_CHUNK
    grp = GDN_GROUP
    n_chunks = s_len // c_len
    n_grp = s_len // grp
    hd = GDN_DK
    row = lax.broadcasted_iota(jnp.int32, (s_len, LANES), 0)
    lane = lax.broadcasted_iota(jnp.int32, (s_len, LANES), 1)
    pos = row % c_len
    ii = lax.broadcasted_iota(jnp.int32, (grp, grp), 0)
    jj = lax.broadcasted_iota(jnp.int32, (grp, grp), 1)
    same_chunk = (ii // c_len) == (jj // c_len)
    tril = same_chunk & (ii >= jj)
    strict = same_chunk & (ii > jj)
    first = lax.broadcasted_iota(jnp.int32, (grp, hd), 0) < c_len
    small = small_ref[...]
    g3 = lambda a: a.reshape(n_grp, grp, hd)

    top = GDN_CONV * 2
    row_top = lax.broadcasted_iota(jnp.int32, (top, hd), 0)

    def conv_silu(x_ref, w_ref, cols):
        w = w_ref[:, cols]
        x_top = x_ref[:top, cols]
        acc_top = x_top * w[GDN_CONV - 1:GDN_CONV, :]
        acc = x_ref[top:, cols] * w[GDN_CONV - 1:GDN_CONV, :]
        for sh in range(1, GDN_CONV):
            tap = w[GDN_CONV - 1 - sh:GDN_CONV - sh, :]
            acc_top = acc_top + jnp.where(row_top >= sh, pltpu.roll(x_top, sh, axis=0), 0.0) * tap
            acc = acc + x_ref[pl.ds(top - sh, s_len - top), cols] * tap
        return _silu(jnp.concatenate([acc_top, acc], axis=0))

    def l2n(x):
        return x * lax.rsqrt(jnp.sum(x * x, axis=-1, keepdims=True) + EPS)

    for j in range(GDN_HPS):
        h = hp * GDN_HPS + j
        cols = slice(j * hd, (j + 1) * hd)
        q = l2n(conv_silu(xq_ref, wq_ref, cols)) * (GDN_DK ** -0.5)
        k = l2n(conv_silu(xk_ref, wk_ref, cols))
        v = conv_silu(xv_ref, wv_ref, cols)

        pick = lambda ln: jnp.broadcast_to(
            jnp.sum(jnp.where(lane == ln, small, 0.0), axis=-1, keepdims=True), (s_len, LANES))
        beta = pick(GDN_HEADS + h)
        g = pick(h)
        sh = 1
        while sh < c_len:
            g = g + jnp.where(pos >= sh, pltpu.roll(g, sh, axis=0), 0.0)
            sh *= 2
        gc = g
        gc3 = gc.reshape(n_chunks, c_len, LANES)
        gl3 = gc3[:, c_len - 1:c_len, :]
        egl_s[j] = jnp.exp(gl3)
        gl = jnp.broadcast_to(gl3, gc3.shape).reshape(s_len, LANES)
        eg = jnp.exp(gc)
        kb = k * beta
        vb = g3(v * beta)
        kbg = g3(kb * eg)

        k3 = g3(k).astype(BF16)
        kk = _bmm_nt(g3(kb).astype(BF16), k3)
        qk = _bmm_nt(g3(q).astype(BF16), k3)
        gcg = g3(gc)
        diff = gcg - jnp.swapaxes(gcg, 1, 2)
        decay = jnp.where(tril, jnp.exp(jnp.where(tril, diff, 0.0)), 0.0)
        e_b = _unit_lower_inverse_minus_eye(jnp.where(strict, kk * decay, 0.0), ii, jj).astype(BF16)
        u = vb + _bmm(e_b, vb.astype(BF16))
        w = kbg + _bmm(e_b, kbg.astype(BF16))
        qkm = jnp.where(tril, qk * decay, 0.0).astype(BF16)
        r = _bmm(qkm, jnp.concatenate([w, u], axis=-1).astype(BF16))
        qt_s[j] = (g3(q * eg) - r[..., :hd]).astype(BF16)
        o0_s[j] = r[..., hd:]
        kd_t = jnp.swapaxes(g3(k * jnp.exp(gl - gc)), 1, 2).astype(BF16)
        rhs = jnp.concatenate([jnp.where(first, w, 0.0), jnp.where(first, 0.0, w),
                               jnp.where(first, u, 0.0), jnp.where(first, 0.0, u)], axis=-1)
        mn = _bmm(kd_t, rhs.astype(BF16))
        mp_s[j] = (-mn[..., :2 * hd]).astype(BF16)
        n_s[j] = mn[..., 2 * hd:]

    def scan(n, states):
        out = []
        for j in range(GDN_HPS):
            st = states[j]
            for half in range(grp // c_len):
                c = n * (grp // c_len) + half
                lanes = slice(half * hd, (half + 1) * hd)
                sb = st.astype(BF16)
                sall_s[j, c] = sb
                st = egl_s[j, c] * st + _dot(mp_s[j, n, :, lanes], sb) + n_s[j, n, :, lanes]
            out.append(st)
        return tuple(out)

    lax.fori_loop(0, n_grp, scan, tuple(jnp.zeros((GDN_DK, GDN_DV), F32) for _ in range(GDN_HPS)))

    for j in range(GDN_HPS):
        cols = slice(j * hd, (j + 1) * hd)
        qt = qt_s[j].reshape(n_chunks, c_len, hd)
        o = _bmm(qt, sall_s[j]) + o0_s[j].reshape(n_chunks, c_len, hd)
        o_ref[:, cols] = _rms(o.reshape(s_len, hd), onorm_ref[...]) * _silu(z_ref[:, cols])


def _gdn(qkv, z, small, conv_w, onorm):
    b, s, _ = qkv.shape
    wid = GDN_HPS * GDN_DK
    n_hp = GDN_HEADS // GDN_HPS
    n_grp, n_chunks = s // GDN_GROUP, s // GDN_CHUNK
    col = lambda off: pl.BlockSpec((None, s, wid), lambda bi, h: (bi, 0, off + h))
    wcol = lambda off: pl.BlockSpec((GDN_CONV, wid), lambda bi, h: (0, off + h))
    vec = pl.BlockSpec((1, LANES), lambda bi, h: (0, 0))
    return pl.pallas_call(
        _gdn_kernel,
        grid=(b, n_hp),
        in_specs=[col(0), col(n_hp), col(2 * n_hp),
                  wcol(0), wcol(n_hp), wcol(2 * n_hp),
                  pl.BlockSpec((None, s, LANES), lambda bi, h: (bi, 0, 0)),
                  col(0), vec],
        out_specs=col(0),
        out_shape=jax.ShapeDtypeStruct((b, s, GDN_HEADS * GDN_DV), F32),
        scratch_shapes=[pltpu.VMEM((GDN_HPS, n_grp, GDN_DK, 2 * GDN_DV), BF16),
                        pltpu.VMEM((GDN_HPS, n_grp, GDN_DK, 2 * GDN_DV), F32),
                        pltpu.VMEM((GDN_HPS, n_grp, GDN_GROUP, GDN_DK), BF16),
                        pltpu.VMEM((GDN_HPS, n_grp, GDN_GROUP, GDN_DV), F32),
                        pltpu.VMEM((GDN_HPS, n_chunks, 1, LANES), F32),
                        pltpu.VMEM((GDN_HPS, n_chunks, GDN_DK, GDN_DV), BF16)],
        compiler_params=pltpu.CompilerParams(dimension_semantics=("arbitrary", "arbitrary"),
                                             vmem_limit_bytes=VMEM_LIMIT),
        name="gdn",
    )(qkv, qkv, qkv, conv_w, conv_w, conv_w, small, z, onorm)


def _compress_kernel(t_ref, pet_ref, peb_ref, wt_ref, wb_ref, w2_ref, o_ref):
    t = t_ref[...]
    a = _dot((t + pet_ref[...]).astype(BF16), wt_ref[...])
    bm = _dot((t + peb_ref[...]).astype(BF16), wb_ref[...])
    n_rows = t.shape[0]
    hid = a + pltpu.roll(bm, n_rows - 1, axis=0)
    out = _dot(_silu(hid).astype(BF16), w2_ref[...])
    for j in range(2 * NSA_KV):
        o_ref[j] = out[:, j * LANES:(j + 1) * LANES].astype(o_ref.dtype)


def _compress(tm, pe_top, pe_bot, w_top, w_bot, w2):
    b, n_rows, width = tm.shape
    nh = 2 * NSA_KV * CMP_HIDDEN
    return pl.pallas_call(
        _compress_kernel,
        grid=(b,),
        in_specs=[pl.BlockSpec((None, n_rows, width), lambda bi: (bi, 0, 0)),
                  _const_spec((1, width)), _const_spec((1, width)),
                  _const_spec((width, nh)), _const_spec((width, nh)),
                  _const_spec((nh, 2 * NSA_KV * LANES))],
        out_specs=pl.BlockSpec((None, 2 * NSA_KV, n_rows, LANES), lambda bi: (bi, 0, 0, 0)),
        out_shape=jax.ShapeDtypeStruct((b, 2 * NSA_KV, n_rows, LANES), BF16),
        compiler_params=pltpu.CompilerParams(dimension_semantics=("arbitrary",),
                                             vmem_limit_bytes=VMEM_LIMIT),
        name="compress",
    )(tm, pe_top, pe_bot, w_top, w_bot, w2)


def _gate_columns(gates, lane, base):
    return [jnp.sum(jnp.where(lane == base + r, gates, 0.0), axis=-1, keepdims=True) for r in range(3)]


def _pack_heads(heads, lane):
    pair = lambda a, b: jnp.where(lane < NSA_DH, a, pltpu.roll(b, NSA_DH, axis=1))
    return jnp.concatenate([pair(heads[0], heads[1]), pair(heads[2], heads[3])], axis=1)


def _nsa_select_kernel(qp_ref, kc_ref, vc_ref, small_ref, ocmp_ref, bias_ref):
    grp = pl.program_id(1)
    ts, hpg = qp_ref.shape[1], NSA_HPG
    q0 = pl.program_id(2) * ts
    rows = hpg * ts
    n_cmp = kc_ref.shape[0]
    n_blk = n_cmp * CMP_STRIDE // SEL_BLOCK

    t_c = q0 + lax.broadcasted_iota(jnp.int32, (ts, n_cmp), 0)
    n_c = lax.broadcasted_iota(jnp.int32, (ts, n_cmp), 1)
    vis = n_c * CMP_STRIDE + (CMP_BLOCK - 1) <= t_c
    s_cmp = _dot_nt(qp_ref[...].reshape(rows, LANES), kc_ref[...]).reshape(hpg, ts, n_cmp)
    s_cmp = jnp.where(vis[None], s_cmp, NEG)
    p_cmp = jnp.exp(s_cmp - jnp.max(s_cmp, axis=-1, keepdims=True)) * vis[None].astype(F32)
    p_cmp = p_cmp * (1.0 / jnp.maximum(jnp.sum(p_cmp, axis=-1, keepdims=True), 1e-30))
    o_cmp = _dot(p_cmp.reshape(rows, n_cmp).astype(BF16), vc_ref[...]).reshape(hpg, ts, LANES)
    lane = lax.broadcasted_iota(jnp.int32, (ts, LANES), 1)
    gates = small_ref[...]
    ocmp_ref[...] = _pack_heads(
        [_gate_columns(gates, lane, GATE_LANE0 + (grp * hpg + hh) * 3)[0] * o_cmp[hh] for hh in range(hpg)],
        lane)

    p_sum = p_cmp[0]
    for hh in range(1, hpg):
        p_sum = p_sum + p_cmp[hh]
    oj = lax.broadcasted_iota(jnp.int32, (n_blk, n_cmp), 0) * SEL_BLOCK
    on = lax.broadcasted_iota(jnp.int32, (n_blk, n_cmp), 1) * CMP_STRIDE
    overlap_t = jnp.where(on < oj + SEL_BLOCK, jnp.where(on + CMP_BLOCK > oj, 1.0, 0.0), 0.0)
    imp_t = lax.dot_general(overlap_t, p_sum, (((1,), (1,)), ((), ())),
                            precision=lax.Precision.HIGHEST, preferred_element_type=F32)
    blk = lax.broadcasted_iota(jnp.int32, (n_blk, ts), 0)
    ahead = (q0 + lax.broadcasted_iota(jnp.int32, (n_blk, ts), 1)) // SEL_BLOCK - blk
    score = jnp.where(ahead < 0, NEG, jnp.where(blk == 0, BIG, jnp.where(ahead <= 1, BIG, imp_t)))
    rank = jnp.zeros((n_blk, ts), F32)
    for i in range(n_blk):
        ci = jnp.broadcast_to(score[i:i + 1, :], (n_blk, ts))
        tie = jnp.where(blk > i, 1.0, 0.0)
        rank = rank + jnp.where(ci > score, 1.0, jnp.where(ci == score, tie, 0.0))
    unselected = jnp.where(rank >= SEL_TOPK, 1.0, 0.0).astype(BF16)
    place = (lax.broadcasted_iota(jnp.int32, (n_blk, LANES), 1) - NSA_DH
             == lax.broadcasted_iota(jnp.int32, (n_blk, LANES), 0)).astype(BF16)
    bias_ref[...] = (_dot_tn(unselected, place) * NEG).astype(bias_ref.dtype)


def _nsa_select(qp, kcvc, small):
    b, _, s, _ = qp.shape
    ts = NSA_TS
    n_cmp = kcvc.shape[2]
    cspec = lambda off: pl.BlockSpec((None, None, n_cmp, LANES), lambda bi, g, i: (bi, off + g, 0, 0))
    return pl.pallas_call(
        _nsa_select_kernel,
        grid=(b, NSA_KV, s // ts),
        in_specs=[pl.BlockSpec((None, NSA_HPG, ts, LANES), lambda bi, g, i: (bi, g, i, 0)),
                  cspec(0), cspec(NSA_KV),
                  pl.BlockSpec((None, ts, LANES), lambda bi, g, i: (bi, i, 0))],
        out_specs=(pl.BlockSpec((None, ts, NSA_HPG * NSA_DH), lambda bi, g, i: (bi, i, g)),
                   pl.BlockSpec((None, None, ts, LANES), lambda bi, g, i: (bi, g, i, 0))),
        out_shape=(jax.ShapeDtypeStruct((b, s, W_NSA_Q), F32),
                   jax.ShapeDtypeStruct((b, NSA_KV, s, LANES), BF16)),
        compiler_params=pltpu.CompilerParams(
            dimension_semantics=("arbitrary", "arbitrary", "arbitrary"),
            vmem_limit_bytes=VMEM_LIMIT),
        name="nsa_select",
    )(qp, kcvc, kcvc, small)


def _nsa_kernel(qr_ref, bias_ref, ks_ref, vs_ref, kw_ref, vw_ref, small_ref, ocmp_ref, o_ref, acc_s):
    grp = pl.program_id(1)
    q0 = pl.program_id(2) * NSA_TQ
    tq, hpg, dh = NSA_TQ, NSA_HPG, NSA_DH
    rows = hpg * tq
    qr3 = qr_ref[...]
    qr = qr3.reshape(rows, LANES)
    q_aug = (qr3.astype(F32) + bias_ref[...].astype(F32)[None]).astype(BF16).reshape(rows, LANES)

    def normalised(pv):
        return pv * (1.0 / jnp.maximum(pv[:, dh:dh + 1], 1e-30))

    wk = WINDOW + tq
    w0 = pl.multiple_of(jnp.maximum(q0 - WINDOW, 0), tq)
    back = (q0 + lax.broadcasted_iota(jnp.int32, (tq, wk), 0)
            - w0 - lax.broadcasted_iota(jnp.int32, (tq, wk), 1))
    band = jnp.where(back < 0, NEG, jnp.where(back < WINDOW, 0.0, NEG))
    s_win = _dot_nt(qr, kw_ref[pl.ds(w0, wk), :]).reshape(hpg, tq, wk) + band[None]
    p_win = jnp.exp(s_win - jnp.max(s_win, axis=-1, keepdims=True))
    o_win = normalised(_dot(p_win.reshape(rows, wk).astype(BF16), vw_ref[pl.ds(w0, wk), :]))

    kt = SLC_KT
    last = (q0 + tq - 1) // kt
    t_k = q0 + lax.broadcasted_iota(jnp.int32, (tq, kt), 0)
    lane_k = lax.broadcasted_iota(jnp.int32, (tq, kt), 1)

    def slc_prefix(n_chunks):
        below = (n_chunks - 1) * kt
        diag = pl.ds(below, kt)
        s_d = (_dot_nt(q_aug, ks_ref[diag, :]).reshape(hpg, tq, kt)
               + jnp.where(below + lane_k <= t_k, 0.0, NEG)[None])
        m = jnp.max(s_d, axis=-1, keepdims=True)
        if below:
            s_b = _dot_nt(q_aug, ks_ref[:below, :]).reshape(hpg, tq, below)
            m = jnp.maximum(m, jnp.max(s_b, axis=-1, keepdims=True))
        pv = _dot(jnp.exp(s_d - m).reshape(rows, kt).astype(BF16), vs_ref[diag, :])
        if below:
            pv = pv + _dot(jnp.exp(s_b - m).reshape(rows, below).astype(BF16), vs_ref[:below, :])
        acc_s[...] = pv

    for n_chunks in range(1, ks_ref.shape[0] // kt + 1):
        pl.when(last == n_chunks - 1)(functools.partial(slc_prefix, n_chunks))

    o_slc = normalised(acc_s[...])

    lane = lax.broadcasted_iota(jnp.int32, (tq, LANES), 1)
    gates = small_ref[...]
    mixed = []
    for hh in range(hpg):
        _, g_slc, g_win = _gate_columns(gates, lane, GATE_LANE0 + (grp * hpg + hh) * 3)
        head = slice(hh * tq, (hh + 1) * tq)
        mixed.append(g_slc * o_slc[head] + g_win * o_win[head])
    o_ref[...] = (ocmp_ref[...] + _pack_heads(mixed, lane)).astype(o_ref.dtype)


def _nsa_attention(qr, sel_bias, ks, vs, kw, vw, small, o_cmp):
    b, _, s, _ = qr.shape
    tq = NSA_TQ
    kspec = pl.BlockSpec((None, None, s, LANES), lambda bi, g, i: (bi, g, 0, 0))
    tok = pl.BlockSpec((None, tq, NSA_HPG * NSA_DH), lambda bi, g, i: (bi, i, g))
    return pl.pallas_call(
        _nsa_kernel,
        grid=(b, NSA_KV, s // tq),
        in_specs=[pl.BlockSpec((None, NSA_HPG, tq, LANES), lambda bi, g, i: (bi, g, i, 0)),
                  pl.BlockSpec((None, None, tq, LANES), lambda bi, g, i: (bi, g, i, 0)),
                  kspec, kspec, kspec, kspec,
                  pl.BlockSpec((None, tq, LANES), lambda bi, g, i: (bi, i, 0)), tok],
        out_specs=tok,
        out_shape=jax.ShapeDtypeStruct((b, s, W_NSA_Q), BF16),
        scratch_shapes=[pltpu.VMEM((NSA_HPG * tq, LANES), F32)],
        compiler_params=pltpu.CompilerParams(
            dimension_semantics=("arbitrary", "arbitrary", "arbitrary"),
            vmem_limit_bytes=VMEM_LIMIT),
        name="nsa_attn",
    )(qr, sel_bias, ks, vs, kw, vw, small, o_cmp)


def _out_ffn2_kernel(x_ref, og_ref, on_ref, wo_ref, nw_ref, wg_ref, wu_ref, wd_ref, fw_ref, o_ref):
    half = W_GDN_Z
    x = (x_ref[...] + _dot(og_ref[...].astype(BF16), wo_ref[:half, :])
         + _dot(on_ref[...], wo_ref[half:, :]))
    y = _swiglu_half_step(x, nw_ref[...], wg_ref, wu_ref, wd_ref)
    o_ref[...] = _rms(y, fw_ref[...])


def _out_ffn2(x2d, o_gdn, o_nsa, w_out, nw, wg, wu, wd, fw):
    t = x2d.shape[0]
    row = lambda w: pl.BlockSpec((TOKEN_TILE, w), lambda i: (i, 0))
    return pl.pallas_call(
        _out_ffn2_kernel,
        grid=(t // TOKEN_TILE,),
        in_specs=[row(D_MODEL), row(W_GDN_Z), row(W_NSA_Q), _const_spec((D_MODEL, D_MODEL)),
                  _const_spec((1, D_MODEL)), _const_spec((D_MODEL, D_FF)),
                  _const_spec((D_MODEL, D_FF)), _const_spec((D_FF, D_MODEL)),
                  _const_spec((1, D_MODEL))],
        out_specs=row(D_MODEL),
        out_shape=jax.ShapeDtypeStruct((t, D_MODEL), F32),
        compiler_params=pltpu.CompilerParams(dimension_semantics=("arbitrary",),
                                             vmem_limit_bytes=VMEM_LIMIT),
        name="out_ffn2",
    )(x2d, o_gdn, o_nsa, w_out, nw, wg, wu, wd, fw)


def _rope_tables(s_len):
    half = ROPE_DIM // 2
    inv = ROPE_THETA ** (-jnp.arange(half, dtype=F32) / half)
    ang = jnp.arange(s_len, dtype=F32)[:, None] * inv[None, :]
    cos, sin = jnp.cos(ang), jnp.sin(ang)
    pad = jnp.zeros((s_len, NSA_DH - ROPE_DIM), F32)
    zero = jnp.zeros_like(sin)
    cos_h = jnp.concatenate([cos, cos, pad + 1.0], axis=1)
    sa_h = jnp.concatenate([-sin, zero, pad], axis=1)
    sb_h = jnp.concatenate([zero, sin, pad], axis=1)
    rep = lambda a: jnp.tile(a, (1, LANES // NSA_DH))
    return rep(cos_h), rep(sa_h), rep(sb_h)


def _split_w_in(w_in):
    cuts = np.cumsum([W_GDN_QKV, W_GDN_Z, GDN_HEADS, GDN_HEADS, W_NSA_Q,
                      W_NSA_KV, W_NSA_KV, W_NSA_KV]).tolist()
    qkv, z, a, bb, nq, kvc, kvs, kvw, gate = jnp.split(w_in, cuts, axis=1)
    w_main = jnp.concatenate([qkv, z, nq, kvc, kvs, kvw], axis=1).astype(BF16)
    small = jnp.concatenate([a, bb, gate], axis=1)
    w_small = jnp.pad(small, ((0, 0), (0, LANES - small.shape[1]))).astype(BF16)
    return w_main, w_small


def _compress_weights(pe_k, w1_k, w2_k, pe_v, w1_v, w2_v):
    half = CMP_BLOCK // 2
    eye = jnp.eye(2 * NSA_KV, dtype=F32)

    def first_layer(lo):
        w1 = jnp.stack([w1_k, w1_v]).reshape(2, CMP_BLOCK, NSA_DH, CMP_HIDDEN)[:, lo:lo + half]
        w1 = jnp.repeat(w1, NSA_KV, axis=0)
        w = jnp.einsum('crdj,ce->rcdej', w1, eye)
        return w.reshape(half * 2 * NSA_KV * NSA_DH, 2 * NSA_KV * CMP_HIDDEN).astype(BF16)

    def pe_row(lo):
        pe = jnp.stack([pe_k, pe_v])[:, lo:lo + half]
        pe = jnp.repeat(pe, NSA_KV, axis=0)
        return pe.transpose(1, 0, 2).reshape(1, half * 2 * NSA_KV * NSA_DH)

    w2 = jnp.repeat(jnp.stack([w2_k, w2_v]), NSA_KV, axis=0)
    w2 = jnp.pad(w2, ((0, 0), (0, 0), (0, LANES - NSA_DH)))
    w2 = jnp.einsum('cjd,ce->cjed', w2, eye).reshape(2 * NSA_KV * CMP_HIDDEN, 2 * NSA_KV * LANES)
    return pe_row(0), pe_row(half), first_layer(0), first_layer(half), w2.astype(BF16)


def kernel(x, ffn1_norm, ffn1_w_gate, ffn1_w_up, ffn1_w_down, mix_norm, w_in, gdn_conv_w, gdn_a_log,
           gdn_dt_bias, gdn_out_norm, cmp_pe_k, cmp_w1_k, cmp_w2_k, cmp_pe_v, cmp_w1_v, cmp_w2_v,
           w_out, ffn2_norm, ffn2_w_gate, ffn2_w_up, ffn2_w_down, final_norm):
    b, s, d = x.shape
    assert d == D_MODEL and s % TOKEN_TILE == 0 and ffn1_norm.shape[0] == 1
    vec = lambda a: a.reshape(1, -1)
    pad_lanes = lambda a: jnp.pad(a.reshape(1, -1), ((0, 0), (0, LANES - a.size)))
    bf = lambda a: a.astype(BF16)

    x1 = _ffn1(x.reshape(b * s, d), vec(ffn1_norm[0]), bf(ffn1_w_gate[0]), bf(ffn1_w_up[0]),
               bf(ffn1_w_down[0]))

    w_main, w_small = _split_w_in(w_in[0])
    cos_t, sin_a, sin_b = _rope_tables(s)
    qkv, z, small, qp, qr, kvc, ks, vs, kw, vw = _in_proj(
        x1.reshape(b, s, d), vec(mix_norm[0]), w_main, w_small, cos_t, sin_a, sin_b,
        pad_lanes(gdn_a_log[0]), pad_lanes(gdn_dt_bias[0]))

    o_gdn = _gdn(qkv, z, small, gdn_conv_w[0], vec(gdn_out_norm[0]))

    pe_top, pe_bot, w_top, w_bot, w2 = _compress_weights(
        cmp_pe_k[0], cmp_w1_k[0], cmp_w2_k[0], cmp_pe_v[0], cmp_w1_v[0], cmp_w2_v[0])
    tokens16 = kvc.reshape(b, s // CMP_STRIDE, CMP_STRIDE * W_NSA_KV)
    kcvc = _compress(tokens16, pe_top, pe_bot, w_top, w_bot, w2)

    o_cmp, sel_bias = _nsa_select(qp, kcvc, small)
    o_nsa = _nsa_attention(qr, sel_bias, ks, vs, kw, vw, small, o_cmp).reshape(b * s, W_NSA_Q)

    out = _out_ffn2(x1, o_gdn.reshape(b * s, W_GDN_Z), o_nsa, bf(w_out[0]), vec(ffn2_norm[0]),
                    bf(ffn2_w_gate[0]), bf(ffn2_w_up[0]), bf(ffn2_w_down[0]), vec(final_norm))
    return out.reshape(b, s, d)
```

```python
import functools

import numpy as np
import jax
import jax.numpy as jnp
from jax import lax
from jax.experimental import pallas as pl
from jax.experimental.pallas import tpu as pltpu

F32 = jnp.float32
BF16 = jnp.bfloat16

D_MODEL = 1024
D_FF = 2816
GDN_HEADS = 4
GDN_DK = 128
GDN_DV = 128
GDN_CONV = 4
GDN_CHUNK = 64
NSA_HEADS = 8
NSA_KV = 2
NSA_HPG = NSA_HEADS // NSA_KV
NSA_DH = 64
CMP_BLOCK = 32
CMP_STRIDE = 16
CMP_HIDDEN = 128
SEL_BLOCK = 64
SEL_TOPK = 16
WINDOW = 512
ROPE_THETA = 500000.0
ROPE_DIM = NSA_DH // 4
EPS = 1e-6
NEG = -1e30
BIG = 1e30

W_GDN_QKV = GDN_HEADS * (2 * GDN_DK + GDN_DV)
W_GDN_Z = GDN_HEADS * GDN_DV
W_NSA_Q = NSA_HEADS * NSA_DH
W_NSA_KV = 2 * NSA_KV * NSA_DH
W_NSA_GATE = 3 * NSA_HEADS
W_MAIN = W_GDN_QKV + W_GDN_Z + W_NSA_Q + 3 * W_NSA_KV
LANES = 128
GATE_LANE0 = 2 * GDN_HEADS

TOKEN_TILE = 512
FF_CHUNK = D_FF // 2
GDN_HPS = 2
GDN_GROUP = 2 * GDN_CHUNK
NSA_TQ = 128
NSA_TS = 512
SLC_KT = 256
NSA_VT_ROWS = 80
NSA_GATE_ROWS = 32
VMEM_LIMIT = 56 * 1024 * 1024


def _dot(a, b):
    return jnp.dot(a, b, preferred_element_type=F32)


def _dot_nt(a, b):
    return lax.dot_general(a, b, (((1,), (1,)), ((), ())), preferred_element_type=F32)


def _dot_tn(a, b):
    return lax.dot_general(a, b, (((0,), (0,)), ((), ())), preferred_element_type=F32)


def _rms(x, w):
    return x * lax.rsqrt(jnp.mean(x * x, axis=-1, keepdims=True) + EPS) * w


def _sigmoid(x):
    return 0.5 * jnp.tanh(0.5 * x) + 0.5


def _silu(x):
    return x * _sigmoid(x)


def _swiglu_half_step(x, nw, wg_ref, wu_ref, wd_ref):
    hb = _rms(x, nw).astype(BF16)
    acc = jnp.zeros_like(x)
    for c in range(D_FF // FF_CHUNK):
        sl = slice(c * FF_CHUNK, (c + 1) * FF_CHUNK)
        g = _dot(hb, wg_ref[:, sl])
        u = _dot(hb, wu_ref[:, sl])
        acc = acc + _dot((_silu(g) * u).astype(BF16), wd_ref[sl, :])
    return x + 0.5 * acc


def _ffn_kernel(x_ref, nw_ref, wg_ref, wu_ref, wd_ref, o_ref):
    o_ref[...] = _swiglu_half_step(x_ref[...], nw_ref[...], wg_ref, wu_ref, wd_ref)


def _const_spec(shape):
    nd = len(shape)
    return pl.BlockSpec(shape, lambda *_: (0,) * nd, pipeline_mode=pl.Buffered(1))


def _ffn1(x2d, nw, wg, wu, wd):
    t = x2d.shape[0]
    row = pl.BlockSpec((TOKEN_TILE, D_MODEL), lambda i: (i, 0))
    return pl.pallas_call(
        _ffn_kernel,
        grid=(t // TOKEN_TILE,),
        in_specs=[row, _const_spec((1, D_MODEL)), _const_spec((D_MODEL, D_FF)),
                  _const_spec((D_MODEL, D_FF)), _const_spec((D_FF, D_MODEL))],
        out_specs=row,
        out_shape=jax.ShapeDtypeStruct((t, D_MODEL), F32),
        compiler_params=pltpu.CompilerParams(dimension_semantics=("arbitrary",),
                                             vmem_limit_bytes=VMEM_LIMIT),
        name="ffn1",
    )(x2d, nw, wg, wu, wd)


def _rope(x, cos_t, sin_a, sin_b):
    return (x * cos_t + pltpu.roll(x, LANES - ROPE_DIM // 2, axis=1) * sin_a
            + pltpu.roll(x, ROPE_DIM // 2, axis=1) * sin_b)


def _inproj_kernel(x_ref, nw_ref, wm_ref, ws_ref, cos_ref, sa_ref, sb_ref, alog_ref, dt_ref,
                   qkv_ref, z_ref, small_ref, gt_ref, qp_ref, qt_ref, kvc_ref,
                   ks_ref, vst_ref, kw_ref, vwt_ref):
    hb = _rms(x_ref[...], nw_ref[...]).astype(BF16)
    y = _dot(hb, wm_ref[...])
    tm = x_ref.shape[0]
    lane = lax.broadcasted_iota(jnp.int32, (tm, LANES), 1)
    sm = _dot(hb, ws_ref[...])
    xg = sm + dt_ref[...]
    softplus = jnp.maximum(xg, 0.0) + jnp.log1p(jnp.exp(-jnp.abs(xg)))
    small = jnp.where(lane < GDN_HEADS, -jnp.exp(alog_ref[...]) * softplus, _sigmoid(sm))
    small_ref[...] = small
    gt_ref[...] = small.T[:gt_ref.shape[0], :]
    qkv_ref[...] = y[:, :W_GDN_QKV]
    o = W_GDN_QKV
    z_ref[...] = y[:, o:o + W_GDN_Z]
    o += W_GDN_Z
    cos_t, sin_a, sin_b = cos_ref[...], sa_ref[...], sb_ref[...]
    scale = NSA_DH ** -0.5
    pos = pl.program_id(1) * tm + lax.broadcasted_iota(jnp.int32, (tm, LANES), 0)
    low_half = lane < NSA_DH
    block_onehot = jnp.where(lane - NSA_DH == pos // SEL_BLOCK, 1.0, 0.0)

    def halves(slab):
        return slab, pltpu.roll(slab, NSA_DH, axis=1)

    for s in range(W_NSA_Q // LANES):
        slab = y[:, o + s * LANES:o + (s + 1) * LANES]
        plain = halves(slab * scale)
        rot_t = (_rope(slab, cos_t, sin_a, sin_b) * scale).T
        for j in range(2):
            qp_ref[2 * s + j] = jnp.where(low_half, plain[j], 0.0).astype(BF16)
            qt_ref[2 * s + j] = rot_t[j * NSA_DH:(j + 1) * NSA_DH, :].astype(BF16)
    o += W_NSA_Q
    kvc_ref[...] = y[:, o:o + W_NSA_KV]
    o += W_NSA_KV
    n_tail = vst_ref.shape[1] - NSA_DH
    tail = jnp.where(lax.broadcasted_iota(jnp.int32, (n_tail, tm), 0) == 0, 1.0, 0.0)
    for k_ref, vt_ref, k_pad in ((ks_ref, vst_ref, block_onehot), (kw_ref, vwt_ref, 0.0)):
        kr = halves(_rope(y[:, o:o + LANES], cos_t, sin_a, sin_b))
        v_t = y[:, o + LANES:o + 2 * LANES].T
        for j in range(NSA_KV):
            k_ref[j] = jnp.where(low_half, kr[j], k_pad).astype(BF16)
            vt_ref[j] = jnp.concatenate([v_t[j * NSA_DH:(j + 1) * NSA_DH, :], tail], axis=0).astype(BF16)
        o += W_NSA_KV


def _in_proj(x3d, nw, w_main, w_small, cos_t, sin_a, sin_b, alog_p, dt_p):
    b, s, _ = x3d.shape
    tm = TOKEN_TILE
    row = lambda w: pl.BlockSpec((None, tm, w), lambda bi, i: (bi, i, 0))
    heads = lambda n, w: pl.BlockSpec((None, n, tm, w), lambda bi, i: (bi, 0, i, 0))
    heads_t = lambda n, r: pl.BlockSpec((None, n, r, tm), lambda bi, i: (bi, 0, 0, i))
    tab = pl.BlockSpec((tm, LANES), lambda bi, i: (i, 0))
    bf_heads = lambda n, w: jax.ShapeDtypeStruct((b, n, s, w), BF16)
    out_shape = (
        jax.ShapeDtypeStruct((b, s, W_GDN_QKV), F32),
        jax.ShapeDtypeStruct((b, s, W_GDN_Z), F32),
        jax.ShapeDtypeStruct((b, s, LANES), F32),
        jax.ShapeDtypeStruct((b, NSA_GATE_ROWS, s), F32),
        bf_heads(NSA_HEADS, LANES),
        jax.ShapeDtypeStruct((b, NSA_HEADS, NSA_DH, s), BF16),
        jax.ShapeDtypeStruct((b, s, W_NSA_KV), F32),
        bf_heads(NSA_KV, LANES),
        jax.ShapeDtypeStruct((b, NSA_KV, NSA_VT_ROWS, s), BF16),
        bf_heads(NSA_KV, LANES),
        jax.ShapeDtypeStruct((b, NSA_KV, NSA_VT_ROWS, s), BF16),
    )
    return pl.pallas_call(
        _inproj_kernel,
        grid=(b, s // tm),
        in_specs=[row(D_MODEL), _const_spec((1, D_MODEL)), _const_spec((D_MODEL, W_MAIN)),
                  _const_spec((D_MODEL, LANES)), tab, tab, tab,
                  _const_spec((1, LANES)), _const_spec((1, LANES))],
        out_specs=(row(W_GDN_QKV), row(W_GDN_Z), row(LANES),
                   pl.BlockSpec((None, NSA_GATE_ROWS, tm), lambda bi, i: (bi, 0, i)),
                   heads(NSA_HEADS, LANES), heads_t(NSA_HEADS, NSA_DH), row(W_NSA_KV),
                   heads(NSA_KV, LANES), heads_t(NSA_KV, NSA_VT_ROWS),
                   heads(NSA_KV, LANES), heads_t(NSA_KV, NSA_VT_ROWS)),
        out_shape=out_shape,
        compiler_params=pltpu.CompilerParams(dimension_semantics=("arbitrary", "arbitrary"),
                                             vmem_limit_bytes=VMEM_LIMIT),
        name="in_proj",
    )(x3d, nw, w_main, w_small, cos_t, sin_a, sin_b, alog_p, dt_p)


def _bmm(a, b):
    return jnp.einsum('nij,njk->nik', a, b, preferred_element_type=F32)


def _bmm_nt(a, b):
    return jnp.einsum('nid,njd->nij', a, b, preferred_element_type=F32)


def _unit_lower_inverse_minus_eye(low, ii, jj):
    mm = lambda a, b: _bmm(a.astype(BF16), b.astype(BF16))
    same16 = (ii // 16) == (jj // 16)
    same32 = (ii // 32) == (jj // 32)
    ld = jnp.where(same16, low, 0.0)
    e = -ld
    p = mm(ld, ld)
    for level in range(3):
        e = e + p + mm(e, p)
        if level < 2:
            p = mm(p, p)
    for within in (same32 & jnp.logical_not(same16), jnp.logical_not(same32)):
        c = jnp.where(within, low, 0.0)
        y = c + mm(e, c)
        e = e - y - mm(y, e)
    return e


def _gdn_kernel(xq_ref, xk_ref, xv_ref, wq_ref, wk_ref, wv_ref, small_ref, z_ref,
                onorm_ref, o_ref,
                mp_s, n_s, qt_s, o0_s, egl_s, sall_s):
    hp = pl.program_id(1)
    s_len = xq_ref.shape[0]
    c_len = GDN_CHUNK
    grp = GDN_GROUP
    n_chunks = s_len // c_len
    n_grp = s_len // grp
    hd = GDN_DK
    row = lax.broadcasted_iota(jnp.int32, (s_len, LANES), 0)
    lane = lax.broadcasted_iota(jnp.int32, (s_len, LANES), 1)
    pos = row % c_len
    ii = lax.broadcasted_iota(jnp.int32, (grp, grp), 0)
    jj = lax.broadcasted_iota(jnp.int32, (grp, grp), 1)
    same_chunk = (ii // c_len) == (jj // c_len)
    tril = same_chunk & (ii >= jj)
    strict = same_chunk & (ii > jj)
    first = lax.broadcasted_iota(jnp.int32, (grp, hd), 0) < c_len
    small = small_ref[...]
    g3 = lambda a: a.reshape(n_grp, grp, hd)

    top = GDN_CONV * 2
    row_top = lax.broadcasted_iota(jnp.int32, (top, hd), 0)

    def conv_silu(x_ref, w_ref, cols):
        w = w_ref[:, cols]
        x_top = x_ref[:top, cols]
        acc_top = x_top * w[GDN_CONV - 1:GDN_CONV, :]
        acc = x_ref[top:, cols] * w[GDN_CONV - 1:GDN_CONV, :]
        for sh in range(1, GDN_CONV):
            tap = w[GDN_CONV - 1 - sh:GDN_CONV - sh, :]
            acc_top = acc_top + jnp.where(row_top >= sh, pltpu.roll(x_top, sh, axis=0), 0.0) * tap
            acc = acc + x_ref[pl.ds(top - sh, s_len - top), cols] * tap
        return _silu(jnp.concatenate([acc_top, acc], axis=0))

    def l2n(x):
        return x * lax.rsqrt(jnp.sum(x * x, axis=-1, keepdims=True) + EPS)

    for j in range(GDN_HPS):
        h = hp * GDN_HPS + j
        cols = slice(j * hd, (j + 1) * hd)
        q = l2n(conv_silu(xq_ref, wq_ref, cols)) * (GDN_DK ** -0.5)
        k = l2n(conv_silu(xk_ref, wk_ref, cols))
        v = conv_silu(xv_ref, wv_ref, cols)

        pick = lambda ln: jnp.broadcast_to(
            jnp.sum(jnp.where(lane == ln, small, 0.0), axis=-1, keepdims=True), (s_len, LANES))
        beta = pick(GDN_HEADS + h)
        g = pick(h)
        sh = 1
        while sh < c_len:
            g = g + jnp.where(pos >= sh, pltpu.roll(g, sh, axis=0), 0.0)
            sh *= 2
        gc = g
        gc3 = gc.reshape(n_chunks, c_len, LANES)
        gl3 = gc3[:, c_len - 1:c_len, :]
        egl_s[j] = jnp.exp(gl3)
        gl = jnp.broadcast_to(gl3, gc3.shape).reshape(s_len, LANES)
        eg = jnp.exp(gc)
        kb = k * beta
        vb = g3(v * beta)
        kbg = g3(kb * eg)

        k3 = g3(k).astype(BF16)
        kk = _bmm_nt(g3(kb).astype(BF16), k3)
        qk = _bmm_nt(g3(q).astype(BF16), k3)
        gcg = g3(gc)
        diff = gcg - jnp.swapaxes(gcg, 1, 2)
        decay = jnp.where(tril, jnp.exp(jnp.where(tril, diff, 0.0)), 0.0)
        e_b = _unit_lower_inverse_minus_eye(jnp.where(strict, kk * decay, 0.0), ii, jj).astype(BF16)
        u = vb + _bmm(e_b, vb.astype(BF16))
        w = kbg + _bmm(e_b, kbg.astype(BF16))
        qkm = jnp.where(tril, qk * decay, 0.0).astype(BF16)
        r = _bmm(qkm, jnp.concatenate([w, u], axis=-1).astype(BF16))
        qt_s[j] = (g3(q * eg) - r[..., :hd]).astype(BF16)
        o0_s[j] = r[..., hd:]
        kd_t = jnp.swapaxes(g3(k * jnp.exp(gl - gc)), 1, 2).astype(BF16)
        rhs = jnp.concatenate([jnp.where(first, w, 0.0), jnp.where(first, 0.0, w),
                               jnp.where(first, u, 0.0), jnp.where(first, 0.0, u)], axis=-1)
        mn = _bmm(kd_t, rhs.astype(BF16))
        mp_s[j] = (-mn[..., :2 * hd]).astype(BF16)
        n_s[j] = mn[..., 2 * hd:]

    def scan(n, states):
        out = []
        for j in range(GDN_HPS):
            st = states[j]
            for half in range(grp // c_len):
                c = n * (grp // c_len) + half
                lanes = slice(half * hd, (half + 1) * hd)
                sb = st.astype(BF16)
                sall_s[j, c] = sb
                st = egl_s[j, c] * st + _dot(mp_s[j, n, :, lanes], sb) + n_s[j, n, :, lanes]
            out.append(st)
        return tuple(out)

    lax.fori_loop(0, n_grp, scan, tuple(jnp.zeros((GDN_DK, GDN_DV), F32) for _ in range(GDN_HPS)))

    for j in range(GDN_HPS):
        cols = slice(j * hd, (j + 1) * hd)
        qt = qt_s[j].reshape(n_chunks, c_len, hd)
        o = _bmm(qt, sall_s[j]) + o0_s[j].reshape(n_chunks, c_len, hd)
        o_ref[:, cols] = _rms(o.reshape(s_len, hd), onorm_ref[...]) * _silu(z_ref[:, cols])


def _gdn(qkv, z, small, conv_w, onorm):
    b, s, _ = qkv.shape
    wid = GDN_HPS * GDN_DK
    n_hp = GDN_HEADS // GDN_HPS
    n_grp, n_chunks = s // GDN_GROUP, s // GDN_CHUNK
    col = lambda off: pl.BlockSpec((None, s, wid), lambda bi, h: (bi, 0, off + h))
    wcol = lambda off: pl.BlockSpec((GDN_CONV, wid), lambda bi, h: (0, off + h))
    vec = pl.BlockSpec((1, LANES), lambda bi, h: (0, 0))
    return pl.pallas_call(
        _gdn_kernel,
        grid=(b, n_hp),
        in_specs=[col(0), col(n_hp), col(2 * n_hp),
                  wcol(0), wcol(n_hp), wcol(2 * n_hp),
                  pl.BlockSpec((None, s, LANES), lambda bi, h: (bi, 0, 0)),
                  col(0), vec],
        out_specs=col(0),
        out_shape=jax.ShapeDtypeStruct((b, s, GDN_HEADS * GDN_DV), F32),
        scratch_shapes=[pltpu.VMEM((GDN_HPS, n_grp, GDN_DK, 2 * GDN_DV), BF16),
                        pltpu.VMEM((GDN_HPS, n_grp, GDN_DK, 2 * GDN_DV), F32),
                        pltpu.VMEM((GDN_HPS, n_grp, GDN_GROUP, GDN_DK), BF16),
                        pltpu.VMEM((GDN_HPS, n_grp, GDN_GROUP, GDN_DV), F32),
                        pltpu.VMEM((GDN_HPS, n_chunks, 1, LANES), F32),
                        pltpu.VMEM((GDN_HPS, n_chunks, GDN_DK, GDN_DV), BF16)],
        compiler_params=pltpu.CompilerParams(dimension_semantics=("arbitrary", "arbitrary"),
                                             vmem_limit_bytes=VMEM_LIMIT),
        name="gdn",
    )(qkv, qkv, qkv, conv_w, conv_w, conv_w, small, z, onorm)


def _compress_kernel(t_ref, pet_ref, peb_ref, wt_ref, wb_ref, w2_ref, o_ref):
    t = t_ref[...]
    a = _dot((t + pet_ref[...]).astype(BF16), wt_ref[...])
    bm = _dot((t + peb_ref[...]).astype(BF16), wb_ref[...])
    n_rows = t.shape[0]
    hid = a + pltpu.roll(bm, n_rows - 1, axis=0)
    out = _dot(_silu(hid).astype(BF16), w2_ref[...])
    for j in range(2 * NSA_KV):
        o_ref[j] = out[:, j * LANES:(j + 1) * LANES].astype(o_ref.dtype)


def _compress(tm, pe_top, pe_bot, w_top, w_bot, w2):
    b, n_rows, width = tm.shape
    nh = 2 * NSA_KV * CMP_HIDDEN
    return pl.pallas_call(
        _compress_kernel,
        grid=(b,),
        in_specs=[pl.BlockSpec((None, n_rows, width), lambda bi: (bi, 0, 0)),
                  _const_spec((1, width)), _const_spec((1, width)),
                  _const_spec((width, nh)), _const_spec((width, nh)),
                  _const_spec((nh, 2 * NSA_KV * LANES))],
        out_specs=pl.BlockSpec((None, 2 * NSA_KV, n_rows, LANES), lambda bi: (bi, 0, 0, 0)),
        out_shape=jax.ShapeDtypeStruct((b, 2 * NSA_KV, n_rows, LANES), BF16),
        compiler_params=pltpu.CompilerParams(dimension_semantics=("arbitrary",),
                                             vmem_limit_bytes=VMEM_LIMIT),
        name="compress",
    )(tm, pe_top, pe_bot, w_top, w_bot, w2)


def _gate_columns(gates, lane, base):
    return [jnp.sum(jnp.where(lane == base + r, gates, 0.0), axis=-1, keepdims=True) for r in range(3)]


def _pack_heads(heads, lane):
    pair = lambda a, b: jnp.where(lane < NSA_DH, a, pltpu.roll(b, NSA_DH, axis=1))
    return jnp.concatenate([pair(heads[0], heads[1]), pair(heads[2], heads[3])], axis=1)


def _nsa_select_kernel(qp_ref, kc_ref, vc_ref, small_ref, ocmp_ref, bias_ref):
    grp = pl.program_id(1)
    ts, hpg = qp_ref.shape[1], NSA_HPG
    q0 = pl.program_id(2) * ts
    rows = hpg * ts
    n_cmp = kc_ref.shape[0]
    n_blk = n_cmp * CMP_STRIDE // SEL_BLOCK

    t_c = q0 + lax.broadcasted_iota(jnp.int32, (ts, n_cmp), 0)
    n_c = lax.broadcasted_iota(jnp.int32, (ts, n_cmp), 1)
    vis = n_c * CMP_STRIDE + (CMP_BLOCK - 1) <= t_c
    s_cmp = _dot_nt(qp_ref[...].reshape(rows, LANES), kc_ref[...]).reshape(hpg, ts, n_cmp)
    s_cmp = jnp.where(vis[None], s_cmp, NEG)
    p_cmp = jnp.exp(s_cmp - jnp.max(s_cmp, axis=-1, keepdims=True)) * vis[None].astype(F32)
    p_cmp = p_cmp * (1.0 / jnp.maximum(jnp.sum(p_cmp, axis=-1, keepdims=True), 1e-30))
    o_cmp = _dot(p_cmp.reshape(rows, n_cmp).astype(BF16), vc_ref[...]).reshape(hpg, ts, LANES)
    lane = lax.broadcasted_iota(jnp.int32, (ts, LANES), 1)
    gates = small_ref[...]
    ocmp_ref[...] = _pack_heads(
        [_gate_columns(gates, lane, GATE_LANE0 + (grp * hpg + hh) * 3)[0] * o_cmp[hh] for hh in range(hpg)],
        lane)

    p_sum = p_cmp[0]
    for hh in range(1, hpg):
        p_sum = p_sum + p_cmp[hh]
    oj = lax.broadcasted_iota(jnp.int32, (n_blk, n_cmp), 0) * SEL_BLOCK
    on = lax.broadcasted_iota(jnp.int32, (n_blk, n_cmp), 1) * CMP_STRIDE
    overlap_t = jnp.where(on < oj + SEL_BLOCK, jnp.where(on + CMP_BLOCK > oj, 1.0, 0.0), 0.0)
    imp_t = lax.dot_general(overlap_t, p_sum, (((1,), (1,)), ((), ())),
                            precision=lax.Precision.HIGHEST, preferred_element_type=F32)
    blk = lax.broadcasted_iota(jnp.int32, (n_blk, ts), 0)
    ahead = (q0 + lax.broadcasted_iota(jnp.int32, (n_blk, ts), 1)) // SEL_BLOCK - blk
    score = jnp.where(ahead < 0, NEG, jnp.where(blk == 0, BIG, jnp.where(ahead <= 1, BIG, imp_t)))
    rank = jnp.zeros((n_blk, ts), F32)
    for i in range(n_blk):
        ci = jnp.broadcast_to(score[i:i + 1, :], (n_blk, ts))
        tie = jnp.where(blk > i, 1.0, 0.0)
        rank = rank + jnp.where(ci > score, 1.0, jnp.where(ci == score, tie, 0.0))
    bias_ref[...] = jnp.where(rank >= SEL_TOPK, NEG, 0.0).astype(bias_ref.dtype)


def _nsa_select(qp, kcvc, small):
    b, _, s, _ = qp.shape
    ts = NSA_TS
    n_cmp = kcvc.shape[2]
    cspec = lambda off: pl.BlockSpec((None, None, n_cmp, LANES), lambda bi, g, i: (bi, off + g, 0, 0))
    return pl.pallas_call(
        _nsa_select_kernel,
        grid=(b, NSA_KV, s // ts),
        in_specs=[pl.BlockSpec((None, NSA_HPG, ts, LANES), lambda bi, g, i: (bi, g, i, 0)),
                  cspec(0), cspec(NSA_KV),
                  pl.BlockSpec((None, ts, LANES), lambda bi, g, i: (bi, i, 0))],
        out_specs=(pl.BlockSpec((None, ts, NSA_HPG * NSA_DH), lambda bi, g, i: (bi, i, g)),
                   pl.BlockSpec((None, None, s // SEL_BLOCK, ts), lambda bi, g, i: (bi, g, 0, i))),
        out_shape=(jax.ShapeDtypeStruct((b, s, W_NSA_Q), F32),
                   jax.ShapeDtypeStruct((b, NSA_KV, s // SEL_BLOCK, s), BF16)),
        compiler_params=pltpu.CompilerParams(
            dimension_semantics=("arbitrary", "arbitrary", "arbitrary"),
            vmem_limit_bytes=VMEM_LIMIT),
        name="nsa_select",
    )(qp, kcvc, kcvc, small)


def _nsa_kernel(qt_ref, bias_ref, ks_ref, vst_ref, kw_ref, vwt_ref, gt_ref, ocmp_ref, o_ref, acc_s):
    grp = pl.program_id(1)
    q0 = pl.program_id(2) * NSA_TQ
    tq, hpg, dh = NSA_TQ, NSA_HPG, NSA_DH
    bias = bias_ref[...]
    pad = jnp.zeros((LANES - dh - bias.shape[0], tq), BF16)
    q_aug = jnp.concatenate(
        [jnp.concatenate([qt_ref[hh], bias, pad], axis=0) for hh in range(hpg)], axis=1)
    per_head = lambda a: jnp.concatenate([a] * hpg, axis=1)

    def attend(s_parts, vt_parts):
        m = jnp.max(s_parts[0], axis=0, keepdims=True)
        for s_t in s_parts[1:]:
            m = jnp.maximum(m, jnp.max(s_t, axis=0, keepdims=True))
        out = None
        for s_t, v_t in zip(s_parts, vt_parts):
            part = _dot(v_t, jnp.exp(s_t - m).astype(BF16))
            out = part if out is None else out + part
        return out

    def normalised(o_t):
        return o_t[:dh, :] * (1.0 / jnp.maximum(o_t[dh:dh + 1, :], 1e-30))

    wk = WINDOW + tq
    w0 = pl.multiple_of(jnp.maximum(q0 - WINDOW, 0), tq)
    back = (q0 + lax.broadcasted_iota(jnp.int32, (wk, tq), 1)
            - w0 - lax.broadcasted_iota(jnp.int32, (wk, tq), 0))
    band = per_head(jnp.where(back < 0, NEG, jnp.where(back < WINDOW, 0.0, NEG)))
    o_win = normalised(attend([_dot(kw_ref[pl.ds(w0, wk), :], q_aug) + band],
                              [vwt_ref[:, pl.ds(w0, wk)]]))

    kt = SLC_KT
    last = (q0 + tq - 1) // kt
    ahead = (lax.broadcasted_iota(jnp.int32, (kt, tq), 0)
             - q0 - lax.broadcasted_iota(jnp.int32, (kt, tq), 1))

    def slc_prefix(n_chunks):
        below = (n_chunks - 1) * kt
        causal = per_head(jnp.where(below + ahead <= 0, 0.0, NEG))
        s_parts = [_dot(ks_ref[below:below + kt, :], q_aug) + causal]
        vt_parts = [vst_ref[:, below:below + kt]]
        if below:
            s_parts.append(_dot(ks_ref[:below, :], q_aug))
            vt_parts.append(vst_ref[:, :below])
        acc_s[...] = attend(s_parts, vt_parts)

    for n_chunks in range(1, ks_ref.shape[0] // kt + 1):
        pl.when(last == n_chunks - 1)(functools.partial(slc_prefix, n_chunks))
    o_slc = normalised(acc_s[...])

    mixed = []
    for hh in range(hpg):
        base = GATE_LANE0 + (grp * hpg + hh) * 3
        head = slice(hh * tq, (hh + 1) * tq)
        mixed.append(gt_ref[pl.ds(base + 1, 1), :] * o_slc[:, head]
                     + gt_ref[pl.ds(base + 2, 1), :] * o_win[:, head])
    o_ref[...] = (ocmp_ref[...] + jnp.concatenate(mixed, axis=0).T).astype(o_ref.dtype)


def _nsa_attention(qt, sel_bias, ks, vst, kw, vwt, gates_t, o_cmp):
    b, _, _, s = qt.shape
    tq = NSA_TQ
    kspec = pl.BlockSpec((None, None, s, LANES), lambda bi, g, i: (bi, g, 0, 0))
    vspec = pl.BlockSpec((None, None, NSA_VT_ROWS, s), lambda bi, g, i: (bi, g, 0, 0))
    tok = pl.BlockSpec((None, tq, NSA_HPG * NSA_DH), lambda bi, g, i: (bi, i, g))
    return pl.pallas_call(
        _nsa_kernel,
        grid=(b, NSA_KV, s // tq),
        in_specs=[pl.BlockSpec((None, NSA_HPG, NSA_DH, tq), lambda bi, g, i: (bi, g, 0, i)),
                  pl.BlockSpec((None, None, s // SEL_BLOCK, tq), lambda bi, g, i: (bi, g, 0, i)),
                  kspec, vspec, kspec, vspec,
                  pl.BlockSpec((None, NSA_GATE_ROWS, tq), lambda bi, g, i: (bi, 0, i)), tok],
        out_specs=tok,
        out_shape=jax.ShapeDtypeStruct((b, s, W_NSA_Q), BF16),
        scratch_shapes=[pltpu.VMEM((NSA_VT_ROWS, NSA_HPG * tq), F32)],
        compiler_params=pltpu.CompilerParams(
            dimension_semantics=("arbitrary", "arbitrary", "arbitrary"),
            vmem_limit_bytes=VMEM_LIMIT),
        name="nsa_attn",
    )(qt, sel_bias, ks, vst, kw, vwt, gates_t, o_cmp)


def _out_ffn2_kernel(x_ref, og_ref, on_ref, wo_ref, nw_ref, wg_ref, wu_ref, wd_ref, fw_ref, o_ref):
    half = W_GDN_Z
    x = (x_ref[...] + _dot(og_ref[...].astype(BF16), wo_ref[:half, :])
         + _dot(on_ref[...], wo_ref[half:, :]))
    y = _swiglu_half_step(x, nw_ref[...], wg_ref, wu_ref, wd_ref)
    o_ref[...] = _rms(y, fw_ref[...])


def _out_ffn2(x2d, o_gdn, o_nsa, w_out, nw, wg, wu, wd, fw):
    t = x2d.shape[0]
    row = lambda w: pl.BlockSpec((TOKEN_TILE, w), lambda i: (i, 0))
    return pl.pallas_call(
        _out_ffn2_kernel,
        grid=(t // TOKEN_TILE,),
        in_specs=[row(D_MODEL), row(W_GDN_Z), row(W_NSA_Q), _const_spec((D_MODEL, D_MODEL)),
                  _const_spec((1, D_MODEL)), _const_spec((D_MODEL, D_FF)),
                  _const_spec((D_MODEL, D_FF)), _const_spec((D_FF, D_MODEL)),
                  _const_spec((1, D_MODEL))],
        out_specs=row(D_MODEL),
        out_shape=jax.ShapeDtypeStruct((t, D_MODEL), F32),
        compiler_params=pltpu.CompilerParams(dimension_semantics=("arbitrary",),
                                             vmem_limit_bytes=VMEM_LIMIT),
        name="out_ffn2",
    )(x2d, o_gdn, o_nsa, w_out, nw, wg, wu, wd, fw)


def _rope_tables(s_len):
    half = ROPE_DIM // 2
    inv = ROPE_THETA ** (-jnp.arange(half, dtype=F32) / half)
    ang = jnp.arange(s_len, dtype=F32)[:, None] * inv[None, :]
    cos, sin = jnp.cos(ang), jnp.sin(ang)
    pad = jnp.zeros((s_len, NSA_DH - ROPE_DIM), F32)
    zero = jnp.zeros_like(sin)
    cos_h = jnp.concatenate([cos, cos, pad + 1.0], axis=1)
    sa_h = jnp.concatenate([-sin, zero, pad], axis=1)
    sb_h = jnp.concatenate([zero, sin, pad], axis=1)
    rep = lambda a: jnp.tile(a, (1, LANES // NSA_DH))
    return rep(cos_h), rep(sa_h), rep(sb_h)


def _split_w_in(w_in):
    cuts = np.cumsum([W_GDN_QKV, W_GDN_Z, GDN_HEADS, GDN_HEADS, W_NSA_Q,
                      W_NSA_KV, W_NSA_KV, W_NSA_KV]).tolist()
    qkv, z, a, bb, nq, kvc, kvs, kvw, gate = jnp.split(w_in, cuts, axis=1)
    w_main = jnp.concatenate([qkv, z, nq, kvc, kvs, kvw], axis=1).astype(BF16)
    small = jnp.concatenate([a, bb, gate], axis=1)
    w_small = jnp.pad(small, ((0, 0), (0, LANES - small.shape[1]))).astype(BF16)
    return w_main, w_small


def _compress_weights(pe_k, w1_k, w2_k, pe_v, w1_v, w2_v):
    half = CMP_BLOCK // 2
    eye = jnp.eye(2 * NSA_KV, dtype=F32)

    def first_layer(lo):
        w1 = jnp.stack([w1_k, w1_v]).reshape(2, CMP_BLOCK, NSA_DH, CMP_HIDDEN)[:, lo:lo + half]
        w1 = jnp.repeat(w1, NSA_KV, axis=0)
        w = jnp.einsum('crdj,ce->rcdej', w1, eye)
        return w.reshape(half * 2 * NSA_KV * NSA_DH, 2 * NSA_KV * CMP_HIDDEN).astype(BF16)

    def pe_row(lo):
        pe = jnp.stack([pe_k, pe_v])[:, lo:lo + half]
        pe = jnp.repeat(pe, NSA_KV, axis=0)
        return pe.transpose(1, 0, 2).reshape(1, half * 2 * NSA_KV * NSA_DH)

    w2 = jnp.repeat(jnp.stack([w2_k, w2_v]), NSA_KV, axis=0)
    w2 = jnp.pad(w2, ((0, 0), (0, 0), (0, LANES - NSA_DH)))
    w2 = jnp.einsum('cjd,ce->cjed', w2, eye).reshape(2 * NSA_KV * CMP_HIDDEN, 2 * NSA_KV * LANES)
    return pe_row(0), pe_row(half), first_layer(0), first_layer(half), w2.astype(BF16)


def kernel(x, ffn1_norm, ffn1_w_gate, ffn1_w_up, ffn1_w_down, mix_norm, w_in, gdn_conv_w, gdn_a_log,
           gdn_dt_bias, gdn_out_norm, cmp_pe_k, cmp_w1_k, cmp_w2_k, cmp_pe_v, cmp_w1_v, cmp_w2_v,
           w_out, ffn2_norm, ffn2_w_gate, ffn2_w_up, ffn2_w_down, final_norm):
    b, s, d = x.shape
    assert d == D_MODEL and s % TOKEN_TILE == 0 and ffn1_norm.shape[0] == 1
    vec = lambda a: a.reshape(1, -1)
    pad_lanes = lambda a: jnp.pad(a.reshape(1, -1), ((0, 0), (0, LANES - a.size)))
    bf = lambda a: a.astype(BF16)

    x1 = _ffn1(x.reshape(b * s, d), vec(ffn1_norm[0]), bf(ffn1_w_gate[0]), bf(ffn1_w_up[0]),
               bf(ffn1_w_down[0]))

    w_main, w_small = _split_w_in(w_in[0])
    cos_t, sin_a, sin_b = _rope_tables(s)
    qkv, z, small, gates_t, qp, qt, kvc, ks, vst, kw, vwt = _in_proj(
        x1.reshape(b, s, d), vec(mix_norm[0]), w_main, w_small, cos_t, sin_a, sin_b,
        pad_lanes(gdn_a_log[0]), pad_lanes(gdn_dt_bias[0]))

    o_gdn = _gdn(qkv, z, small, gdn_conv_w[0], vec(gdn_out_norm[0]))

    pe_top, pe_bot, w_top, w_bot, w2 = _compress_weights(
        cmp_pe_k[0], cmp_w1_k[0], cmp_w2_k[0], cmp_pe_v[0], cmp_w1_v[0], cmp_w2_v[0])
    tokens16 = kvc.reshape(b, s // CMP_STRIDE, CMP_STRIDE * W_NSA_KV)
    kcvc = _compress(tokens16, pe_top, pe_bot, w_top, w_bot, w2)

    o_cmp, sel_bias = _nsa_select(qp, kcvc, small)
    o_nsa = _nsa_attention(qt, sel_bias, ks, vst, kw, vwt, gates_t, o_cmp).reshape(b * s, W_NSA_Q)

    out = _out_ffn2(x1, o_gdn.reshape(b * s, W_GDN_Z), o_nsa, bf(w_out[0]), vec(ffn2_norm[0]),
                    bf(ffn2_w_gate[0]), bf(ffn2_w_up[0]), bf(ffn2_w_down[0]), vec(final_norm))
    return out.reshape(b, s, d)
```

```python
import functools

import numpy as np
import jax
import jax.numpy as jnp
from jax import lax
from jax.experimental import pallas as pl
from jax.experimental.pallas import tpu as pltpu

F32 = jnp.float32
BF16 = jnp.bfloat16

D_MODEL = 1024
D_FF = 2816
GDN_HEADS = 4
GDN_DK = 128
GDN_DV = 128
GDN_CONV = 4
GDN_CHUNK = 64
NSA_HEADS = 8
NSA_KV = 2
NSA_HPG = NSA_HEADS // NSA_KV
NSA_DH = 64
CMP_BLOCK = 32
CMP_STRIDE = 16
CMP_HIDDEN = 128
SEL_BLOCK = 64
SEL_TOPK = 16
WINDOW = 512
ROPE_THETA = 500000.0
ROPE_DIM = NSA_DH // 4
EPS = 1e-6
NEG = -1e30
LOG2E = 1.4426950408889634
BIG = 1e30

W_GDN_QKV = GDN_HEADS * (2 * GDN_DK + GDN_DV)
W_GDN_Z = GDN_HEADS * GDN_DV
W_NSA_Q = NSA_HEADS * NSA_DH
W_NSA_KV = 2 * NSA_KV * NSA_DH
W_NSA_GATE = 3 * NSA_HEADS
W_MAIN = W_GDN_QKV + W_GDN_Z + W_NSA_Q + 3 * W_NSA_KV
LANES = 128
GATE_LANE0 = 2 * GDN_HEADS

TOKEN_TILE = 512
FF_CHUNK = D_FF // 2
GDN_HPS = 2
GDN_GROUP = 2 * GDN_CHUNK
NSA_TQ = 128
NSA_TS = 512
SLC_KT = 256
NSA_VT_ROWS = 80
NSA_GATE_ROWS = 32
VMEM_LIMIT = 56 * 1024 * 1024


def _dot(a, b):
    return jnp.dot(a, b, preferred_element_type=F32)


def _dot_nt(a, b):
    return lax.dot_general(a, b, (((1,), (1,)), ((), ())), preferred_element_type=F32)


def _dot_tn(a, b):
    return lax.dot_general(a, b, (((0,), (0,)), ((), ())), preferred_element_type=F32)


def _rms(x, w):
    return x * lax.rsqrt(jnp.mean(x * x, axis=-1, keepdims=True) + EPS) * w


def _sigmoid(x):
    return 0.5 * jnp.tanh(0.5 * x) + 0.5


def _silu(x):
    return x * _sigmoid(x)


def _swiglu_half_step(x, nw, wg_ref, wu_ref, wd_ref):
    hb = _rms(x, nw).astype(BF16)
    acc = jnp.zeros_like(x)
    for c in range(D_FF // FF_CHUNK):
        sl = slice(c * FF_CHUNK, (c + 1) * FF_CHUNK)
        g = _dot(hb, wg_ref[:, sl])
        u = _dot(hb, wu_ref[:, sl])
        acc = acc + _dot((_silu(g) * u).astype(BF16), wd_ref[sl, :])
    return x + 0.5 * acc


def _ffn_kernel(x_ref, nw_ref, wg_ref, wu_ref, wd_ref, o_ref):
    o_ref[...] = _swiglu_half_step(x_ref[...], nw_ref[...], wg_ref, wu_ref, wd_ref)


def _const_spec(shape):
    nd = len(shape)
    return pl.BlockSpec(shape, lambda *_: (0,) * nd, pipeline_mode=pl.Buffered(1))


def _ffn1(x2d, nw, wg, wu, wd):
    t = x2d.shape[0]
    row = pl.BlockSpec((TOKEN_TILE, D_MODEL), lambda i: (i, 0))
    return pl.pallas_call(
        _ffn_kernel,
        grid=(t // TOKEN_TILE,),
        in_specs=[row, _const_spec((1, D_MODEL)), _const_spec((D_MODEL, D_FF)),
                  _const_spec((D_MODEL, D_FF)), _const_spec((D_FF, D_MODEL))],
        out_specs=row,
        out_shape=jax.ShapeDtypeStruct((t, D_MODEL), F32),
        compiler_params=pltpu.CompilerParams(dimension_semantics=("arbitrary",),
                                             vmem_limit_bytes=VMEM_LIMIT),
        name="ffn1",
    )(x2d, nw, wg, wu, wd)


def _rope(x, cos_t, sin_a, sin_b):
    return (x * cos_t + pltpu.roll(x, LANES - ROPE_DIM // 2, axis=1) * sin_a
            + pltpu.roll(x, ROPE_DIM // 2, axis=1) * sin_b)


def _inproj_kernel(x_ref, nw_ref, wm_ref, ws_ref, cos_ref, sa_ref, sb_ref, alog_ref, dt_ref,
                   qkv_ref, z_ref, small_ref, gt_ref, qp_ref, qt_ref, kvc_ref,
                   ks_ref, vst_ref, kw_ref, vwt_ref):
    hb = _rms(x_ref[...], nw_ref[...]).astype(BF16)
    y = _dot(hb, wm_ref[...])
    tm = x_ref.shape[0]
    lane = lax.broadcasted_iota(jnp.int32, (tm, LANES), 1)
    sm = _dot(hb, ws_ref[...])
    xg = sm + dt_ref[...]
    softplus = jnp.maximum(xg, 0.0) + jnp.log1p(jnp.exp(-jnp.abs(xg)))
    small = jnp.where(lane < GDN_HEADS, -jnp.exp(alog_ref[...]) * softplus, _sigmoid(sm))
    small_ref[...] = small
    gt_ref[...] = small.T[:gt_ref.shape[0], :]
    qkv_ref[...] = y[:, :W_GDN_QKV]
    o = W_GDN_QKV
    z_ref[...] = y[:, o:o + W_GDN_Z]
    o += W_GDN_Z
    cos_t, sin_a, sin_b = cos_ref[...], sa_ref[...], sb_ref[...]
    scale = NSA_DH ** -0.5
    pos = pl.program_id(1) * tm + lax.broadcasted_iota(jnp.int32, (tm, LANES), 0)
    low_half = lane < NSA_DH
    block_onehot = jnp.where(lane - NSA_DH == pos // SEL_BLOCK, 1.0, 0.0)

    def halves(slab):
        return slab, pltpu.roll(slab, NSA_DH, axis=1)

    for s in range(W_NSA_Q // LANES):
        slab = y[:, o + s * LANES:o + (s + 1) * LANES]
        plain_t = (slab * scale).T
        rot_t = (_rope(slab, cos_t, sin_a, sin_b) * (scale * LOG2E)).T
        for j in range(2):
            qp_ref[2 * s + j] = plain_t[j * NSA_DH:(j + 1) * NSA_DH, :].astype(BF16)
            qt_ref[2 * s + j] = rot_t[j * NSA_DH:(j + 1) * NSA_DH, :].astype(BF16)
    o += W_NSA_Q
    kvc_ref[...] = y[:, o:o + W_NSA_KV]
    o += W_NSA_KV
    n_tail = vst_ref.shape[1] - NSA_DH
    tail = jnp.where(lax.broadcasted_iota(jnp.int32, (n_tail, tm), 0) == 0, 1.0, 0.0)
    for k_ref, vt_ref, k_pad in ((ks_ref, vst_ref, block_onehot), (kw_ref, vwt_ref, 0.0)):
        kr = halves(_rope(y[:, o:o + LANES], cos_t, sin_a, sin_b))
        v_t = y[:, o + LANES:o + 2 * LANES].T
        for j in range(NSA_KV):
            k_ref[j] = jnp.where(low_half, kr[j], k_pad).astype(BF16)
            vt_ref[j] = jnp.concatenate([v_t[j * NSA_DH:(j + 1) * NSA_DH, :], tail], axis=0).astype(BF16)
        o += W_NSA_KV


def _in_proj(x3d, nw, w_main, w_small, cos_t, sin_a, sin_b, alog_p, dt_p):
    b, s, _ = x3d.shape
    tm = TOKEN_TILE
    row = lambda w: pl.BlockSpec((None, tm, w), lambda bi, i: (bi, i, 0))
    heads = lambda n, w: pl.BlockSpec((None, n, tm, w), lambda bi, i: (bi, 0, i, 0))
    heads_t = lambda n, r: pl.BlockSpec((None, n, r, tm), lambda bi, i: (bi, 0, 0, i))
    tab = pl.BlockSpec((tm, LANES), lambda bi, i: (i, 0))
    bf_heads = lambda n, w: jax.ShapeDtypeStruct((b, n, s, w), BF16)
    out_shape = (
        jax.ShapeDtypeStruct((b, s, W_GDN_QKV), F32),
        jax.ShapeDtypeStruct((b, s, W_GDN_Z), F32),
        jax.ShapeDtypeStruct((b, s, LANES), F32),
        jax.ShapeDtypeStruct((b, NSA_GATE_ROWS, s), F32),
        jax.ShapeDtypeStruct((b, NSA_HEADS, NSA_DH, s), BF16),
        jax.ShapeDtypeStruct((b, NSA_HEADS, NSA_DH, s), BF16),
        jax.ShapeDtypeStruct((b, s, W_NSA_KV), F32),
        bf_heads(NSA_KV, LANES),
        jax.ShapeDtypeStruct((b, NSA_KV, NSA_VT_ROWS, s), BF16),
        bf_heads(NSA_KV, LANES),
        jax.ShapeDtypeStruct((b, NSA_KV, NSA_VT_ROWS, s), BF16),
    )
    return pl.pallas_call(
        _inproj_kernel,
        grid=(b, s // tm),
        in_specs=[row(D_MODEL), _const_spec((1, D_MODEL)), _const_spec((D_MODEL, W_MAIN)),
                  _const_spec((D_MODEL, LANES)), tab, tab, tab,
                  _const_spec((1, LANES)), _const_spec((1, LANES))],
        out_specs=(row(W_GDN_QKV), row(W_GDN_Z), row(LANES),
                   pl.BlockSpec((None, NSA_GATE_ROWS, tm), lambda bi, i: (bi, 0, i)),
                   heads_t(NSA_HEADS, NSA_DH), heads_t(NSA_HEADS, NSA_DH), row(W_NSA_KV),
                   heads(NSA_KV, LANES), heads_t(NSA_KV, NSA_VT_ROWS),
                   heads(NSA_KV, LANES), heads_t(NSA_KV, NSA_VT_ROWS)),
        out_shape=out_shape,
        compiler_params=pltpu.CompilerParams(dimension_semantics=("arbitrary", "arbitrary"),
                                             vmem_limit_bytes=VMEM_LIMIT),
        name="in_proj",
    )(x3d, nw, w_main, w_small, cos_t, sin_a, sin_b, alog_p, dt_p)


def _bmm(a, b):
    return jnp.einsum('nij,njk->nik', a, b, preferred_element_type=F32)


def _bmm_nt(a, b):
    return jnp.einsum('nid,njd->nij', a, b, preferred_element_type=F32)


def _unit_lower_inverse_minus_eye(low, ii, jj):
    mm = lambda a, b: _bmm(a.astype(BF16), b.astype(BF16))
    same16 = (ii // 16) == (jj // 16)
    same32 = (ii // 32) == (jj // 32)
    ld = jnp.where(same16, low, 0.0)
    e = -ld
    p = mm(ld, ld)
    for level in range(3):
        e = e + p + mm(e, p)
        if level < 2:
            p = mm(p, p)
    for within in (same32 & jnp.logical_not(same16), jnp.logical_not(same32)):
        c = jnp.where(within, low, 0.0)
        y = c + mm(e, c)
        e = e - y - mm(y, e)
    return e


def _gdn_kernel(xq_ref, xk_ref, xv_ref, wq_ref, wk_ref, wv_ref, small_ref, z_ref,
                onorm_ref, o_ref,
                mp_s, n_s, qt_s, o0_s, egl_s, sall_s):
    hp = pl.program_id(1)
    s_len = xq_ref.shape[0]
    c_len = GDN_CHUNK
    grp = GDN_GROUP
    n_chunks = s_len // c_len
    n_grp = s_len // grp
    hd = GDN_DK
    row = lax.broadcasted_iota(jnp.int32, (s_len, LANES), 0)
    lane = lax.broadcasted_iota(jnp.int32, (s_len, LANES), 1)
    pos = row % c_len
    ii = lax.broadcasted_iota(jnp.int32, (grp, grp), 0)
    jj = lax.broadcasted_iota(jnp.int32, (grp, grp), 1)
    same_chunk = (ii // c_len) == (jj // c_len)
    tril = same_chunk & (ii >= jj)
    strict = same_chunk & (ii > jj)
    first = lax.broadcasted_iota(jnp.int32, (grp, hd), 0) < c_len
    small = small_ref[...]
    g3 = lambda a: a.reshape(n_grp, grp, hd)

    top = GDN_CONV * 2
    row_top = lax.broadcasted_iota(jnp.int32, (top, hd), 0)

    def conv_silu(x_ref, w_ref, cols):
        w = w_ref[:, cols]
        x_top = x_ref[:top, cols]
        acc_top = x_top * w[GDN_CONV - 1:GDN_CONV, :]
        acc = x_ref[top:, cols] * w[GDN_CONV - 1:GDN_CONV, :]
        for sh in range(1, GDN_CONV):
            tap = w[GDN_CONV - 1 - sh:GDN_CONV - sh, :]
            acc_top = acc_top + jnp.where(row_top >= sh, pltpu.roll(x_top, sh, axis=0), 0.0) * tap
            acc = acc + x_ref[pl.ds(top - sh, s_len - top), cols] * tap
        return _silu(jnp.concatenate([acc_top, acc], axis=0))

    def l2n(x):
        return x * lax.rsqrt(jnp.sum(x * x, axis=-1, keepdims=True) + EPS)

    for j in range(GDN_HPS):
        h = hp * GDN_HPS + j
        cols = slice(j * hd, (j + 1) * hd)
        q = l2n(conv_silu(xq_ref, wq_ref, cols)) * (GDN_DK ** -0.5)
        k = l2n(conv_silu(xk_ref, wk_ref, cols))
        v = conv_silu(xv_ref, wv_ref, cols)

        pick = lambda ln: jnp.broadcast_to(
            jnp.sum(jnp.where(lane == ln, small, 0.0), axis=-1, keepdims=True), (s_len, LANES))
        beta = pick(GDN_HEADS + h)
        g = pick(h)
        sh = 1
        while sh < c_len:
            g = g + jnp.where(pos >= sh, pltpu.roll(g, sh, axis=0), 0.0)
            sh *= 2
        gc = g
        gc3 = gc.reshape(n_chunks, c_len, LANES)
        gl3 = gc3[:, c_len - 1:c_len, :]
        egl_s[j] = jnp.exp(gl3)
        gl = jnp.broadcast_to(gl3, gc3.shape).reshape(s_len, LANES)
        eg = jnp.exp(gc)
        kb = k * beta
        vb = g3(v * beta)
        kbg = g3(kb * eg)

        k3 = g3(k).astype(BF16)
        kk = _bmm_nt(g3(kb).astype(BF16), k3)
        qk = _bmm_nt(g3(q).astype(BF16), k3)
        gcg = g3(gc)
        diff = gcg - jnp.swapaxes(gcg, 1, 2)
        decay = jnp.where(tril, jnp.exp(jnp.where(tril, diff, 0.0)), 0.0)
        e_b = _unit_lower_inverse_minus_eye(jnp.where(strict, kk * decay, 0.0), ii, jj).astype(BF16)
        u = vb + _bmm(e_b, vb.astype(BF16))
        w = kbg + _bmm(e_b, kbg.astype(BF16))
        qkm = jnp.where(tril, qk * decay, 0.0).astype(BF16)
        r = _bmm(qkm, jnp.concatenate([w, u], axis=-1).astype(BF16))
        qt_s[j] = (g3(q * eg) - r[..., :hd]).astype(BF16)
        o0_s[j] = r[..., hd:]
        kd_t = jnp.swapaxes(g3(k * jnp.exp(gl - gc)), 1, 2).astype(BF16)
        rhs = jnp.concatenate([jnp.where(first, w, 0.0), jnp.where(first, 0.0, w),
                               jnp.where(first, u, 0.0), jnp.where(first, 0.0, u)], axis=-1)
        mn = _bmm(kd_t, rhs.astype(BF16))
        mp_s[j] = (-mn[..., :2 * hd]).astype(BF16)
        n_s[j] = mn[..., 2 * hd:]

    def scan(n, states):
        out = []
        for j in range(GDN_HPS):
            st = states[j]
            for half in range(grp // c_len):
                c = n * (grp // c_len) + half
                lanes = slice(half * hd, (half + 1) * hd)
                sb = st.astype(BF16)
                sall_s[j, c] = sb
                st = egl_s[j, c] * st + _dot(mp_s[j, n, :, lanes], sb) + n_s[j, n, :, lanes]
            out.append(st)
        return tuple(out)

    lax.fori_loop(0, n_grp, scan, tuple(jnp.zeros((GDN_DK, GDN_DV), F32) for _ in range(GDN_HPS)))

    for j in range(GDN_HPS):
        cols = slice(j * hd, (j + 1) * hd)
        qt = qt_s[j].reshape(n_chunks, c_len, hd)
        o = _bmm(qt, sall_s[j]) + o0_s[j].reshape(n_chunks, c_len, hd)
        o_ref[:, cols] = _rms(o.reshape(s_len, hd), onorm_ref[...]) * _silu(z_ref[:, cols])


def _gdn(qkv, z, small, conv_w, onorm):
    b, s, _ = qkv.shape
    wid = GDN_HPS * GDN_DK
    n_hp = GDN_HEADS // GDN_HPS
    n_grp, n_chunks = s // GDN_GROUP, s // GDN_CHUNK
    col = lambda off: pl.BlockSpec((None, s, wid), lambda bi, h: (bi, 0, off + h))
    wcol = lambda off: pl.BlockSpec((GDN_CONV, wid), lambda bi, h: (0, off + h))
    vec = pl.BlockSpec((1, LANES), lambda bi, h: (0, 0))
    return pl.pallas_call(
        _gdn_kernel,
        grid=(b, n_hp),
        in_specs=[col(0), col(n_hp), col(2 * n_hp),
                  wcol(0), wcol(n_hp), wcol(2 * n_hp),
                  pl.BlockSpec((None, s, LANES), lambda bi, h: (bi, 0, 0)),
                  col(0), vec],
        out_specs=col(0),
        out_shape=jax.ShapeDtypeStruct((b, s, GDN_HEADS * GDN_DV), F32),
        scratch_shapes=[pltpu.VMEM((GDN_HPS, n_grp, GDN_DK, 2 * GDN_DV), BF16),
                        pltpu.VMEM((GDN_HPS, n_grp, GDN_DK, 2 * GDN_DV), F32),
                        pltpu.VMEM((GDN_HPS, n_grp, GDN_GROUP, GDN_DK), BF16),
                        pltpu.VMEM((GDN_HPS, n_grp, GDN_GROUP, GDN_DV), F32),
                        pltpu.VMEM((GDN_HPS, n_chunks, 1, LANES), F32),
                        pltpu.VMEM((GDN_HPS, n_chunks, GDN_DK, GDN_DV), BF16)],
        compiler_params=pltpu.CompilerParams(dimension_semantics=("arbitrary", "arbitrary"),
                                             vmem_limit_bytes=VMEM_LIMIT),
        name="gdn",
    )(qkv, qkv, qkv, conv_w, conv_w, conv_w, small, z, onorm)


def _compress_kernel(t_ref, pet_ref, peb_ref, wt_ref, wb_ref, w2_ref, o_ref):
    t = t_ref[...]
    a = _dot((t + pet_ref[...]).astype(BF16), wt_ref[...])
    bm = _dot((t + peb_ref[...]).astype(BF16), wb_ref[...])
    n_rows = t.shape[0]
    hid = a + pltpu.roll(bm, n_rows - 1, axis=0)
    out = _dot(_silu(hid).astype(BF16), w2_ref[...])
    for j in range(2 * NSA_KV):
        blk = out[:, j * LANES:(j + 1) * LANES]
        o_ref[j] = (blk if j < NSA_KV else blk.T).astype(o_ref.dtype)


def _compress(tm, pe_top, pe_bot, w_top, w_bot, w2):
    b, n_rows, width = tm.shape
    nh = 2 * NSA_KV * CMP_HIDDEN
    return pl.pallas_call(
        _compress_kernel,
        grid=(b,),
        in_specs=[pl.BlockSpec((None, n_rows, width), lambda bi: (bi, 0, 0)),
                  _const_spec((1, width)), _const_spec((1, width)),
                  _const_spec((width, nh)), _const_spec((width, nh)),
                  _const_spec((nh, 2 * NSA_KV * LANES))],
        out_specs=pl.BlockSpec((None, 2 * NSA_KV, n_rows, LANES), lambda bi: (bi, 0, 0, 0)),
        out_shape=jax.ShapeDtypeStruct((b, 2 * NSA_KV, n_rows, LANES), BF16),
        compiler_params=pltpu.CompilerParams(dimension_semantics=("arbitrary",),
                                             vmem_limit_bytes=VMEM_LIMIT),
        name="compress",
    )(tm, pe_top, pe_bot, w_top, w_bot, w2)


def _nsa_select_kernel(qp_ref, kc_ref, vct_ref, gt_ref, ocmp_ref, bias_ref):
    grp = pl.program_id(1)
    ts, hpg, dh = qp_ref.shape[2], NSA_HPG, NSA_DH
    q0 = pl.program_id(2) * ts
    n_cmp = kc_ref.shape[0]
    n_blk = n_cmp * CMP_STRIDE // SEL_BLOCK
    per_head = lambda a: jnp.concatenate([a] * hpg, axis=1)

    t_c = q0 + lax.broadcasted_iota(jnp.int32, (n_cmp, ts), 1)
    n_c = lax.broadcasted_iota(jnp.int32, (n_cmp, ts), 0)
    vis = per_head(jnp.where(n_c * CMP_STRIDE + (CMP_BLOCK - 1) <= t_c, 1.0, 0.0))
    pad = jnp.zeros((LANES - dh, ts), BF16)
    q_t = jnp.concatenate([jnp.concatenate([qp_ref[hh], pad], axis=0) for hh in range(hpg)], axis=1)
    s_cmp = jnp.where(vis > 0.5, _dot(kc_ref[...], q_t), NEG)
    p_cmp = jnp.exp(s_cmp - jnp.max(s_cmp, axis=0, keepdims=True)) * vis
    p_cmp = p_cmp * (1.0 / jnp.maximum(jnp.sum(p_cmp, axis=0, keepdims=True), 1e-30))
    o_cmp = _dot(vct_ref[...], p_cmp.astype(BF16))
    ocmp_ref[...] = jnp.concatenate(
        [gt_ref[pl.ds(GATE_LANE0 + (grp * hpg + hh) * 3, 1), :] * o_cmp[:dh, hh * ts:(hh + 1) * ts]
         for hh in range(hpg)], axis=0)

    p_sum = p_cmp[:, :ts]
    for hh in range(1, hpg):
        p_sum = p_sum + p_cmp[:, hh * ts:(hh + 1) * ts]
    oj = lax.broadcasted_iota(jnp.int32, (n_blk, n_cmp), 0) * SEL_BLOCK
    on = lax.broadcasted_iota(jnp.int32, (n_blk, n_cmp), 1) * CMP_STRIDE
    overlap_t = jnp.where(on < oj + SEL_BLOCK, jnp.where(on + CMP_BLOCK > oj, 1.0, 0.0), 0.0)
    imp_t = jnp.dot(overlap_t, p_sum, precision=lax.Precision.HIGHEST,
                    preferred_element_type=F32)
    blk = lax.broadcasted_iota(jnp.int32, (n_blk, ts), 0)
    ahead = (q0 + lax.broadcasted_iota(jnp.int32, (n_blk, ts), 1)) // SEL_BLOCK - blk
    score = jnp.where(ahead < 0, NEG, jnp.where(blk == 0, BIG, jnp.where(ahead <= 1, BIG, imp_t)))
    rank = jnp.zeros((n_blk, ts), F32)
    for i in range(n_blk):
        ci = jnp.broadcast_to(score[i:i + 1, :], (n_blk, ts))
        tie = jnp.where(blk > i, 1.0, 0.0)
        rank = rank + jnp.where(ci > score, 1.0, jnp.where(ci == score, tie, 0.0))
    bias_ref[...] = jnp.where(rank >= SEL_TOPK, NEG, 0.0).astype(bias_ref.dtype)


def _nsa_select(qp_t, kcvc, gates_t):
    b, _, _, s = qp_t.shape
    ts = NSA_TS
    n_cmp = kcvc.shape[2]
    cspec = lambda off: pl.BlockSpec((None, None, n_cmp, LANES), lambda bi, g, i: (bi, off + g, 0, 0))
    return pl.pallas_call(
        _nsa_select_kernel,
        grid=(b, NSA_KV, s // ts),
        in_specs=[pl.BlockSpec((None, NSA_HPG, NSA_DH, ts), lambda bi, g, i: (bi, g, 0, i)),
                  cspec(0), cspec(NSA_KV),
                  pl.BlockSpec((None, NSA_GATE_ROWS, ts), lambda bi, g, i: (bi, 0, i))],
        out_specs=(pl.BlockSpec((None, None, NSA_HPG * NSA_DH, ts), lambda bi, g, i: (bi, g, 0, i)),
                   pl.BlockSpec((None, None, s // SEL_BLOCK, ts), lambda bi, g, i: (bi, g, 0, i))),
        out_shape=(jax.ShapeDtypeStruct((b, NSA_KV, NSA_HPG * NSA_DH, s), F32),
                   jax.ShapeDtypeStruct((b, NSA_KV, s // SEL_BLOCK, s), BF16)),
        compiler_params=pltpu.CompilerParams(
            dimension_semantics=("arbitrary", "arbitrary", "arbitrary"),
            vmem_limit_bytes=VMEM_LIMIT),
        name="nsa_select",
    )(qp_t, kcvc, kcvc, gates_t)


def _nsa_kernel(qt_ref, bias_ref, ks_ref, vst_ref, kw_ref, vwt_ref, gt_ref, ocmp_ref, o_ref,
                *score_bufs):
    grp = pl.program_id(1)
    q0 = pl.program_id(2) * NSA_TQ
    tq, hpg, dh = NSA_TQ, NSA_HPG, NSA_DH
    bias = bias_ref[...]
    pad = jnp.zeros((LANES - dh - bias.shape[0], tq), BF16)
    q_aug = jnp.concatenate(
        [jnp.concatenate([qt_ref[hh], bias, pad], axis=0) for hh in range(hpg)], axis=1)
    per_head = lambda a: jnp.concatenate([a] * hpg, axis=1)

    def attend(*streams):
        tasks = [(i, chunk) for i, chunks in enumerate(streams) for chunk in chunks]

        def scores(n):
            k, _, mask = tasks[n][1]
            s_t = _dot(k, q_aug)
            score_bufs[n % 2][:k.shape[0], :] = s_t if mask is None else s_t + mask

        state = [(None, None)] * len(streams)
        scores(0)
        for n, (i, (k, v_t, _)) in enumerate(tasks):
            if n + 1 < len(tasks):
                scores(n + 1)
            s_t = score_bufs[n % 2][:k.shape[0], :]
            m, acc = state[i]
            m_new = jnp.max(s_t, axis=0, keepdims=True)
            if m is not None:
                m_new = jnp.maximum(m, m_new)
            part = _dot(v_t, jnp.exp2(s_t - m_new).astype(BF16))
            state[i] = (m_new, part if m is None else acc * jnp.exp2(m - m_new) + part)
        return [acc[:dh, :] * (1.0 / jnp.maximum(acc[dh:dh + 1, :], 1e-30)) for _, acc in state]

    kt = SLC_KT
    last = (q0 + tq - 1) // kt
    w0 = pl.multiple_of(jnp.maximum(q0 - WINDOW, 0), tq)
    row_k = lax.broadcasted_iota(jnp.int32, (kt, tq), 0)
    t_q = q0 + lax.broadcasted_iota(jnp.int32, (kt, tq), 1)

    def step(n_chunks):
        slc = []
        for c in range(n_chunks):
            rows = slice(c * kt, (c + 1) * kt)
            causal = per_head(jnp.where(c * kt + row_k <= t_q, 0.0, NEG)) if c == n_chunks - 1 else None
            slc.append((ks_ref[rows, :], vst_ref[:, rows], causal))
        win = []
        for off, n in ((0, WINDOW + tq - kt), (WINDOW + tq - kt, kt)):
            back = (q0 + lax.broadcasted_iota(jnp.int32, (n, tq), 1)
                    - (w0 + off + lax.broadcasted_iota(jnp.int32, (n, tq), 0)))
            band = per_head(jnp.where(back < 0, NEG, jnp.where(back < WINDOW, 0.0, NEG)))
            keys = pl.ds(pl.multiple_of(w0 + off, tq), n)
            win.append((kw_ref[keys, :], vwt_ref[:, keys], band))
        o_win, o_slc = attend(win, slc)
        mixed = []
        for hh in range(hpg):
            base = GATE_LANE0 + (grp * hpg + hh) * 3
            head = slice(hh * tq, (hh + 1) * tq)
            mixed.append(gt_ref[pl.ds(base + 1, 1), :] * o_slc[:, head]
                         + gt_ref[pl.ds(base + 2, 1), :] * o_win[:, head])
        o_ref[...] = (ocmp_ref[...] + jnp.concatenate(mixed, axis=0)).T.astype(o_ref.dtype)

    for n_chunks in range(1, ks_ref.shape[0] // kt + 1):
        pl.when(last == n_chunks - 1)(functools.partial(step, n_chunks))


def _nsa_attention(qt, sel_bias, ks, vst, kw, vwt, gates_t, o_cmp):
    b, _, _, s = qt.shape
    tq = NSA_TQ
    kspec = pl.BlockSpec((None, None, s, LANES), lambda bi, g, i: (bi, g, 0, 0))
    vspec = pl.BlockSpec((None, None, NSA_VT_ROWS, s), lambda bi, g, i: (bi, g, 0, 0))
    tok = pl.BlockSpec((None, tq, NSA_HPG * NSA_DH), lambda bi, g, i: (bi, i, g))
    return pl.pallas_call(
        _nsa_kernel,
        grid=(b, NSA_KV, s // tq),
        in_specs=[pl.BlockSpec((None, NSA_HPG, NSA_DH, tq), lambda bi, g, i: (bi, g, 0, i)),
                  pl.BlockSpec((None, None, s // SEL_BLOCK, tq), lambda bi, g, i: (bi, g, 0, i)),
                  kspec, vspec, kspec, vspec,
                  pl.BlockSpec((None, NSA_GATE_ROWS, tq), lambda bi, g, i: (bi, 0, i)),
                  pl.BlockSpec((None, None, NSA_HPG * NSA_DH, tq), lambda bi, g, i: (bi, g, 0, i))],
        out_specs=tok,
        out_shape=jax.ShapeDtypeStruct((b, s, W_NSA_Q), BF16),
        scratch_shapes=[pltpu.VMEM((WINDOW + tq - SLC_KT, NSA_HPG * tq), F32)] * 2,
        compiler_params=pltpu.CompilerParams(
            dimension_semantics=("arbitrary", "arbitrary", "arbitrary"),
            vmem_limit_bytes=VMEM_LIMIT),
        name="nsa_attn",
    )(qt, sel_bias, ks, vst, kw, vwt, gates_t, o_cmp)


def _out_ffn2_kernel(x_ref, og_ref, on_ref, wo_ref, nw_ref, wg_ref, wu_ref, wd_ref, fw_ref, o_ref):
    half = W_GDN_Z
    x = (x_ref[...] + _dot(og_ref[...].astype(BF16), wo_ref[:half, :])
         + _dot(on_ref[...], wo_ref[half:, :]))
    y = _swiglu_half_step(x, nw_ref[...], wg_ref, wu_ref, wd_ref)
    o_ref[...] = _rms(y, fw_ref[...])


def _out_ffn2(x2d, o_gdn, o_nsa, w_out, nw, wg, wu, wd, fw):
    t = x2d.shape[0]
    row = lambda w: pl.BlockSpec((TOKEN_TILE, w), lambda i: (i, 0))
    return pl.pallas_call(
        _out_ffn2_kernel,
        grid=(t // TOKEN_TILE,),
        in_specs=[row(D_MODEL), row(W_GDN_Z), row(W_NSA_Q), _const_spec((D_MODEL, D_MODEL)),
                  _const_spec((1, D_MODEL)), _const_spec((D_MODEL, D_FF)),
                  _const_spec((D_MODEL, D_FF)), _const_spec((D_FF, D_MODEL)),
                  _const_spec((1, D_MODEL))],
        out_specs=row(D_MODEL),
        out_shape=jax.ShapeDtypeStruct((t, D_MODEL), F32),
        compiler_params=pltpu.CompilerParams(dimension_semantics=("arbitrary",),
                                             vmem_limit_bytes=VMEM_LIMIT),
        name="out_ffn2",
    )(x2d, o_gdn, o_nsa, w_out, nw, wg, wu, wd, fw)


def _rope_tables(s_len):
    half = ROPE_DIM // 2
    inv = ROPE_THETA ** (-jnp.arange(half, dtype=F32) / half)
    ang = jnp.arange(s_len, dtype=F32)[:, None] * inv[None, :]
    cos, sin = jnp.cos(ang), jnp.sin(ang)
    pad = jnp.zeros((s_len, NSA_DH - ROPE_DIM), F32)
    zero = jnp.zeros_like(sin)
    cos_h = jnp.concatenate([cos, cos, pad + 1.0], axis=1)
    sa_h = jnp.concatenate([-sin, zero, pad], axis=1)
    sb_h = jnp.concatenate([zero, sin, pad], axis=1)
    rep = lambda a: jnp.tile(a, (1, LANES // NSA_DH))
    return rep(cos_h), rep(sa_h), rep(sb_h)


def _split_w_in(w_in):
    cuts = np.cumsum([W_GDN_QKV, W_GDN_Z, GDN_HEADS, GDN_HEADS, W_NSA_Q,
                      W_NSA_KV, W_NSA_KV, W_NSA_KV]).tolist()
    qkv, z, a, bb, nq, kvc, kvs, kvw, gate = jnp.split(w_in, cuts, axis=1)
    w_main = jnp.concatenate([qkv, z, nq, kvc, kvs, kvw], axis=1).astype(BF16)
    small = jnp.concatenate([a, bb, gate], axis=1)
    w_small = jnp.pad(small, ((0, 0), (0, LANES - small.shape[1]))).astype(BF16)
    return w_main, w_small


def _compress_weights(pe_k, w1_k, w2_k, pe_v, w1_v, w2_v):
    half = CMP_BLOCK // 2
    eye = jnp.eye(2 * NSA_KV, dtype=F32)

    def first_layer(lo):
        w1 = jnp.stack([w1_k, w1_v]).reshape(2, CMP_BLOCK, NSA_DH, CMP_HIDDEN)[:, lo:lo + half]
        w1 = jnp.repeat(w1, NSA_KV, axis=0)
        w = jnp.einsum('crdj,ce->rcdej', w1, eye)
        return w.reshape(half * 2 * NSA_KV * NSA_DH, 2 * NSA_KV * CMP_HIDDEN).astype(BF16)

    def pe_row(lo):
        pe = jnp.stack([pe_k, pe_v])[:, lo:lo + half]
        pe = jnp.repeat(pe, NSA_KV, axis=0)
        return pe.transpose(1, 0, 2).reshape(1, half * 2 * NSA_KV * NSA_DH)

    w2 = jnp.repeat(jnp.stack([w2_k, w2_v]), NSA_KV, axis=0)
    w2 = jnp.pad(w2, ((0, 0), (0, 0), (0, LANES - NSA_DH)))
    w2 = jnp.einsum('cjd,ce->cjed', w2, eye).reshape(2 * NSA_KV * CMP_HIDDEN, 2 * NSA_KV * LANES)
    return pe_row(0), pe_row(half), first_layer(0), first_layer(half), w2.astype(BF16)


def kernel(x, ffn1_norm, ffn1_w_gate, ffn1_w_up, ffn1_w_down, mix_norm, w_in, gdn_conv_w, gdn_a_log,
           gdn_dt_bias, gdn_out_norm, cmp_pe_k, cmp_w1_k, cmp_w2_k, cmp_pe_v, cmp_w1_v, cmp_w2_v,
           w_out, ffn2_norm, ffn2_w_gate, ffn2_w_up, ffn2_w_down, final_norm):
    b, s, d = x.shape
    assert d == D_MODEL and s % TOKEN_TILE == 0 and ffn1_norm.shape[0] == 1
    vec = lambda a: a.reshape(1, -1)
    pad_lanes = lambda a: jnp.pad(a.reshape(1, -1), ((0, 0), (0, LANES - a.size)))
    bf = lambda a: a.astype(BF16)

    x1 = _ffn1(x.reshape(b * s, d), vec(ffn1_norm[0]), bf(ffn1_w_gate[0]), bf(ffn1_w_up[0]),
               bf(ffn1_w_down[0]))

    w_main, w_small = _split_w_in(w_in[0])
    cos_t, sin_a, sin_b = _rope_tables(s)
    qkv, z, small, gates_t, qp, qt, kvc, ks, vst, kw, vwt = _in_proj(
        x1.reshape(b, s, d), vec(mix_norm[0]), w_main, w_small, cos_t, sin_a, sin_b,
        pad_lanes(gdn_a_log[0]), pad_lanes(gdn_dt_bias[0]))

    o_gdn = _gdn(qkv, z, small, gdn_conv_w[0], vec(gdn_out_norm[0]))

    pe_top, pe_bot, w_top, w_bot, w2 = _compress_weights(
        cmp_pe_k[0], cmp_w1_k[0], cmp_w2_k[0], cmp_pe_v[0], cmp_w1_v[0], cmp_w2_v[0])
    tokens16 = kvc.reshape(b, s // CMP_STRIDE, CMP_STRIDE * W_NSA_KV)
    kcvc = _compress(tokens16, pe_top, pe_bot, w_top, w_bot, w2)

    o_cmp, sel_bias = _nsa_select(qp, kcvc, gates_t)
    o_nsa = _nsa_attention(qt, sel_bias, ks, vst, kw, vwt, gates_t, o_cmp).reshape(b * s, W_NSA_Q)

    out = _out_ffn2(x1, o_gdn.reshape(b * s, W_GDN_Z), o_nsa, bf(w_out[0]), vec(ffn2_norm[0]),
                    bf(ffn2_w_gate[0]), bf(ffn2_w_up[0]), bf(ffn2_w_down[0]), vec(final_norm))
    return out.reshape(b, s, d)
```

```python
import functools

import numpy as np
import jax
import jax.numpy as jnp
from jax import lax
from jax.experimental import pallas as pl
from jax.experimental.pallas import tpu as pltpu

F32 = jnp.float32
BF16 = jnp.bfloat16

D_MODEL = 1024
D_FF = 2816
GDN_HEADS = 4
GDN_DK = 128
GDN_DV = 128
GDN_CONV = 4
GDN_CHUNK = 64
NSA_HEADS = 8
NSA_KV = 2
NSA_HPG = NSA_HEADS // NSA_KV
NSA_DH = 64
CMP_BLOCK = 32
CMP_STRIDE = 16
CMP_HIDDEN = 128
SEL_BLOCK = 64
SEL_TOPK = 16
WINDOW = 512
ROPE_THETA = 500000.0
ROPE_DIM = NSA_DH // 4
EPS = 1e-6
NEG = -1e30
LOG2E = 1.4426950408889634
BIG = 1e30

W_GDN_QKV = GDN_HEADS * (2 * GDN_DK + GDN_DV)
W_GDN_Z = GDN_HEADS * GDN_DV
W_NSA_Q = NSA_HEADS * NSA_DH
W_NSA_KV = 2 * NSA_KV * NSA_DH
W_NSA_GATE = 3 * NSA_HEADS
W_MAIN = W_GDN_QKV + W_GDN_Z + W_NSA_Q + 3 * W_NSA_KV
LANES = 128
GATE_LANE0 = 2 * GDN_HEADS

TOKEN_TILE = 512
MXU_TILE = 256
FF_SPLITS = (0, 6 * MXU_TILE, D_FF)
GDN_HPS = 2
GDN_GROUP = 2 * GDN_CHUNK
NSA_TQ = 128
NSA_TS = 512
SLC_KT = 256
NSA_VT_ROWS = 80
NSA_GATE_ROWS = 32
VMEM_LIMIT = 56 * 1024 * 1024


def _dot(a, b):
    return jnp.dot(a, b, preferred_element_type=F32)


def _dot_nt(a, b):
    return lax.dot_general(a, b, (((1,), (1,)), ((), ())), preferred_element_type=F32)


def _dot_tn(a, b):
    return lax.dot_general(a, b, (((0,), (0,)), ((), ())), preferred_element_type=F32)


def _rms(x, w):
    return x * lax.rsqrt(jnp.mean(x * x, axis=-1, keepdims=True) + EPS) * w


def _sigmoid(x):
    return 0.5 * jnp.tanh(0.5 * x) + 0.5


def _silu(x):
    return x * _sigmoid(x)


def _swiglu_half_step(x, nw, wg_ref, wu_ref, wd_ref):
    hb = _rms(x, nw).astype(BF16)
    acc = jnp.zeros_like(x)
    for lo, hi in zip(FF_SPLITS[:-1], FF_SPLITS[1:]):
        sl = slice(lo, hi)
        g = _dot(hb, wg_ref[:, sl])
        u = _dot(hb, wu_ref[:, sl])
        acc = acc + _dot((_silu(g) * u).astype(BF16), wd_ref[sl, :])
    return x + 0.5 * acc


def _ffn_kernel(x_ref, nw_ref, wg_ref, wu_ref, wd_ref, o_ref):
    o_ref[...] = _swiglu_half_step(x_ref[...], nw_ref[...], wg_ref, wu_ref, wd_ref)


def _const_spec(shape):
    nd = len(shape)
    return pl.BlockSpec(shape, lambda *_: (0,) * nd, pipeline_mode=pl.Buffered(1))


def _ffn1(x2d, nw, wg, wu, wd):
    t = x2d.shape[0]
    row = pl.BlockSpec((TOKEN_TILE, D_MODEL), lambda i: (i, 0))
    return pl.pallas_call(
        _ffn_kernel,
        grid=(t // TOKEN_TILE,),
        in_specs=[row, _const_spec((1, D_MODEL)), _const_spec((D_MODEL, D_FF)),
                  _const_spec((D_MODEL, D_FF)), _const_spec((D_FF, D_MODEL))],
        out_specs=row,
        out_shape=jax.ShapeDtypeStruct((t, D_MODEL), F32),
        compiler_params=pltpu.CompilerParams(dimension_semantics=("arbitrary",),
                                             vmem_limit_bytes=VMEM_LIMIT),
        name="ffn1",
    )(x2d, nw, wg, wu, wd)


def _rope(x, cos_t, sin_a, sin_b):
    return (x * cos_t + pltpu.roll(x, LANES - ROPE_DIM // 2, axis=1) * sin_a
            + pltpu.roll(x, ROPE_DIM // 2, axis=1) * sin_b)


def _inproj_kernel(x_ref, nw_ref, wm_ref, ws_ref, cos_ref, sa_ref, sb_ref, alog_ref, dt_ref,
                   qkv_ref, z_ref, small_ref, gt_ref, qp_ref, qt_ref, kvc_ref,
                   ks_ref, vst_ref, kw_ref, vwt_ref, kvc_s):
    hb = _rms(x_ref[...], nw_ref[...]).astype(BF16)
    y = _dot(hb, wm_ref[...])
    tm = x_ref.shape[0]
    lane = lax.broadcasted_iota(jnp.int32, (tm, LANES), 1)
    sm = _dot(hb, ws_ref[...])
    xg = sm + dt_ref[...]
    softplus = jnp.maximum(xg, 0.0) + jnp.log1p(jnp.exp(-jnp.abs(xg)))
    small = jnp.where(lane < GDN_HEADS, -jnp.exp(alog_ref[...]) * softplus, _sigmoid(sm))
    small_ref[...] = small
    gt_ref[...] = small.T[:gt_ref.shape[0], :]
    qkv_ref[...] = y[:, :W_GDN_QKV]
    o = W_GDN_QKV
    z_ref[...] = y[:, o:o + W_GDN_Z]
    o += W_GDN_Z
    cos_t, sin_a, sin_b = cos_ref[...], sa_ref[...], sb_ref[...]
    scale = NSA_DH ** -0.5
    pos = pl.program_id(1) * tm + lax.broadcasted_iota(jnp.int32, (tm, LANES), 0)
    low_half = lane < NSA_DH
    block_onehot = jnp.where(lane - NSA_DH == pos // SEL_BLOCK, 1.0, 0.0)

    def halves(slab):
        return slab, pltpu.roll(slab, NSA_DH, axis=1)

    for s in range(W_NSA_Q // LANES):
        slab = y[:, o + s * LANES:o + (s + 1) * LANES]
        plain_t = (slab * scale).T
        rot_t = (_rope(slab, cos_t, sin_a, sin_b) * (scale * LOG2E)).T
        for j in range(2):
            qp_ref[2 * s + j] = plain_t[j * NSA_DH:(j + 1) * NSA_DH, :].astype(BF16)
            qt_ref[2 * s + j] = rot_t[j * NSA_DH:(j + 1) * NSA_DH, :].astype(BF16)
    o += W_NSA_Q
    for half in range(W_NSA_KV // LANES):
        kvc_s[half] = y[:, o + half * LANES:o + (half + 1) * LANES]
    for r in range(CMP_STRIDE):
        for half in range(W_NSA_KV // LANES):
            dst = r * W_NSA_KV + half * LANES
            kvc_ref[:, dst:dst + LANES] = kvc_s[half, pl.ds(r, tm // CMP_STRIDE, stride=CMP_STRIDE), :]
    o += W_NSA_KV
    n_tail = vst_ref.shape[1] - NSA_DH
    tail = jnp.where(lax.broadcasted_iota(jnp.int32, (n_tail, tm), 0) == 0, 1.0, 0.0)
    for k_ref, vt_ref, k_pad in ((ks_ref, vst_ref, block_onehot), (kw_ref, vwt_ref, 0.0)):
        kr = halves(_rope(y[:, o:o + LANES], cos_t, sin_a, sin_b))
        v_t = y[:, o + LANES:o + 2 * LANES].T
        for j in range(NSA_KV):
            k_ref[j] = jnp.where(low_half, kr[j], k_pad).astype(BF16)
            vt_ref[j] = jnp.concatenate([v_t[j * NSA_DH:(j + 1) * NSA_DH, :], tail], axis=0).astype(BF16)
        o += W_NSA_KV


def _in_proj(x3d, nw, w_main, w_small, cos_t, sin_a, sin_b, alog_p, dt_p):
    b, s, _ = x3d.shape
    tm = TOKEN_TILE
    row = lambda w: pl.BlockSpec((None, tm, w), lambda bi, i: (bi, i, 0))
    heads = lambda n, w: pl.BlockSpec((None, n, tm, w), lambda bi, i: (bi, 0, i, 0))
    heads_t = lambda n, r: pl.BlockSpec((None, n, r, tm), lambda bi, i: (bi, 0, 0, i))
    tab = pl.BlockSpec((tm, LANES), lambda bi, i: (i, 0))
    bf_heads = lambda n, w: jax.ShapeDtypeStruct((b, n, s, w), BF16)
    out_shape = (
        jax.ShapeDtypeStruct((b, s, W_GDN_QKV), F32),
        jax.ShapeDtypeStruct((b, s, W_GDN_Z), F32),
        jax.ShapeDtypeStruct((b, s, LANES), F32),
        jax.ShapeDtypeStruct((b, NSA_GATE_ROWS, s), F32),
        jax.ShapeDtypeStruct((b, NSA_HEADS, NSA_DH, s), BF16),
        jax.ShapeDtypeStruct((b, NSA_HEADS, NSA_DH, s), BF16),
        jax.ShapeDtypeStruct((b, s // CMP_STRIDE, CMP_STRIDE * W_NSA_KV), F32),
        bf_heads(NSA_KV, LANES),
        jax.ShapeDtypeStruct((b, NSA_KV, NSA_VT_ROWS, s), BF16),
        bf_heads(NSA_KV, LANES),
        jax.ShapeDtypeStruct((b, NSA_KV, NSA_VT_ROWS, s), BF16),
    )
    return pl.pallas_call(
        _inproj_kernel,
        grid=(b, s // tm),
        in_specs=[row(D_MODEL), _const_spec((1, D_MODEL)), _const_spec((D_MODEL, W_MAIN)),
                  _const_spec((D_MODEL, LANES)), tab, tab, tab,
                  _const_spec((1, LANES)), _const_spec((1, LANES))],
        out_specs=(row(W_GDN_QKV), row(W_GDN_Z), row(LANES),
                   pl.BlockSpec((None, NSA_GATE_ROWS, tm), lambda bi, i: (bi, 0, i)),
                   heads_t(NSA_HEADS, NSA_DH), heads_t(NSA_HEADS, NSA_DH),
                   pl.BlockSpec((None, tm // CMP_STRIDE, CMP_STRIDE * W_NSA_KV), lambda bi, i: (bi, i, 0)),
                   heads(NSA_KV, LANES), heads_t(NSA_KV, NSA_VT_ROWS),
                   heads(NSA_KV, LANES), heads_t(NSA_KV, NSA_VT_ROWS)),
        out_shape=out_shape,
        scratch_shapes=[pltpu.VMEM((W_NSA_KV // LANES, tm, LANES), F32)],
        compiler_params=pltpu.CompilerParams(dimension_semantics=("arbitrary", "arbitrary"),
                                             vmem_limit_bytes=VMEM_LIMIT),
        name="in_proj",
    )(x3d, nw, w_main, w_small, cos_t, sin_a, sin_b, alog_p, dt_p)


def _bmm(a, b):
    return jnp.einsum('nij,njk->nik', a, b, preferred_element_type=F32)


def _bmm_nt(a, b):
    return jnp.einsum('nid,njd->nij', a, b, preferred_element_type=F32)


def _unit_lower_inverse_minus_eye(low, ii, jj):
    mm = lambda a, b: _bmm(a.astype(BF16), b.astype(BF16))
    same16 = (ii // 16) == (jj // 16)
    same32 = (ii // 32) == (jj // 32)
    ld = jnp.where(same16, low, 0.0)
    e = -ld
    p = mm(ld, ld)
    for level in range(3):
        e = e + p + mm(e, p)
        if level < 2:
            p = mm(p, p)
    for within in (same32 & jnp.logical_not(same16), jnp.logical_not(same32)):
        c = jnp.where(within, low, 0.0)
        y = c + mm(e, c)
        e = e - y - mm(y, e)
    return e


def _gdn_kernel(xq_ref, xk_ref, xv_ref, wq_ref, wk_ref, wv_ref, small_ref, z_ref,
                onorm_ref, o_ref,
                p_s, qn_s, e2_s, mpa_s, na_s, ea_s, qt_s, o0_s, seven_s):
    hp = pl.program_id(1)
    s_len = xq_ref.shape[0]
    c_len = GDN_CHUNK
    grp = GDN_GROUP
    n_chunks = s_len // c_len
    n_grp = s_len // grp
    hd = GDN_DK
    row = lax.broadcasted_iota(jnp.int32, (s_len, LANES), 0)
    lane = lax.broadcasted_iota(jnp.int32, (s_len, LANES), 1)
    pos = row % c_len
    ii = lax.broadcasted_iota(jnp.int32, (grp, grp), 0)
    jj = lax.broadcasted_iota(jnp.int32, (grp, grp), 1)
    same_chunk = (ii // c_len) == (jj // c_len)
    tril = same_chunk & (ii >= jj)
    strict = same_chunk & (ii > jj)
    first = lax.broadcasted_iota(jnp.int32, (grp, hd), 0) < c_len
    small = small_ref[...]
    g3 = lambda a: a.reshape(n_grp, grp, hd)

    top = GDN_CONV * 2
    row_top = lax.broadcasted_iota(jnp.int32, (top, hd), 0)

    def conv_silu(x_ref, w_ref, cols):
        w = w_ref[:, cols]
        x_top = x_ref[:top, cols]
        acc_top = x_top * w[GDN_CONV - 1:GDN_CONV, :]
        acc = x_ref[top:, cols] * w[GDN_CONV - 1:GDN_CONV, :]
        for sh in range(1, GDN_CONV):
            tap = w[GDN_CONV - 1 - sh:GDN_CONV - sh, :]
            acc_top = acc_top + jnp.where(row_top >= sh, pltpu.roll(x_top, sh, axis=0), 0.0) * tap
            acc = acc + x_ref[pl.ds(top - sh, s_len - top), cols] * tap
        return _silu(jnp.concatenate([acc_top, acc], axis=0))

    def l2n(x):
        return x * lax.rsqrt(jnp.sum(x * x, axis=-1, keepdims=True) + EPS)

    for j in range(GDN_HPS):
        h = hp * GDN_HPS + j
        cols = slice(j * hd, (j + 1) * hd)
        q = l2n(conv_silu(xq_ref, wq_ref, cols)) * (GDN_DK ** -0.5)
        k = l2n(conv_silu(xk_ref, wk_ref, cols))
        v = conv_silu(xv_ref, wv_ref, cols)

        pick = lambda ln: jnp.broadcast_to(
            jnp.sum(jnp.where(lane == ln, small, 0.0), axis=-1, keepdims=True), (s_len, LANES))
        beta = pick(GDN_HEADS + h)
        g = pick(h)
        sh = 1
        while sh < c_len:
            g = g + jnp.where(pos >= sh, pltpu.roll(g, sh, axis=0), 0.0)
            sh *= 2
        gc = g
        gc3 = gc.reshape(n_chunks, c_len, LANES)
        gl3 = gc3[:, c_len - 1:c_len, :]
        e_gl = jnp.exp(gl3).reshape(n_grp, grp // c_len, 1, LANES)
        dec_a, dec_b = e_gl[:, 0], e_gl[:, 1]
        gl = jnp.broadcast_to(gl3, gc3.shape).reshape(s_len, LANES)
        eg = jnp.exp(gc)
        kb = k * beta
        vb = g3(v * beta)
        kbg = g3(kb * eg)

        k3 = g3(k).astype(BF16)
        kk = _bmm_nt(g3(kb).astype(BF16), k3)
        qk = _bmm_nt(g3(q).astype(BF16), k3)
        gcg = g3(gc)
        diff = gcg - jnp.swapaxes(gcg, 1, 2)
        decay = jnp.where(tril, jnp.exp(jnp.where(tril, diff, 0.0)), 0.0)
        e_b = _unit_lower_inverse_minus_eye(jnp.where(strict, kk * decay, 0.0), ii, jj).astype(BF16)
        u = vb + _bmm(e_b, vb.astype(BF16))
        w = kbg + _bmm(e_b, kbg.astype(BF16))
        qkm = jnp.where(tril, qk * decay, 0.0).astype(BF16)
        r = _bmm(qkm, jnp.concatenate([w, u], axis=-1).astype(BF16))
        qt_s[j] = (g3(q * eg) - r[..., :hd]).astype(BF16)
        o0_s[j] = r[..., hd:]
        kd_t = jnp.swapaxes(g3(k * jnp.exp(gl - gc)), 1, 2).astype(BF16)
        rhs = jnp.concatenate([jnp.where(first, w, 0.0), jnp.where(first, 0.0, w),
                               jnp.where(first, u, 0.0), jnp.where(first, 0.0, u)], axis=-1)
        mn = _bmm(kd_t, rhs.astype(BF16))
        mp_a, mp_b = -mn[..., :hd], -mn[..., hd:2 * hd]
        n_a, n_b = mn[..., 2 * hd:3 * hd], mn[..., 3 * hd:]
        x = _bmm(mp_b.astype(BF16), jnp.concatenate([mp_a, n_a], axis=-1).astype(BF16))
        p_s[j] = (dec_b * mp_a + dec_a * mp_b + x[..., :hd]).astype(BF16)
        qn_s[j] = dec_b * n_a + x[..., hd:] + n_b
        e2_s[j] = dec_a * dec_b
        mpa_s[j] = mp_a.astype(BF16)
        na_s[j] = n_a
        ea_s[j] = dec_a

    def scan(n, states):
        out = []
        for j in range(GDN_HPS):
            sb = states[j].astype(BF16)
            seven_s[j, n] = sb
            out.append(e2_s[j, n] * states[j] + _dot(p_s[j, n], sb) + qn_s[j, n])
        return tuple(out)

    lax.fori_loop(0, n_grp, scan, tuple(jnp.zeros((GDN_DK, GDN_DV), F32) for _ in range(GDN_HPS)))

    for j in range(GDN_HPS):
        cols = slice(j * hd, (j + 1) * hd)
        s_even = seven_s[j]
        s_odd = (ea_s[j] * s_even.astype(F32) + _bmm(mpa_s[j], s_even) + na_s[j]).astype(BF16)
        qt = qt_s[j]
        o = jnp.concatenate([_bmm(qt[:, :c_len], s_even), _bmm(qt[:, c_len:], s_odd)], axis=1) + o0_s[j]
        o_ref[:, cols] = _rms(o.reshape(s_len, hd), onorm_ref[...]) * _silu(z_ref[:, cols])


def _gdn(qkv, z, small, conv_w, onorm):
    b, s, _ = qkv.shape
    wid = GDN_HPS * GDN_DK
    n_hp = GDN_HEADS // GDN_HPS
    n_grp = s // GDN_GROUP
    per_pair = lambda rows, dt: pltpu.VMEM((GDN_HPS, n_grp, rows, GDN_DV), dt)
    col = lambda off: pl.BlockSpec((None, s, wid), lambda bi, h: (bi, 0, off + h))
    wcol = lambda off: pl.BlockSpec((GDN_CONV, wid), lambda bi, h: (0, off + h))
    vec = pl.BlockSpec((1, LANES), lambda bi, h: (0, 0))
    return pl.pallas_call(
        _gdn_kernel,
        grid=(b, n_hp),
        in_specs=[col(0), col(n_hp), col(2 * n_hp),
                  wcol(0), wcol(n_hp), wcol(2 * n_hp),
                  pl.BlockSpec((None, s, LANES), lambda bi, h: (bi, 0, 0)),
                  col(0), vec],
        out_specs=col(0),
        out_shape=jax.ShapeDtypeStruct((b, s, GDN_HEADS * GDN_DV), F32),
        scratch_shapes=[per_pair(GDN_DK, BF16), per_pair(GDN_DK, F32), per_pair(1, F32),
                        per_pair(GDN_DK, BF16), per_pair(GDN_DK, F32), per_pair(1, F32),
                        per_pair(GDN_GROUP, BF16), per_pair(GDN_GROUP, F32), per_pair(GDN_DK, BF16)],
        compiler_params=pltpu.CompilerParams(dimension_semantics=("arbitrary", "arbitrary"),
                                             vmem_limit_bytes=VMEM_LIMIT),
        name="gdn",
    )(qkv, qkv, qkv, conv_w, conv_w, conv_w, small, z, onorm)


def _compress_kernel(t_ref, pet_ref, peb_ref, wt_ref, wb_ref, w2_ref, o_ref):
    t = t_ref[...]
    a = _dot((t + pet_ref[...]).astype(BF16), wt_ref[...])
    bm = _dot((t + peb_ref[...]).astype(BF16), wb_ref[...])
    n_rows = t.shape[0]
    hid = a + pltpu.roll(bm, n_rows - 1, axis=0)
    out = _dot(_silu(hid).astype(BF16), w2_ref[...])
    for j in range(2 * NSA_KV):
        blk = out[:, j * LANES:(j + 1) * LANES]
        o_ref[j] = (blk if j < NSA_KV else blk.T).astype(o_ref.dtype)


def _compress(tm, pe_top, pe_bot, w_top, w_bot, w2):
    b, n_rows, width = tm.shape
    nh = 2 * NSA_KV * CMP_HIDDEN
    return pl.pallas_call(
        _compress_kernel,
        grid=(b,),
        in_specs=[pl.BlockSpec((None, n_rows, width), lambda bi: (bi, 0, 0)),
                  _const_spec((1, width)), _const_spec((1, width)),
                  _const_spec((width, nh)), _const_spec((width, nh)),
                  _const_spec((nh, 2 * NSA_KV * LANES))],
        out_specs=pl.BlockSpec((None, 2 * NSA_KV, n_rows, LANES), lambda bi: (bi, 0, 0, 0)),
        out_shape=jax.ShapeDtypeStruct((b, 2 * NSA_KV, n_rows, LANES), BF16),
        compiler_params=pltpu.CompilerParams(dimension_semantics=("arbitrary",),
                                             vmem_limit_bytes=VMEM_LIMIT),
        name="compress",
    )(tm, pe_top, pe_bot, w_top, w_bot, w2)


def _nsa_select_kernel(qp_ref, kc_ref, vct_ref, gt_ref, ocmp_ref, bias_ref):
    grp = pl.program_id(1)
    ts, hpg, dh = qp_ref.shape[2], NSA_HPG, NSA_DH
    q0 = pl.program_id(2) * ts
    n_cmp = kc_ref.shape[0]
    n_blk = n_cmp * CMP_STRIDE // SEL_BLOCK
    per_head = lambda a: jnp.concatenate([a] * hpg, axis=1)

    t_c = q0 + lax.broadcasted_iota(jnp.int32, (n_cmp, ts), 1)
    n_c = lax.broadcasted_iota(jnp.int32, (n_cmp, ts), 0)
    vis = per_head(jnp.where(n_c * CMP_STRIDE + (CMP_BLOCK - 1) <= t_c, 1.0, 0.0))
    pad = jnp.zeros((LANES - dh, ts), BF16)
    q_t = jnp.concatenate([jnp.concatenate([qp_ref[hh], pad], axis=0) for hh in range(hpg)], axis=1)
    s_cmp = jnp.where(vis > 0.5, _dot(kc_ref[...], q_t), NEG)
    p_cmp = jnp.exp(s_cmp - jnp.max(s_cmp, axis=0, keepdims=True)) * vis
    p_cmp = p_cmp * (1.0 / jnp.maximum(jnp.sum(p_cmp, axis=0, keepdims=True), 1e-30))
    o_cmp = _dot(vct_ref[...], p_cmp.astype(BF16))
    ocmp_ref[...] = jnp.concatenate(
        [gt_ref[pl.ds(GATE_LANE0 + (grp * hpg + hh) * 3, 1), :] * o_cmp[:dh, hh * ts:(hh + 1) * ts]
         for hh in range(hpg)], axis=0)

    p_sum = p_cmp[:, :ts]
    for hh in range(1, hpg):
        p_sum = p_sum + p_cmp[:, hh * ts:(hh + 1) * ts]
    oj = lax.broadcasted_iota(jnp.int32, (n_blk, n_cmp), 0) * SEL_BLOCK
    on = lax.broadcasted_iota(jnp.int32, (n_blk, n_cmp), 1) * CMP_STRIDE
    overlap_t = jnp.where(on < oj + SEL_BLOCK, jnp.where(on + CMP_BLOCK > oj, 1.0, 0.0), 0.0)
    imp_t = jnp.dot(overlap_t, p_sum, precision=lax.Precision.HIGHEST,
                    preferred_element_type=F32)
    blk = lax.broadcasted_iota(jnp.int32, (n_blk, ts), 0)
    ahead = (q0 + lax.broadcasted_iota(jnp.int32, (n_blk, ts), 1)) // SEL_BLOCK - blk
    score = jnp.where(ahead < 0, NEG, jnp.where(blk == 0, BIG, jnp.where(ahead <= 1, BIG, imp_t)))
    rank = jnp.zeros((n_blk, ts), F32)
    for i in range(n_blk):
        ci = jnp.broadcast_to(score[i:i + 1, :], (n_blk, ts))
        tie = jnp.where(blk > i, 1.0, 0.0)
        rank = rank + jnp.where(ci > score, 1.0, jnp.where(ci == score, tie, 0.0))
    bias_ref[...] = jnp.where(rank >= SEL_TOPK, NEG, 0.0).astype(bias_ref.dtype)


def _nsa_select(qp_t, kcvc, gates_t):
    b, _, _, s = qp_t.shape
    ts = NSA_TS
    n_cmp = kcvc.shape[2]
    cspec = lambda off: pl.BlockSpec((None, None, n_cmp, LANES), lambda bi, g, i: (bi, off + g, 0, 0))
    return pl.pallas_call(
        _nsa_select_kernel,
        grid=(b, NSA_KV, s // ts),
        in_specs=[pl.BlockSpec((None, NSA_HPG, NSA_DH, ts), lambda bi, g, i: (bi, g, 0, i)),
                  cspec(0), cspec(NSA_KV),
                  pl.BlockSpec((None, NSA_GATE_ROWS, ts), lambda bi, g, i: (bi, 0, i))],
        out_specs=(pl.BlockSpec((None, None, NSA_HPG * NSA_DH, ts), lambda bi, g, i: (bi, g, 0, i)),
                   pl.BlockSpec((None, None, s // SEL_BLOCK, ts), lambda bi, g, i: (bi, g, 0, i))),
        out_shape=(jax.ShapeDtypeStruct((b, NSA_KV, NSA_HPG * NSA_DH, s), F32),
                   jax.ShapeDtypeStruct((b, NSA_KV, s // SEL_BLOCK, s), BF16)),
        compiler_params=pltpu.CompilerParams(
            dimension_semantics=("arbitrary", "arbitrary", "arbitrary"),
            vmem_limit_bytes=VMEM_LIMIT),
        name="nsa_select",
    )(qp_t, kcvc, kcvc, gates_t)


def _nsa_kernel(qt_ref, bias_ref, ks_ref, vst_ref, kw_ref, vwt_ref, gt_ref, ocmp_ref, o_ref,
                *score_bufs):
    grp = pl.program_id(1)
    q0 = pl.program_id(2) * NSA_TQ
    tq, hpg, dh = NSA_TQ, NSA_HPG, NSA_DH
    bias = bias_ref[...]
    pad = jnp.zeros((LANES - dh - bias.shape[0], tq), BF16)
    q_aug = jnp.concatenate(
        [jnp.concatenate([qt_ref[hh], bias, pad], axis=0) for hh in range(hpg)], axis=1)
    per_head = lambda a: jnp.concatenate([a] * hpg, axis=1)

    def attend(*streams):
        tasks = [(i, chunk) for i, chunks in enumerate(streams) for chunk in chunks]

        def scores(n):
            k, _, mask = tasks[n][1]
            s_t = _dot(k, q_aug)
            score_bufs[n % 2][:k.shape[0], :] = s_t if mask is None else s_t + mask

        state = [(None, None)] * len(streams)
        scores(0)
        for n, (i, (k, v_t, _)) in enumerate(tasks):
            if n + 1 < len(tasks):
                scores(n + 1)
            s_t = score_bufs[n % 2][:k.shape[0], :]
            m, acc = state[i]
            m_new = jnp.max(s_t, axis=0, keepdims=True)
            if m is not None:
                m_new = jnp.maximum(m, m_new)
            part = _dot(v_t, jnp.exp2(s_t - m_new).astype(BF16))
            state[i] = (m_new, part if m is None else acc * jnp.exp2(m - m_new) + part)
        return [acc[:dh, :] * (1.0 / jnp.maximum(acc[dh:dh + 1, :], 1e-30)) for _, acc in state]

    kt = SLC_KT
    last = (q0 + tq - 1) // kt
    w0 = pl.multiple_of(jnp.maximum(q0 - WINDOW, 0), tq)
    row_k = lax.broadcasted_iota(jnp.int32, (kt, tq), 0)
    t_q = q0 + lax.broadcasted_iota(jnp.int32, (kt, tq), 1)

    def step(n_chunks):
        slc = []
        for c in range(n_chunks):
            rows = slice(c * kt, (c + 1) * kt)
            causal = per_head(jnp.where(c * kt + row_k <= t_q, 0.0, NEG)) if c == n_chunks - 1 else None
            slc.append((ks_ref[rows, :], vst_ref[:, rows], causal))
        win = []
        for off, n in ((0, WINDOW + tq - kt), (WINDOW + tq - kt, kt)):
            back = (q0 + lax.broadcasted_iota(jnp.int32, (n, tq), 1)
                    - (w0 + off + lax.broadcasted_iota(jnp.int32, (n, tq), 0)))
            band = per_head(jnp.where(back < 0, NEG, jnp.where(back < WINDOW, 0.0, NEG)))
            keys = pl.ds(pl.multiple_of(w0 + off, tq), n)
            win.append((kw_ref[keys, :], vwt_ref[:, keys], band))
        o_win, o_slc = attend(win, slc)
        mixed = []
        for hh in range(hpg):
            base = GATE_LANE0 + (grp * hpg + hh) * 3
            head = slice(hh * tq, (hh + 1) * tq)
            mixed.append(gt_ref[pl.ds(base + 1, 1), :] * o_slc[:, head]
                         + gt_ref[pl.ds(base + 2, 1), :] * o_win[:, head])
        o_ref[...] = (ocmp_ref[...] + jnp.concatenate(mixed, axis=0)).T.astype(o_ref.dtype)

    for n_chunks in range(1, ks_ref.shape[0] // kt + 1):
        pl.when(last == n_chunks - 1)(functools.partial(step, n_chunks))


def _nsa_attention(qt, sel_bias, ks, vst, kw, vwt, gates_t, o_cmp):
    b, _, _, s = qt.shape
    tq = NSA_TQ
    kspec = pl.BlockSpec((None, None, s, LANES), lambda bi, g, i: (bi, g, 0, 0))
    vspec = pl.BlockSpec((None, None, NSA_VT_ROWS, s), lambda bi, g, i: (bi, g, 0, 0))
    tok = pl.BlockSpec((None, tq, NSA_HPG * NSA_DH), lambda bi, g, i: (bi, i, g))
    return pl.pallas_call(
        _nsa_kernel,
        grid=(b, NSA_KV, s // tq),
        in_specs=[pl.BlockSpec((None, NSA_HPG, NSA_DH, tq), lambda bi, g, i: (bi, g, 0, i)),
                  pl.BlockSpec((None, None, s // SEL_BLOCK, tq), lambda bi, g, i: (bi, g, 0, i)),
                  kspec, vspec, kspec, vspec,
                  pl.BlockSpec((None, NSA_GATE_ROWS, tq), lambda bi, g, i: (bi, 0, i)),
                  pl.BlockSpec((None, None, NSA_HPG * NSA_DH, tq), lambda bi, g, i: (bi, g, 0, i))],
        out_specs=tok,
        out_shape=jax.ShapeDtypeStruct((b, s, W_NSA_Q), BF16),
        scratch_shapes=[pltpu.VMEM((WINDOW + tq - SLC_KT, NSA_HPG * tq), F32)] * 2,
        compiler_params=pltpu.CompilerParams(
            dimension_semantics=("arbitrary", "arbitrary", "arbitrary"),
            vmem_limit_bytes=VMEM_LIMIT),
        name="nsa_attn",
    )(qt, sel_bias, ks, vst, kw, vwt, gates_t, o_cmp)


def _out_ffn2_kernel(x_ref, og_ref, on_ref, wo_ref, nw_ref, wg_ref, wu_ref, wd_ref, fw_ref, o_ref):
    half = W_GDN_Z
    x = (x_ref[...] + _dot(og_ref[...].astype(BF16), wo_ref[:half, :])
         + _dot(on_ref[...], wo_ref[half:, :]))
    y = _swiglu_half_step(x, nw_ref[...], wg_ref, wu_ref, wd_ref)
    o_ref[...] = _rms(y, fw_ref[...])


def _out_ffn2(x2d, o_gdn, o_nsa, w_out, nw, wg, wu, wd, fw):
    t = x2d.shape[0]
    row = lambda w: pl.BlockSpec((TOKEN_TILE, w), lambda i: (i, 0))
    return pl.pallas_call(
        _out_ffn2_kernel,
        grid=(t // TOKEN_TILE,),
        in_specs=[row(D_MODEL), row(W_GDN_Z), row(W_NSA_Q), _const_spec((D_MODEL, D_MODEL)),
                  _const_spec((1, D_MODEL)), _const_spec((D_MODEL, D_FF)),
                  _const_spec((D_MODEL, D_FF)), _const_spec((D_FF, D_MODEL)),
                  _const_spec((1, D_MODEL))],
        out_specs=row(D_MODEL),
        out_shape=jax.ShapeDtypeStruct((t, D_MODEL), F32),
        compiler_params=pltpu.CompilerParams(dimension_semantics=("arbitrary",),
                                             vmem_limit_bytes=VMEM_LIMIT),
        name="out_ffn2",
    )(x2d, o_gdn, o_nsa, w_out, nw, wg, wu, wd, fw)


def _rope_tables(s_len):
    half = ROPE_DIM // 2
    inv = ROPE_THETA ** (-jnp.arange(half, dtype=F32) / half)
    ang = jnp.arange(s_len, dtype=F32)[:, None] * inv[None, :]
    cos, sin = jnp.cos(ang), jnp.sin(ang)
    pad = jnp.zeros((s_len, NSA_DH - ROPE_DIM), F32)
    zero = jnp.zeros_like(sin)
    cos_h = jnp.concatenate([cos, cos, pad + 1.0], axis=1)
    sa_h = jnp.concatenate([-sin, zero, pad], axis=1)
    sb_h = jnp.concatenate([zero, sin, pad], axis=1)
    rep = lambda a: jnp.tile(a, (1, LANES // NSA_DH))
    return rep(cos_h), rep(sa_h), rep(sb_h)


def _split_w_in(w_in):
    cuts = np.cumsum([W_GDN_QKV, W_GDN_Z, GDN_HEADS, GDN_HEADS, W_NSA_Q,
                      W_NSA_KV, W_NSA_KV, W_NSA_KV]).tolist()
    qkv, z, a, bb, nq, kvc, kvs, kvw, gate = jnp.split(w_in, cuts, axis=1)
    w_main = jnp.concatenate([qkv, z, nq, kvc, kvs, kvw], axis=1).astype(BF16)
    small = jnp.concatenate([a, bb, gate], axis=1)
    w_small = jnp.pad(small, ((0, 0), (0, LANES - small.shape[1]))).astype(BF16)
    return w_main, w_small


def _compress_weights(pe_k, w1_k, w2_k, pe_v, w1_v, w2_v):
    half = CMP_BLOCK // 2
    eye = jnp.eye(2 * NSA_KV, dtype=F32)

    def first_layer(lo):
        w1 = jnp.stack([w1_k, w1_v]).reshape(2, CMP_BLOCK, NSA_DH, CMP_HIDDEN)[:, lo:lo + half]
        w1 = jnp.repeat(w1, NSA_KV, axis=0)
        w = jnp.einsum('crdj,ce->rcdej', w1, eye)
        return w.reshape(half * 2 * NSA_KV * NSA_DH, 2 * NSA_KV * CMP_HIDDEN).astype(BF16)

    def pe_row(lo):
        pe = jnp.stack([pe_k, pe_v])[:, lo:lo + half]
        pe = jnp.repeat(pe, NSA_KV, axis=0)
        return pe.transpose(1, 0, 2).reshape(1, half * 2 * NSA_KV * NSA_DH)

    w2 = jnp.repeat(jnp.stack([w2_k, w2_v]), NSA_KV, axis=0)
    w2 = jnp.pad(w2, ((0, 0), (0, 0), (0, LANES - NSA_DH)))
    w2 = jnp.einsum('cjd,ce->cjed', w2, eye).reshape(2 * NSA_KV * CMP_HIDDEN, 2 * NSA_KV * LANES)
    return pe_row(0), pe_row(half), first_layer(0), first_layer(half), w2.astype(BF16)


def kernel(x, ffn1_norm, ffn1_w_gate, ffn1_w_up, ffn1_w_down, mix_norm, w_in, gdn_conv_w, gdn_a_log,
           gdn_dt_bias, gdn_out_norm, cmp_pe_k, cmp_w1_k, cmp_w2_k, cmp_pe_v, cmp_w1_v, cmp_w2_v,
           w_out, ffn2_norm, ffn2_w_gate, ffn2_w_up, ffn2_w_down, final_norm):
    b, s, d = x.shape
    assert d == D_MODEL and s % TOKEN_TILE == 0 and ffn1_norm.shape[0] == 1
    vec = lambda a: a.reshape(1, -1)
    pad_lanes = lambda a: jnp.pad(a.reshape(1, -1), ((0, 0), (0, LANES - a.size)))
    bf = lambda a: a.astype(BF16)

    x1 = _ffn1(x.reshape(b * s, d), vec(ffn1_norm[0]), bf(ffn1_w_gate[0]), bf(ffn1_w_up[0]),
               bf(ffn1_w_down[0]))

    w_main, w_small = _split_w_in(w_in[0])
    cos_t, sin_a, sin_b = _rope_tables(s)
    qkv, z, small, gates_t, qp, qt, kvc, ks, vst, kw, vwt = _in_proj(
        x1.reshape(b, s, d), vec(mix_norm[0]), w_main, w_small, cos_t, sin_a, sin_b,
        pad_lanes(gdn_a_log[0]), pad_lanes(gdn_dt_bias[0]))

    o_gdn = _gdn(qkv, z, small, gdn_conv_w[0], vec(gdn_out_norm[0]))

    pe_top, pe_bot, w_top, w_bot, w2 = _compress_weights(
        cmp_pe_k[0], cmp_w1_k[0], cmp_w2_k[0], cmp_pe_v[0], cmp_w1_v[0], cmp_w2_v[0])
    kcvc = _compress(kvc, pe_top, pe_bot, w_top, w_bot, w2)

    o_cmp, sel_bias = _nsa_select(qp, kcvc, gates_t)
    o_nsa = _nsa_attention(qt, sel_bias, ks, vst, kw, vwt, gates_t, o_cmp).reshape(b * s, W_NSA_Q)

    out = _out_ffn2(x1, o_gdn.reshape(b * s, W_GDN_Z), o_nsa, bf(w_out[0]), vec(ffn2_norm[0]),
                    bf(ffn2_w_gate[0]), bf(ffn2_w_up[0]), bf(ffn2_w_down[0]), vec(final_norm))
    return out.reshape(b, s, d)
```

```python
import functools

import numpy as np
import jax
import jax.numpy as jnp
from jax import lax
from jax.experimental import pallas as pl
from jax.experimental.pallas import tpu as pltpu

F32 = jnp.float32
BF16 = jnp.bfloat16

D_MODEL = 1024
D_FF = 2816
GDN_HEADS = 4
GDN_DK = 128
GDN_DV = 128
GDN_CONV = 4
GDN_CHUNK = 64
NSA_HEADS = 8
NSA_KV = 2
NSA_HPG = NSA_HEADS // NSA_KV
NSA_DH = 64
CMP_BLOCK = 32
CMP_STRIDE = 16
CMP_HIDDEN = 128
SEL_BLOCK = 64
SEL_TOPK = 16
WINDOW = 512
ROPE_THETA = 500000.0
ROPE_DIM = NSA_DH // 4
EPS = 1e-6
NEG = -1e30
LOG2E = 1.4426950408889634
BIG = 1e30

W_GDN_QKV = GDN_HEADS * (2 * GDN_DK + GDN_DV)
W_GDN_Z = GDN_HEADS * GDN_DV
W_NSA_Q = NSA_HEADS * NSA_DH
W_NSA_KV = 2 * NSA_KV * NSA_DH
W_NSA_GATE = 3 * NSA_HEADS
W_MAIN = W_GDN_QKV + W_GDN_Z + W_NSA_Q + 3 * W_NSA_KV
LANES = 128
GATE_LANE0 = 2 * GDN_HEADS

TOKEN_TILE = 512
MXU_TILE = 256
FF_SPLITS = (0, 6 * MXU_TILE, D_FF)
GDN_HPS = 2
GDN_GROUP = 2 * GDN_CHUNK
NSA_TQ = 256
NSA_TS = 512
SLC_KT = 256
NSA_VT_ROWS = 80
NSA_GATE_ROWS = 32
VMEM_LIMIT = 56 * 1024 * 1024


def _dot(a, b):
    return jnp.dot(a, b, preferred_element_type=F32)


def _dot_nt(a, b):
    return lax.dot_general(a, b, (((1,), (1,)), ((), ())), preferred_element_type=F32)


def _dot_tn(a, b):
    return lax.dot_general(a, b, (((0,), (0,)), ((), ())), preferred_element_type=F32)


def _rms(x, w):
    return x * lax.rsqrt(jnp.mean(x * x, axis=-1, keepdims=True) + EPS) * w


def _sigmoid(x):
    return 0.5 * jnp.tanh(0.5 * x) + 0.5


def _silu(x):
    return x * _sigmoid(x)


def _swiglu_half_step(x, nw, wg_ref, wu_ref, wd_ref):
    hb = _rms(x, nw).astype(BF16)
    acc = jnp.zeros_like(x)
    for lo, hi in zip(FF_SPLITS[:-1], FF_SPLITS[1:]):
        sl = slice(lo, hi)
        g = _dot(hb, wg_ref[:, sl])
        u = _dot(hb, wu_ref[:, sl])
        acc = acc + _dot((_silu(g) * u).astype(BF16), wd_ref[sl, :])
    return x + 0.5 * acc


def _ffn_kernel(x_ref, nw_ref, wg_ref, wu_ref, wd_ref, o_ref):
    o_ref[...] = _swiglu_half_step(x_ref[...], nw_ref[...], wg_ref, wu_ref, wd_ref)


def _const_spec(shape):
    nd = len(shape)
    return pl.BlockSpec(shape, lambda *_: (0,) * nd, pipeline_mode=pl.Buffered(1))


def _ffn1(x2d, nw, wg, wu, wd):
    t = x2d.shape[0]
    row = pl.BlockSpec((TOKEN_TILE, D_MODEL), lambda i: (i, 0))
    return pl.pallas_call(
        _ffn_kernel,
        grid=(t // TOKEN_TILE,),
        in_specs=[row, _const_spec((1, D_MODEL)), _const_spec((D_MODEL, D_FF)),
                  _const_spec((D_MODEL, D_FF)), _const_spec((D_FF, D_MODEL))],
        out_specs=row,
        out_shape=jax.ShapeDtypeStruct((t, D_MODEL), F32),
        compiler_params=pltpu.CompilerParams(dimension_semantics=("arbitrary",),
                                             vmem_limit_bytes=VMEM_LIMIT),
        name="ffn1",
    )(x2d, nw, wg, wu, wd)


def _rope(x, cos_t, sin_a, sin_b):
    return (x * cos_t + pltpu.roll(x, LANES - ROPE_DIM // 2, axis=1) * sin_a
            + pltpu.roll(x, ROPE_DIM // 2, axis=1) * sin_b)


def _inproj_kernel(x_ref, nw_ref, wm_ref, ws_ref, cos_ref, sa_ref, sb_ref, alog_ref, dt_ref,
                   qkv_ref, z_ref, small_ref, gt_ref, qp_ref, qt_ref, kvc_ref,
                   ks_ref, vst_ref, kw_ref, vwt_ref, kvc_s):
    hb = _rms(x_ref[...], nw_ref[...]).astype(BF16)
    proj = lambda lo, width: _dot(hb, wm_ref[:, lo:lo + width])
    tm = x_ref.shape[0]
    lane = lax.broadcasted_iota(jnp.int32, (tm, LANES), 1)
    sm = _dot(hb, ws_ref[...])
    xg = sm + dt_ref[...]
    softplus = jnp.maximum(xg, 0.0) + jnp.log1p(jnp.exp(-jnp.abs(xg)))
    small = jnp.where(lane < GDN_HEADS, -jnp.exp(alog_ref[...]) * softplus, _sigmoid(sm))
    small_ref[...] = small
    gt_ref[...] = small.T[:gt_ref.shape[0], :]
    qkv_ref[...] = proj(0, W_GDN_QKV)
    o = W_GDN_QKV
    z_ref[...] = proj(o, W_GDN_Z)
    o += W_GDN_Z
    cos_t, sin_a, sin_b = cos_ref[...], sa_ref[...], sb_ref[...]
    scale = NSA_DH ** -0.5
    pos = pl.program_id(1) * tm + lax.broadcasted_iota(jnp.int32, (tm, LANES), 0)
    low_half = lane < NSA_DH
    block_onehot = jnp.where(lane - NSA_DH == pos // SEL_BLOCK, 1.0, 0.0)

    def halves(slab):
        return slab, pltpu.roll(slab, NSA_DH, axis=1)

    y = proj(o, W_NSA_Q)
    for s in range(W_NSA_Q // LANES):
        slab = y[:, s * LANES:(s + 1) * LANES]
        plain_t = (slab * scale).T
        rot_t = (_rope(slab, cos_t, sin_a, sin_b) * (scale * LOG2E)).T
        for j in range(2):
            qp_ref[2 * s + j] = plain_t[j * NSA_DH:(j + 1) * NSA_DH, :].astype(BF16)
            qt_ref[2 * s + j] = rot_t[j * NSA_DH:(j + 1) * NSA_DH, :].astype(BF16)
    o += W_NSA_Q
    y = proj(o, W_NSA_KV)
    for half in range(W_NSA_KV // LANES):
        kvc_s[half] = y[:, half * LANES:(half + 1) * LANES]
    for r in range(CMP_STRIDE):
        for half in range(W_NSA_KV // LANES):
            dst = r * W_NSA_KV + half * LANES
            kvc_ref[:, dst:dst + LANES] = kvc_s[half, pl.ds(r, tm // CMP_STRIDE, stride=CMP_STRIDE), :]
    o += W_NSA_KV
    n_tail = vst_ref.shape[1] - NSA_DH
    tail = jnp.where(lax.broadcasted_iota(jnp.int32, (n_tail, tm), 0) == 0, 1.0, 0.0)
    for k_ref, vt_ref, k_pad in ((ks_ref, vst_ref, block_onehot), (kw_ref, vwt_ref, 0.0)):
        y = proj(o, W_NSA_KV)
        kr = halves(_rope(y[:, :LANES], cos_t, sin_a, sin_b))
        v_t = y[:, LANES:].T
        for j in range(NSA_KV):
            k_ref[j] = jnp.where(low_half, kr[j], k_pad).astype(BF16)
            vt_ref[j] = jnp.concatenate([v_t[j * NSA_DH:(j + 1) * NSA_DH, :], tail], axis=0).astype(BF16)
        o += W_NSA_KV


def _in_proj(x3d, nw, w_main, w_small, cos_t, sin_a, sin_b, alog_p, dt_p):
    b, s, _ = x3d.shape
    tm = TOKEN_TILE
    row = lambda w: pl.BlockSpec((None, tm, w), lambda bi, i: (bi, i, 0))
    heads = lambda n, w: pl.BlockSpec((None, n, tm, w), lambda bi, i: (bi, 0, i, 0))
    heads_t = lambda n, r: pl.BlockSpec((None, n, r, tm), lambda bi, i: (bi, 0, 0, i))
    tab = pl.BlockSpec((tm, LANES), lambda bi, i: (i, 0))
    bf_heads = lambda n, w: jax.ShapeDtypeStruct((b, n, s, w), BF16)
    out_shape = (
        jax.ShapeDtypeStruct((b, s, W_GDN_QKV), F32),
        jax.ShapeDtypeStruct((b, s, W_GDN_Z), F32),
        jax.ShapeDtypeStruct((b, s, LANES), F32),
        jax.ShapeDtypeStruct((b, NSA_GATE_ROWS, s), F32),
        jax.ShapeDtypeStruct((b, NSA_HEADS, NSA_DH, s), BF16),
        jax.ShapeDtypeStruct((b, NSA_HEADS, NSA_DH, s), BF16),
        jax.ShapeDtypeStruct((b, s // CMP_STRIDE, CMP_STRIDE * W_NSA_KV), F32),
        bf_heads(NSA_KV, LANES),
        jax.ShapeDtypeStruct((b, NSA_KV, NSA_VT_ROWS, s), BF16),
        bf_heads(NSA_KV, LANES),
        jax.ShapeDtypeStruct((b, NSA_KV, NSA_VT_ROWS, s), BF16),
    )
    return pl.pallas_call(
        _inproj_kernel,
        grid=(b, s // tm),
        in_specs=[row(D_MODEL), _const_spec((1, D_MODEL)), _const_spec((D_MODEL, W_MAIN)),
                  _const_spec((D_MODEL, LANES)), tab, tab, tab,
                  _const_spec((1, LANES)), _const_spec((1, LANES))],
        out_specs=(row(W_GDN_QKV), row(W_GDN_Z), row(LANES),
                   pl.BlockSpec((None, NSA_GATE_ROWS, tm), lambda bi, i: (bi, 0, i)),
                   heads_t(NSA_HEADS, NSA_DH), heads_t(NSA_HEADS, NSA_DH),
                   pl.BlockSpec((None, tm // CMP_STRIDE, CMP_STRIDE * W_NSA_KV), lambda bi, i: (bi, i, 0)),
                   heads(NSA_KV, LANES), heads_t(NSA_KV, NSA_VT_ROWS),
                   heads(NSA_KV, LANES), heads_t(NSA_KV, NSA_VT_ROWS)),
        out_shape=out_shape,
        scratch_shapes=[pltpu.VMEM((W_NSA_KV // LANES, tm, LANES), F32)],
        compiler_params=pltpu.CompilerParams(dimension_semantics=("arbitrary", "arbitrary"),
                                             vmem_limit_bytes=VMEM_LIMIT),
        name="in_proj",
    )(x3d, nw, w_main, w_small, cos_t, sin_a, sin_b, alog_p, dt_p)


def _bmm(a, b):
    return jnp.einsum('nij,njk->nik', a, b, preferred_element_type=F32)


def _bmm_nt(a, b):
    return jnp.einsum('nid,njd->nij', a, b, preferred_element_type=F32)


def _unit_lower_inverse_minus_eye(low, ii, jj):
    mm = lambda a, b: _bmm(a.astype(BF16), b.astype(BF16))
    same16 = (ii // 16) == (jj // 16)
    same32 = (ii // 32) == (jj // 32)
    ld = jnp.where(same16, low, 0.0)
    e = -ld
    p = mm(ld, ld)
    for level in range(3):
        e = e + p + mm(e, p)
        if level < 2:
            p = mm(p, p)
    for within in (same32 & jnp.logical_not(same16), jnp.logical_not(same32)):
        c = jnp.where(within, low, 0.0)
        y = c + mm(e, c)
        e = e - y - mm(y, e)
    return e


def _gdn_kernel(xq_ref, xk_ref, xv_ref, wq_ref, wk_ref, wv_ref, small_ref, z_ref,
                onorm_ref, o_ref,
                p_s, qn_s, e2_s, mpa_s, na_s, ea_s, qt_s, o0_s, seven_s):
    hp = pl.program_id(1)
    s_len = xq_ref.shape[0]
    c_len = GDN_CHUNK
    grp = GDN_GROUP
    n_chunks = s_len // c_len
    n_grp = s_len // grp
    hd = GDN_DK
    row = lax.broadcasted_iota(jnp.int32, (s_len, LANES), 0)
    lane = lax.broadcasted_iota(jnp.int32, (s_len, LANES), 1)
    pos = row % c_len
    ii = lax.broadcasted_iota(jnp.int32, (grp, grp), 0)
    jj = lax.broadcasted_iota(jnp.int32, (grp, grp), 1)
    same_chunk = (ii // c_len) == (jj // c_len)
    tril = same_chunk & (ii >= jj)
    strict = same_chunk & (ii > jj)
    first = lax.broadcasted_iota(jnp.int32, (grp, hd), 0) < c_len
    small = small_ref[...]
    g3 = lambda a: a.reshape(n_grp, grp, hd)

    top = GDN_CONV * 2
    row_top = lax.broadcasted_iota(jnp.int32, (top, hd), 0)

    def conv_silu(x_ref, w_ref, cols):
        w = w_ref[:, cols]
        x_top = x_ref[:top, cols]
        acc_top = x_top * w[GDN_CONV - 1:GDN_CONV, :]
        acc = x_ref[top:, cols] * w[GDN_CONV - 1:GDN_CONV, :]
        for sh in range(1, GDN_CONV):
            tap = w[GDN_CONV - 1 - sh:GDN_CONV - sh, :]
            acc_top = acc_top + jnp.where(row_top >= sh, pltpu.roll(x_top, sh, axis=0), 0.0) * tap
            acc = acc + x_ref[pl.ds(top - sh, s_len - top), cols] * tap
        return _silu(jnp.concatenate([acc_top, acc], axis=0))

    def l2n(x):
        return x * lax.rsqrt(jnp.sum(x * x, axis=-1, keepdims=True) + EPS)

    for j in range(GDN_HPS):
        h = hp * GDN_HPS + j
        cols = slice(j * hd, (j + 1) * hd)
        q = l2n(conv_silu(xq_ref, wq_ref, cols)) * (GDN_DK ** -0.5)
        k = l2n(conv_silu(xk_ref, wk_ref, cols))
        v = conv_silu(xv_ref, wv_ref, cols)

        pick = lambda ln: jnp.broadcast_to(
            jnp.sum(jnp.where(lane == ln, small, 0.0), axis=-1, keepdims=True), (s_len, LANES))
        beta = pick(GDN_HEADS + h)
        g = pick(h)
        sh = 1
        while sh < c_len:
            g = g + jnp.where(pos >= sh, pltpu.roll(g, sh, axis=0), 0.0)
            sh *= 2
        gc = g
        gc3 = gc.reshape(n_chunks, c_len, LANES)
        gl3 = gc3[:, c_len - 1:c_len, :]
        e_gl = jnp.exp(gl3).reshape(n_grp, grp // c_len, 1, LANES)
        dec_a, dec_b = e_gl[:, 0], e_gl[:, 1]
        gl = jnp.broadcast_to(gl3, gc3.shape).reshape(s_len, LANES)
        eg = jnp.exp(gc)
        kb = k * beta
        vb = g3(v * beta)
        kbg = g3(kb * eg)

        k3 = g3(k).astype(BF16)
        kk = _bmm_nt(g3(kb).astype(BF16), k3)
        qk = _bmm_nt(g3(q).astype(BF16), k3)
        gcg = g3(gc)
        diff = gcg - jnp.swapaxes(gcg, 1, 2)
        decay = jnp.where(tril, jnp.exp(jnp.where(tril, diff, 0.0)), 0.0)
        e_b = _unit_lower_inverse_minus_eye(jnp.where(strict, kk * decay, 0.0), ii, jj).astype(BF16)
        u = vb + _bmm(e_b, vb.astype(BF16))
        w = kbg + _bmm(e_b, kbg.astype(BF16))
        qkm = jnp.where(tril, qk * decay, 0.0).astype(BF16)
        r = _bmm(qkm, jnp.concatenate([w, u], axis=-1).astype(BF16))
        qt_s[j] = (g3(q * eg) - r[..., :hd]).astype(BF16)
        o0_s[j] = r[..., hd:]
        kd_t = jnp.swapaxes(g3(k * jnp.exp(gl - gc)), 1, 2).astype(BF16)
        rhs = jnp.concatenate([jnp.where(first, w, 0.0), jnp.where(first, 0.0, w),
                               jnp.where(first, u, 0.0), jnp.where(first, 0.0, u)], axis=-1)
        mn = _bmm(kd_t, rhs.astype(BF16))
        mp_a, mp_b = -mn[..., :hd], -mn[..., hd:2 * hd]
        n_a, n_b = mn[..., 2 * hd:3 * hd], mn[..., 3 * hd:]
        x = _bmm(mp_b.astype(BF16), jnp.concatenate([mp_a, n_a], axis=-1).astype(BF16))
        p_s[j] = (dec_b * mp_a + dec_a * mp_b + x[..., :hd]).astype(BF16)
        qn_s[j] = dec_b * n_a + x[..., hd:] + n_b
        e2_s[j] = dec_a * dec_b
        mpa_s[j] = mp_a.astype(BF16)
        na_s[j] = n_a
        ea_s[j] = dec_a

    def scan(n, states):
        out = []
        for j in range(GDN_HPS):
            sb = states[j].astype(BF16)
            seven_s[j, n] = sb
            out.append(e2_s[j, n] * states[j] + _dot(p_s[j, n], sb) + qn_s[j, n])
        return tuple(out)

    lax.fori_loop(0, n_grp, scan, tuple(jnp.zeros((GDN_DK, GDN_DV), F32) for _ in range(GDN_HPS)))

    for j in range(GDN_HPS):
        cols = slice(j * hd, (j + 1) * hd)
        s_even = seven_s[j]
        s_odd = (ea_s[j] * s_even.astype(F32) + _bmm(mpa_s[j], s_even) + na_s[j]).astype(BF16)
        qt = qt_s[j]
        o = jnp.concatenate([_bmm(qt[:, :c_len], s_even), _bmm(qt[:, c_len:], s_odd)], axis=1) + o0_s[j]
        o_ref[:, cols] = _rms(o.reshape(s_len, hd), onorm_ref[...]) * _silu(z_ref[:, cols])


def _gdn(qkv, z, small, conv_w, onorm):
    b, s, _ = qkv.shape
    wid = GDN_HPS * GDN_DK
    n_hp = GDN_HEADS // GDN_HPS
    n_grp = s // GDN_GROUP
    per_pair = lambda rows, dt: pltpu.VMEM((GDN_HPS, n_grp, rows, GDN_DV), dt)
    col = lambda off: pl.BlockSpec((None, s, wid), lambda bi, h: (bi, 0, off + h))
    wcol = lambda off: pl.BlockSpec((GDN_CONV, wid), lambda bi, h: (0, off + h))
    vec = pl.BlockSpec((1, LANES), lambda bi, h: (0, 0))
    return pl.pallas_call(
        _gdn_kernel,
        grid=(b, n_hp),
        in_specs=[col(0), col(n_hp), col(2 * n_hp),
                  wcol(0), wcol(n_hp), wcol(2 * n_hp),
                  pl.BlockSpec((None, s, LANES), lambda bi, h: (bi, 0, 0)),
                  col(0), vec],
        out_specs=col(0),
        out_shape=jax.ShapeDtypeStruct((b, s, GDN_HEADS * GDN_DV), F32),
        scratch_shapes=[per_pair(GDN_DK, BF16), per_pair(GDN_DK, F32), per_pair(1, F32),
                        per_pair(GDN_DK, BF16), per_pair(GDN_DK, F32), per_pair(1, F32),
                        per_pair(GDN_GROUP, BF16), per_pair(GDN_GROUP, F32), per_pair(GDN_DK, BF16)],
        compiler_params=pltpu.CompilerParams(dimension_semantics=("arbitrary", "arbitrary"),
                                             vmem_limit_bytes=VMEM_LIMIT),
        name="gdn",
    )(qkv, qkv, qkv, conv_w, conv_w, conv_w, small, z, onorm)


def _compress_kernel(t_ref, pet_ref, peb_ref, wt_ref, wb_ref, w2_ref, o_ref):
    t = t_ref[...]
    a = _dot((t + pet_ref[...]).astype(BF16), wt_ref[...])
    bm = _dot((t + peb_ref[...]).astype(BF16), wb_ref[...])
    n_rows = t.shape[0]
    hid = a + pltpu.roll(bm, n_rows - 1, axis=0)
    out = _dot(_silu(hid).astype(BF16), w2_ref[...])
    for j in range(2 * NSA_KV):
        blk = out[:, j * LANES:(j + 1) * LANES]
        o_ref[j] = (blk if j < NSA_KV else blk.T).astype(o_ref.dtype)


def _compress(tm, pe_top, pe_bot, w_top, w_bot, w2):
    b, n_rows, width = tm.shape
    nh = 2 * NSA_KV * CMP_HIDDEN
    return pl.pallas_call(
        _compress_kernel,
        grid=(b,),
        in_specs=[pl.BlockSpec((None, n_rows, width), lambda bi: (bi, 0, 0)),
                  _const_spec((1, width)), _const_spec((1, width)),
                  _const_spec((width, nh)), _const_spec((width, nh)),
                  _const_spec((nh, 2 * NSA_KV * LANES))],
        out_specs=pl.BlockSpec((None, 2 * NSA_KV, n_rows, LANES), lambda bi: (bi, 0, 0, 0)),
        out_shape=jax.ShapeDtypeStruct((b, 2 * NSA_KV, n_rows, LANES), BF16),
        compiler_params=pltpu.CompilerParams(dimension_semantics=("arbitrary",),
                                             vmem_limit_bytes=VMEM_LIMIT),
        name="compress",
    )(tm, pe_top, pe_bot, w_top, w_bot, w2)


def _nsa_select_kernel(qp_ref, kc_ref, vct_ref, gt_ref, ocmp_ref, bias_ref):
    grp = pl.program_id(1)
    ts, hpg, dh = qp_ref.shape[2], NSA_HPG, NSA_DH
    q0 = pl.program_id(2) * ts
    n_cmp = kc_ref.shape[0]
    n_blk = n_cmp * CMP_STRIDE // SEL_BLOCK
    per_head = lambda a: jnp.concatenate([a] * hpg, axis=1)

    t_c = q0 + lax.broadcasted_iota(jnp.int32, (n_cmp, ts), 1)
    n_c = lax.broadcasted_iota(jnp.int32, (n_cmp, ts), 0)
    vis = per_head(jnp.where(n_c * CMP_STRIDE + (CMP_BLOCK - 1) <= t_c, 1.0, 0.0))
    pad = jnp.zeros((LANES - dh, ts), BF16)
    q_t = jnp.concatenate([jnp.concatenate([qp_ref[hh], pad], axis=0) for hh in range(hpg)], axis=1)
    s_cmp = jnp.where(vis > 0.5, _dot(kc_ref[...], q_t), NEG)
    p_cmp = jnp.exp(s_cmp - jnp.max(s_cmp, axis=0, keepdims=True)) * vis
    p_cmp = p_cmp * (1.0 / jnp.maximum(jnp.sum(p_cmp, axis=0, keepdims=True), 1e-30))
    o_cmp = _dot(vct_ref[...], p_cmp.astype(BF16))
    ocmp_ref[...] = jnp.concatenate(
        [gt_ref[pl.ds(GATE_LANE0 + (grp * hpg + hh) * 3, 1), :] * o_cmp[:dh, hh * ts:(hh + 1) * ts]
         for hh in range(hpg)], axis=0)

    p_sum = p_cmp[:, :ts]
    for hh in range(1, hpg):
        p_sum = p_sum + p_cmp[:, hh * ts:(hh + 1) * ts]
    oj = lax.broadcasted_iota(jnp.int32, (n_blk, n_cmp), 0) * SEL_BLOCK
    on = lax.broadcasted_iota(jnp.int32, (n_blk, n_cmp), 1) * CMP_STRIDE
    overlap_t = jnp.where(on < oj + SEL_BLOCK, jnp.where(on + CMP_BLOCK > oj, 1.0, 0.0), 0.0)
    imp_t = jnp.dot(overlap_t, p_sum, precision=lax.Precision.HIGHEST,
                    preferred_element_type=F32)
    blk = lax.broadcasted_iota(jnp.int32, (n_blk, ts), 0)
    ahead = (q0 + lax.broadcasted_iota(jnp.int32, (n_blk, ts), 1)) // SEL_BLOCK - blk
    score = jnp.where(ahead < 0, NEG, jnp.where(blk == 0, BIG, jnp.where(ahead <= 1, BIG, imp_t)))
    rank = jnp.zeros((n_blk, ts), F32)
    for i in range(n_blk):
        ci = jnp.broadcast_to(score[i:i + 1, :], (n_blk, ts))
        tie = jnp.where(blk > i, 1.0, 0.0)
        rank = rank + jnp.where(ci > score, 1.0, jnp.where(ci == score, tie, 0.0))
    bias_ref[...] = jnp.where(rank >= SEL_TOPK, NEG, 0.0).astype(bias_ref.dtype)


def _nsa_select(qp_t, kcvc, gates_t):
    b, _, _, s = qp_t.shape
    ts = NSA_TS
    n_cmp = kcvc.shape[2]
    cspec = lambda off: pl.BlockSpec((None, None, n_cmp, LANES), lambda bi, g, i: (bi, off + g, 0, 0))
    return pl.pallas_call(
        _nsa_select_kernel,
        grid=(b, NSA_KV, s // ts),
        in_specs=[pl.BlockSpec((None, NSA_HPG, NSA_DH, ts), lambda bi, g, i: (bi, g, 0, i)),
                  cspec(0), cspec(NSA_KV),
                  pl.BlockSpec((None, NSA_GATE_ROWS, ts), lambda bi, g, i: (bi, 0, i))],
        out_specs=(pl.BlockSpec((None, None, NSA_HPG * NSA_DH, ts), lambda bi, g, i: (bi, g, 0, i)),
                   pl.BlockSpec((None, None, s // SEL_BLOCK, ts), lambda bi, g, i: (bi, g, 0, i))),
        out_shape=(jax.ShapeDtypeStruct((b, NSA_KV, NSA_HPG * NSA_DH, s), F32),
                   jax.ShapeDtypeStruct((b, NSA_KV, s // SEL_BLOCK, s), BF16)),
        compiler_params=pltpu.CompilerParams(
            dimension_semantics=("arbitrary", "arbitrary", "arbitrary"),
            vmem_limit_bytes=VMEM_LIMIT),
        name="nsa_select",
    )(qp_t, kcvc, kcvc, gates_t)


def _nsa_kernel(qt_ref, bias_ref, ks_ref, vst_ref, kw_ref, vwt_ref, gt_ref, ocmp_ref, o_ref,
                *score_bufs):
    grp = pl.program_id(1)
    q0 = pl.program_id(2) * NSA_TQ
    tq, hpg, dh = NSA_TQ, NSA_HPG, NSA_DH
    bias = bias_ref[...]
    pad = jnp.zeros((LANES - dh - bias.shape[0], tq), BF16)
    q_aug = jnp.concatenate(
        [jnp.concatenate([qt_ref[hh], bias, pad], axis=0) for hh in range(hpg)], axis=1)
    per_head = lambda a: jnp.concatenate([a] * hpg, axis=1)

    def attend(*streams):
        tasks = [(i, chunk) for i, chunks in enumerate(streams) for chunk in chunks]

        def scores(n):
            k, _, mask = tasks[n][1]
            s_t = _dot(k, q_aug)
            score_bufs[n % 2][:k.shape[0], :] = s_t if mask is None else s_t + mask

        state = [(None, None)] * len(streams)
        scores(0)
        for n, (i, (k, v_t, _)) in enumerate(tasks):
            if n + 1 < len(tasks):
                scores(n + 1)
            s_t = score_bufs[n % 2][:k.shape[0], :]
            m, acc = state[i]
            m_new = jnp.max(s_t, axis=0, keepdims=True)
            if m is not None:
                m_new = jnp.maximum(m, m_new)
            part = _dot(v_t, jnp.exp2(s_t - m_new).astype(BF16))
            state[i] = (m_new, part if m is None else acc * jnp.exp2(m - m_new) + part)
        return [acc[:dh, :] * (1.0 / jnp.maximum(acc[dh:dh + 1, :], 1e-30)) for _, acc in state]

    kt = SLC_KT
    last = (q0 + tq - 1) // kt
    w0 = pl.multiple_of(jnp.maximum(q0 - WINDOW, 0), tq)
    row_k = lax.broadcasted_iota(jnp.int32, (kt, tq), 0)
    t_q = q0 + lax.broadcasted_iota(jnp.int32, (kt, tq), 1)

    def step(n_chunks):
        slc = []
        for c in range(n_chunks):
            rows = slice(c * kt, (c + 1) * kt)
            causal = per_head(jnp.where(c * kt + row_k <= t_q, 0.0, NEG)) if c == n_chunks - 1 else None
            slc.append((ks_ref[rows, :], vst_ref[:, rows], causal))
        win = []
        for off, n in ((0, WINDOW + tq - kt), (WINDOW + tq - kt, kt)):
            back = (q0 + lax.broadcasted_iota(jnp.int32, (n, tq), 1)
                    - (w0 + off + lax.broadcasted_iota(jnp.int32, (n, tq), 0)))
            band = per_head(jnp.where(back < 0, NEG, jnp.where(back < WINDOW, 0.0, NEG)))
            keys = pl.ds(pl.multiple_of(w0 + off, tq), n)
            win.append((kw_ref[keys, :], vwt_ref[:, keys], band))
        o_win, o_slc = attend(win, slc)
        mixed = []
        for hh in range(hpg):
            base = GATE_LANE0 + (grp * hpg + hh) * 3
            head = slice(hh * tq, (hh + 1) * tq)
            mixed.append(gt_ref[pl.ds(base + 1, 1), :] * o_slc[:, head]
                         + gt_ref[pl.ds(base + 2, 1), :] * o_win[:, head])
        o_ref[...] = (ocmp_ref[...] + jnp.concatenate(mixed, axis=0)).T.astype(o_ref.dtype)

    for n_chunks in range(1, ks_ref.shape[0] // kt + 1):
        pl.when(last == n_chunks - 1)(functools.partial(step, n_chunks))


def _nsa_attention(qt, sel_bias, ks, vst, kw, vwt, gates_t, o_cmp):
    b, _, _, s = qt.shape
    tq = NSA_TQ
    kspec = pl.BlockSpec((None, None, s, LANES), lambda bi, g, i: (bi, g, 0, 0))
    vspec = pl.BlockSpec((None, None, NSA_VT_ROWS, s), lambda bi, g, i: (bi, g, 0, 0))
    tok = pl.BlockSpec((None, tq, NSA_HPG * NSA_DH), lambda bi, g, i: (bi, i, g))
    return pl.pallas_call(
        _nsa_kernel,
        grid=(b, NSA_KV, s // tq),
        in_specs=[pl.BlockSpec((None, NSA_HPG, NSA_DH, tq), lambda bi, g, i: (bi, g, 0, i)),
                  pl.BlockSpec((None, None, s // SEL_BLOCK, tq), lambda bi, g, i: (bi, g, 0, i)),
                  kspec, vspec, kspec, vspec,
                  pl.BlockSpec((None, NSA_GATE_ROWS, tq), lambda bi, g, i: (bi, 0, i)),
                  pl.BlockSpec((None, None, NSA_HPG * NSA_DH, tq), lambda bi, g, i: (bi, g, 0, i))],
        out_specs=tok,
        out_shape=jax.ShapeDtypeStruct((b, s, W_NSA_Q), BF16),
        scratch_shapes=[pltpu.VMEM((WINDOW + tq - SLC_KT, NSA_HPG * tq), F32)] * 2,
        compiler_params=pltpu.CompilerParams(
            dimension_semantics=("arbitrary", "arbitrary", "arbitrary"),
            vmem_limit_bytes=VMEM_LIMIT),
        name="nsa_attn",
    )(qt, sel_bias, ks, vst, kw, vwt, gates_t, o_cmp)


def _out_ffn2_kernel(x_ref, og_ref, on_ref, wo_ref, nw_ref, wg_ref, wu_ref, wd_ref, fw_ref, o_ref):
    half = W_GDN_Z
    x = (x_ref[...] + _dot(og_ref[...].astype(BF16), wo_ref[:half, :])
         + _dot(on_ref[...], wo_ref[half:, :]))
    y = _swiglu_half_step(x, nw_ref[...], wg_ref, wu_ref, wd_ref)
    o_ref[...] = _rms(y, fw_ref[...])


def _out_ffn2(x2d, o_gdn, o_nsa, w_out, nw, wg, wu, wd, fw):
    t = x2d.shape[0]
    row = lambda w: pl.BlockSpec((TOKEN_TILE, w), lambda i: (i, 0))
    return pl.pallas_call(
        _out_ffn2_kernel,
        grid=(t // TOKEN_TILE,),
        in_specs=[row(D_MODEL), row(W_GDN_Z), row(W_NSA_Q), _const_spec((D_MODEL, D_MODEL)),
                  _const_spec((1, D_MODEL)), _const_spec((D_MODEL, D_FF)),
                  _const_spec((D_MODEL, D_FF)), _const_spec((D_FF, D_MODEL)),
                  _const_spec((1, D_MODEL))],
        out_specs=row(D_MODEL),
        out_shape=jax.ShapeDtypeStruct((t, D_MODEL), F32),
        compiler_params=pltpu.CompilerParams(dimension_semantics=("arbitrary",),
                                             vmem_limit_bytes=VMEM_LIMIT),
        name="out_ffn2",
    )(x2d, o_gdn, o_nsa, w_out, nw, wg, wu, wd, fw)


def _rope_tables(s_len):
    half = ROPE_DIM // 2
    inv = ROPE_THETA ** (-jnp.arange(half, dtype=F32) / half)
    ang = jnp.arange(s_len, dtype=F32)[:, None] * inv[None, :]
    cos, sin = jnp.cos(ang), jnp.sin(ang)
    pad = jnp.zeros((s_len, NSA_DH - ROPE_DIM), F32)
    zero = jnp.zeros_like(sin)
    cos_h = jnp.concatenate([cos, cos, pad + 1.0], axis=1)
    sa_h = jnp.concatenate([-sin, zero, pad], axis=1)
    sb_h = jnp.concatenate([zero, sin, pad], axis=1)
    rep = lambda a: jnp.tile(a, (1, LANES // NSA_DH))
    return rep(cos_h), rep(sa_h), rep(sb_h)


def _split_w_in(w_in):
    cuts = np.cumsum([W_GDN_QKV, W_GDN_Z, GDN_HEADS, GDN_HEADS, W_NSA_Q,
                      W_NSA_KV, W_NSA_KV, W_NSA_KV]).tolist()
    qkv, z, a, bb, nq, kvc, kvs, kvw, gate = jnp.split(w_in, cuts, axis=1)
    w_main = jnp.concatenate([qkv, z, nq, kvc, kvs, kvw], axis=1).astype(BF16)
    small = jnp.concatenate([a, bb, gate], axis=1)
    w_small = jnp.pad(small, ((0, 0), (0, LANES - small.shape[1]))).astype(BF16)
    return w_main, w_small


def _compress_weights(pe_k, w1_k, w2_k, pe_v, w1_v, w2_v):
    half = CMP_BLOCK // 2
    eye = jnp.eye(2 * NSA_KV, dtype=F32)

    def first_layer(lo):
        w1 = jnp.stack([w1_k, w1_v]).reshape(2, CMP_BLOCK, NSA_DH, CMP_HIDDEN)[:, lo:lo + half]
        w1 = jnp.repeat(w1, NSA_KV, axis=0)
        w = jnp.einsum('crdj,ce->rcdej', w1, eye)
        return w.reshape(half * 2 * NSA_KV * NSA_DH, 2 * NSA_KV * CMP_HIDDEN).astype(BF16)

    def pe_row(lo):
        pe = jnp.stack([pe_k, pe_v])[:, lo:lo + half]
        pe = jnp.repeat(pe, NSA_KV, axis=0)
        return pe.transpose(1, 0, 2).reshape(1, half * 2 * NSA_KV * NSA_DH)

    w2 = jnp.repeat(jnp.stack([w2_k, w2_v]), NSA_KV, axis=0)
    w2 = jnp.pad(w2, ((0, 0), (0, 0), (0, LANES - NSA_DH)))
    w2 = jnp.einsum('cjd,ce->cjed', w2, eye).reshape(2 * NSA_KV * CMP_HIDDEN, 2 * NSA_KV * LANES)
    return pe_row(0), pe_row(half), first_layer(0), first_layer(half), w2.astype(BF16)


def kernel(x, ffn1_norm, ffn1_w_gate, ffn1_w_up, ffn1_w_down, mix_norm, w_in, gdn_conv_w, gdn_a_log,
           gdn_dt_bias, gdn_out_norm, cmp_pe_k, cmp_w1_k, cmp_w2_k, cmp_pe_v, cmp_w1_v, cmp_w2_v,
           w_out, ffn2_norm, ffn2_w_gate, ffn2_w_up, ffn2_w_down, final_norm):
    b, s, d = x.shape
    assert d == D_MODEL and s % TOKEN_TILE == 0 and ffn1_norm.shape[0] == 1
    vec = lambda a: a.reshape(1, -1)
    pad_lanes = lambda a: jnp.pad(a.reshape(1, -1), ((0, 0), (0, LANES - a.size)))
    bf = lambda a: a.astype(BF16)

    x1 = _ffn1(x.reshape(b * s, d), vec(ffn1_norm[0]), bf(ffn1_w_gate[0]), bf(ffn1_w_up[0]),
               bf(ffn1_w_down[0]))

    w_main, w_small = _split_w_in(w_in[0])
    cos_t, sin_a, sin_b = _rope_tables(s)
    qkv, z, small, gates_t, qp, qt, kvc, ks, vst, kw, vwt = _in_proj(
        x1.reshape(b, s, d), vec(mix_norm[0]), w_main, w_small, cos_t, sin_a, sin_b,
        pad_lanes(gdn_a_log[0]), pad_lanes(gdn_dt_bias[0]))

    o_gdn = _gdn(qkv, z, small, gdn_conv_w[0], vec(gdn_out_norm[0]))

    pe_top, pe_bot, w_top, w_bot, w2 = _compress_weights(
        cmp_pe_k[0], cmp_w1_k[0], cmp_w2_k[0], cmp_pe_v[0], cmp_w1_v[0], cmp_w2_v[0])
    kcvc = _compress(kvc, pe_top, pe_bot, w_top, w_bot, w2)

    o_cmp, sel_bias = _nsa_select(qp, kcvc, gates_t)
    o_nsa = _nsa_attention(qt, sel_bias, ks, vst, kw, vwt, gates_t, o_cmp).reshape(b * s, W_NSA_Q)

    out = _out_ffn2(x1, o_gdn.reshape(b * s, W_GDN_Z), o_nsa, bf(w_out[0]), vec(ffn2_norm[0]),
                    bf(ffn2_w_gate[0]), bf(ffn2_w_up[0]), bf(ffn2_w_down[0]), vec(final_norm))
    return out.reshape(b, s, d)
```

```python
import functools

import numpy as np
import jax
import jax.numpy as jnp
from jax import lax
from jax.experimental import pallas as pl
from jax.experimental.pallas import tpu as pltpu

F32 = jnp.float32
BF16 = jnp.bfloat16

D_MODEL = 1024
D_FF = 2816
GDN_HEADS = 4
GDN_DK = 128
GDN_DV = 128
GDN_CONV = 4
GDN_CHUNK = 64
NSA_HEADS = 8
NSA_KV = 2
NSA_HPG = NSA_HEADS // NSA_KV
NSA_DH = 64
CMP_BLOCK = 32
CMP_STRIDE = 16
CMP_HIDDEN = 128
SEL_BLOCK = 64
SEL_TOPK = 16
WINDOW = 512
ROPE_THETA = 500000.0
ROPE_DIM = NSA_DH // 4
EPS = 1e-6
NEG = -1e30
LOG2E = 1.4426950408889634
BIG = 1e30

W_GDN_QKV = GDN_HEADS * (2 * GDN_DK + GDN_DV)
W_GDN_Z = GDN_HEADS * GDN_DV
W_NSA_Q = NSA_HEADS * NSA_DH
W_NSA_KV = 2 * NSA_KV * NSA_DH
W_NSA_GATE = 3 * NSA_HEADS
W_MAIN = W_GDN_QKV + W_GDN_Z + W_NSA_Q + 3 * W_NSA_KV
LANES = 128
GATE_LANE0 = 2 * GDN_HEADS

TOKEN_TILE = 512
MXU_TILE = 256
FF_SPLITS = (0, 6 * MXU_TILE, D_FF)
GDN_HPS = 2
GDN_GROUP = 2 * GDN_CHUNK
NSA_TQ = 256
NSA_TS = 512
SLC_KT = 256
NSA_VT_ROWS = 80
NSA_GATE_ROWS = 32
VMEM_LIMIT = 56 * 1024 * 1024


def _dot(a, b):
    return jnp.dot(a, b, preferred_element_type=F32)


def _dot_nt(a, b):
    return lax.dot_general(a, b, (((1,), (1,)), ((), ())), preferred_element_type=F32)


def _dot_tn(a, b):
    return lax.dot_general(a, b, (((0,), (0,)), ((), ())), preferred_element_type=F32)


def _rms(x, w):
    return x * lax.rsqrt(jnp.mean(x * x, axis=-1, keepdims=True) + EPS) * w


def _sigmoid(x):
    return 0.5 * jnp.tanh(0.5 * x) + 0.5


def _silu(x):
    return x * _sigmoid(x)


def _swiglu_half_step(x, nw, wg_ref, wu_ref, wd_ref):
    hb = _rms(x, nw).astype(BF16)
    acc = jnp.zeros_like(x)
    for lo, hi in zip(FF_SPLITS[:-1], FF_SPLITS[1:]):
        sl = slice(lo, hi)
        g = _dot(hb, wg_ref[:, sl])
        u = _dot(hb, wu_ref[:, sl])
        acc = acc + _dot((_silu(g) * u).astype(BF16), wd_ref[sl, :])
    return x + 0.5 * acc


def _ffn_kernel(x_ref, nw_ref, wg_ref, wu_ref, wd_ref, o_ref):
    o_ref[...] = _swiglu_half_step(x_ref[...], nw_ref[...], wg_ref, wu_ref, wd_ref)


def _const_spec(shape):
    nd = len(shape)
    return pl.BlockSpec(shape, lambda *_: (0,) * nd, pipeline_mode=pl.Buffered(1))


def _ffn1(x2d, nw, wg, wu, wd):
    t = x2d.shape[0]
    row = pl.BlockSpec((TOKEN_TILE, D_MODEL), lambda i: (i, 0))
    return pl.pallas_call(
        _ffn_kernel,
        grid=(t // TOKEN_TILE,),
        in_specs=[row, _const_spec((1, D_MODEL)), _const_spec((D_MODEL, D_FF)),
                  _const_spec((D_MODEL, D_FF)), _const_spec((D_FF, D_MODEL))],
        out_specs=row,
        out_shape=jax.ShapeDtypeStruct((t, D_MODEL), F32),
        compiler_params=pltpu.CompilerParams(dimension_semantics=("arbitrary",),
                                             vmem_limit_bytes=VMEM_LIMIT),
        name="ffn1",
    )(x2d, nw, wg, wu, wd)


def _rope(x, cos_t, sin_a, sin_b):
    return (x * cos_t + pltpu.roll(x, LANES - ROPE_DIM // 2, axis=1) * sin_a
            + pltpu.roll(x, ROPE_DIM // 2, axis=1) * sin_b)


def _inproj_kernel(x_ref, xp_ref, nw_ref, wm_ref, ws_ref, cos_ref, sa_ref, sb_ref, alog_ref, dt_ref, conv_ref,
                   qkv_ref, z_ref, small_ref, gt_ref, qp_ref, qt_ref, kvc_ref,
                   ks_ref, vst_ref, kw_ref, vwt_ref, kvc_s):
    hb = _rms(x_ref[...], nw_ref[...]).astype(BF16)
    proj = lambda lo, width: _dot(hb, wm_ref[:, lo:lo + width])
    tm = x_ref.shape[0]
    lane = lax.broadcasted_iota(jnp.int32, (tm, LANES), 1)
    sm = _dot(hb, ws_ref[...])
    xg = sm + dt_ref[...]
    softplus = jnp.maximum(xg, 0.0) + jnp.log1p(jnp.exp(-jnp.abs(xg)))
    small = jnp.where(lane < GDN_HEADS, -jnp.exp(alog_ref[...]) * softplus, _sigmoid(sm))
    small_ref[...] = small
    gt_ref[...] = small.T[:gt_ref.shape[0], :]
    halo = xp_ref.shape[0]
    hb_prev = _rms(xp_ref[...], nw_ref[...]).astype(BF16)
    part_w = GDN_HEADS * GDN_DK
    for part in range(W_GDN_QKV // part_w):
        lo = part * part_w
        cur = proj(lo, part_w)
        prev = jnp.where(pl.program_id(1) == 0, 0.0, _dot(hb_prev, wm_ref[:, lo:lo + part_w]))
        ext = jnp.concatenate([prev, cur], axis=0)
        taps = conv_ref[:, lo:lo + part_w]
        conv = cur * taps[GDN_CONV - 1:GDN_CONV, :]
        for sh in range(1, GDN_CONV):
            conv = conv + ext[halo - sh:halo - sh + tm, :] * taps[GDN_CONV - 1 - sh:GDN_CONV - sh, :]
        conv = _silu(conv)
        for hh in range(GDN_HEADS):
            blk = conv[:, hh * GDN_DK:(hh + 1) * GDN_DK]
            if part < 2:
                blk = blk * lax.rsqrt(jnp.sum(blk * blk, axis=-1, keepdims=True) + EPS)
            if part == 0:
                blk = blk * (GDN_DK ** -0.5)
            qkv_ref[:, lo + hh * GDN_DK:lo + (hh + 1) * GDN_DK] = blk
    o = W_GDN_QKV
    z_ref[...] = _silu(proj(o, W_GDN_Z))
    o += W_GDN_Z
    cos_t, sin_a, sin_b = cos_ref[...], sa_ref[...], sb_ref[...]
    scale = NSA_DH ** -0.5
    pos = pl.program_id(1) * tm + lax.broadcasted_iota(jnp.int32, (tm, LANES), 0)
    low_half = lane < NSA_DH
    block_onehot = jnp.where(lane - NSA_DH == pos // SEL_BLOCK, 1.0, 0.0)

    def halves(slab):
        return slab, pltpu.roll(slab, NSA_DH, axis=1)

    y = proj(o, W_NSA_Q)
    for s in range(W_NSA_Q // LANES):
        slab = y[:, s * LANES:(s + 1) * LANES]
        plain_t = (slab * scale).T
        rot_t = (_rope(slab, cos_t, sin_a, sin_b) * (scale * LOG2E)).T
        for j in range(2):
            qp_ref[2 * s + j] = plain_t[j * NSA_DH:(j + 1) * NSA_DH, :].astype(BF16)
            qt_ref[2 * s + j] = rot_t[j * NSA_DH:(j + 1) * NSA_DH, :].astype(BF16)
    o += W_NSA_Q
    y = proj(o, W_NSA_KV)
    for half in range(W_NSA_KV // LANES):
        kvc_s[half] = y[:, half * LANES:(half + 1) * LANES]
    for r in range(CMP_STRIDE):
        for half in range(W_NSA_KV // LANES):
            dst = r * W_NSA_KV + half * LANES
            kvc_ref[:, dst:dst + LANES] = kvc_s[half, pl.ds(r, tm // CMP_STRIDE, stride=CMP_STRIDE), :]
    o += W_NSA_KV
    n_tail = vst_ref.shape[1] - NSA_DH
    tail = jnp.where(lax.broadcasted_iota(jnp.int32, (n_tail, tm), 0) == 0, 1.0, 0.0)
    for k_ref, vt_ref, k_pad in ((ks_ref, vst_ref, block_onehot), (kw_ref, vwt_ref, 0.0)):
        y = proj(o, W_NSA_KV)
        kr = halves(_rope(y[:, :LANES], cos_t, sin_a, sin_b))
        v_t = y[:, LANES:].T
        for j in range(NSA_KV):
            k_ref[j] = jnp.where(low_half, kr[j], k_pad).astype(BF16)
            vt_ref[j] = jnp.concatenate([v_t[j * NSA_DH:(j + 1) * NSA_DH, :], tail], axis=0).astype(BF16)
        o += W_NSA_KV


def _in_proj(x3d, nw, w_main, w_small, cos_t, sin_a, sin_b, alog_p, dt_p, conv_w):
    b, s, _ = x3d.shape
    tm = TOKEN_TILE
    row = lambda w: pl.BlockSpec((None, tm, w), lambda bi, i: (bi, i, 0))
    heads = lambda n, w: pl.BlockSpec((None, n, tm, w), lambda bi, i: (bi, 0, i, 0))
    heads_t = lambda n, r: pl.BlockSpec((None, n, r, tm), lambda bi, i: (bi, 0, 0, i))
    tab = pl.BlockSpec((tm, LANES), lambda bi, i: (i, 0))
    halo = 2 * GDN_CONV
    prev_rows = pl.BlockSpec((None, halo, D_MODEL), lambda bi, i: (bi, jnp.maximum(i * (tm // halo) - 1, 0), 0))
    bf_heads = lambda n, w: jax.ShapeDtypeStruct((b, n, s, w), BF16)
    out_shape = (
        jax.ShapeDtypeStruct((b, s, W_GDN_QKV), F32),
        jax.ShapeDtypeStruct((b, s, W_GDN_Z), F32),
        jax.ShapeDtypeStruct((b, s, LANES), F32),
        jax.ShapeDtypeStruct((b, NSA_GATE_ROWS, s), F32),
        jax.ShapeDtypeStruct((b, NSA_HEADS, NSA_DH, s), BF16),
        jax.ShapeDtypeStruct((b, NSA_HEADS, NSA_DH, s), BF16),
        jax.ShapeDtypeStruct((b, s // CMP_STRIDE, CMP_STRIDE * W_NSA_KV), F32),
        bf_heads(NSA_KV, LANES),
        jax.ShapeDtypeStruct((b, NSA_KV, NSA_VT_ROWS, s), BF16),
        bf_heads(NSA_KV, LANES),
        jax.ShapeDtypeStruct((b, NSA_KV, NSA_VT_ROWS, s), BF16),
    )
    return pl.pallas_call(
        _inproj_kernel,
        grid=(b, s // tm),
        in_specs=[row(D_MODEL), prev_rows, _const_spec((1, D_MODEL)), _const_spec((D_MODEL, W_MAIN)),
                  _const_spec((D_MODEL, LANES)), tab, tab, tab,
                  _const_spec((1, LANES)), _const_spec((1, LANES)), _const_spec((GDN_CONV, W_GDN_QKV))],
        out_specs=(row(W_GDN_QKV), row(W_GDN_Z), row(LANES),
                   pl.BlockSpec((None, NSA_GATE_ROWS, tm), lambda bi, i: (bi, 0, i)),
                   heads_t(NSA_HEADS, NSA_DH), heads_t(NSA_HEADS, NSA_DH),
                   pl.BlockSpec((None, tm // CMP_STRIDE, CMP_STRIDE * W_NSA_KV), lambda bi, i: (bi, i, 0)),
                   heads(NSA_KV, LANES), heads_t(NSA_KV, NSA_VT_ROWS),
                   heads(NSA_KV, LANES), heads_t(NSA_KV, NSA_VT_ROWS)),
        out_shape=out_shape,
        scratch_shapes=[pltpu.VMEM((W_NSA_KV // LANES, tm, LANES), F32)],
        compiler_params=pltpu.CompilerParams(dimension_semantics=("arbitrary", "arbitrary"),
                                             vmem_limit_bytes=VMEM_LIMIT),
        name="in_proj",
    )(x3d, x3d, nw, w_main, w_small, cos_t, sin_a, sin_b, alog_p, dt_p, conv_w)


def _bmm(a, b):
    return jnp.einsum('nij,njk->nik', a, b, preferred_element_type=F32)


def _bmm_nt(a, b):
    return jnp.einsum('nid,njd->nij', a, b, preferred_element_type=F32)


def _unit_lower_inverse_minus_eye(low, ii, jj):
    mm = lambda a, b: _bmm(a.astype(BF16), b.astype(BF16))
    same16 = (ii // 16) == (jj // 16)
    same32 = (ii // 32) == (jj // 32)
    ld = jnp.where(same16, low, 0.0)
    e = -ld
    p = mm(ld, ld)
    for level in range(3):
        e = e + p + mm(e, p)
        if level < 2:
            p = mm(p, p)
    for within in (same32 & jnp.logical_not(same16), jnp.logical_not(same32)):
        c = jnp.where(within, low, 0.0)
        y = c + mm(e, c)
        e = e - y - mm(y, e)
    return e


def _gdn_kernel(xq_ref, xk_ref, xv_ref, small_ref, z_ref,
                onorm_ref, o_ref,
                p_s, qn_s, e2_s, mpa_s, na_s, ea_s, qt_s, o0_s, seven_s):
    hp = pl.program_id(1)
    s_len = xq_ref.shape[0]
    c_len = GDN_CHUNK
    grp = GDN_GROUP
    n_chunks = s_len // c_len
    n_grp = s_len // grp
    hd = GDN_DK
    row = lax.broadcasted_iota(jnp.int32, (s_len, LANES), 0)
    lane = lax.broadcasted_iota(jnp.int32, (s_len, LANES), 1)
    pos = row % c_len
    ii = lax.broadcasted_iota(jnp.int32, (grp, grp), 0)
    jj = lax.broadcasted_iota(jnp.int32, (grp, grp), 1)
    same_chunk = (ii // c_len) == (jj // c_len)
    tril = same_chunk & (ii >= jj)
    strict = same_chunk & (ii > jj)
    first = lax.broadcasted_iota(jnp.int32, (grp, hd), 0) < c_len
    small = small_ref[...]
    g3 = lambda a: a.reshape(n_grp, grp, hd)

    for j in range(GDN_HPS):
        h = hp * GDN_HPS + j
        cols = slice(j * hd, (j + 1) * hd)
        q, k, v = xq_ref[:, cols], xk_ref[:, cols], xv_ref[:, cols]

        pick = lambda ln: jnp.broadcast_to(
            jnp.sum(jnp.where(lane == ln, small, 0.0), axis=-1, keepdims=True), (s_len, LANES))
        beta = pick(GDN_HEADS + h)
        g = pick(h)
        sh = 1
        while sh < c_len:
            g = g + jnp.where(pos >= sh, pltpu.roll(g, sh, axis=0), 0.0)
            sh *= 2
        gc = g
        gc3 = gc.reshape(n_chunks, c_len, LANES)
        gl3 = gc3[:, c_len - 1:c_len, :]
        e_gl = jnp.exp(gl3).reshape(n_grp, grp // c_len, 1, LANES)
        dec_a, dec_b = e_gl[:, 0], e_gl[:, 1]
        gl = jnp.broadcast_to(gl3, gc3.shape).reshape(s_len, LANES)
        eg = jnp.exp(gc)
        kb = k * beta
        vb = g3(v * beta)
        kbg = g3(kb * eg)

        k3 = g3(k).astype(BF16)
        kk = _bmm_nt(g3(kb).astype(BF16), k3)
        qk = _bmm_nt(g3(q).astype(BF16), k3)
        gcg = g3(gc)
        diff = gcg - jnp.swapaxes(gcg, 1, 2)
        decay = jnp.where(tril, jnp.exp(jnp.where(tril, diff, 0.0)), 0.0)
        e_b = _unit_lower_inverse_minus_eye(jnp.where(strict, kk * decay, 0.0), ii, jj).astype(BF16)
        u = vb + _bmm(e_b, vb.astype(BF16))
        w = kbg + _bmm(e_b, kbg.astype(BF16))
        qkm = jnp.where(tril, qk * decay, 0.0).astype(BF16)
        r = _bmm(qkm, jnp.concatenate([w, u], axis=-1).astype(BF16))
        qt_s[j] = (g3(q * eg) - r[..., :hd]).astype(BF16)
        o0_s[j] = r[..., hd:]
        kd_t = jnp.swapaxes(g3(k * jnp.exp(gl - gc)), 1, 2).astype(BF16)
        rhs = jnp.concatenate([jnp.where(first, w, 0.0), jnp.where(first, 0.0, w),
                               jnp.where(first, u, 0.0), jnp.where(first, 0.0, u)], axis=-1)
        mn = _bmm(kd_t, rhs.astype(BF16))
        mp_a, mp_b = -mn[..., :hd], -mn[..., hd:2 * hd]
        n_a, n_b = mn[..., 2 * hd:3 * hd], mn[..., 3 * hd:]
        x = _bmm(mp_b.astype(BF16), jnp.concatenate([mp_a, n_a], axis=-1).astype(BF16))
        p_s[j] = (dec_b * mp_a + dec_a * mp_b + x[..., :hd]).astype(BF16)
        qn_s[j] = dec_b * n_a + x[..., hd:] + n_b
        e2_s[j] = dec_a * dec_b
        mpa_s[j] = mp_a.astype(BF16)
        na_s[j] = n_a
        ea_s[j] = dec_a

    def scan(n, states):
        out = []
        for j in range(GDN_HPS):
            sb = states[j].astype(BF16)
            seven_s[j, n] = sb
            out.append(e2_s[j, n] * states[j] + _dot(p_s[j, n], sb) + qn_s[j, n])
        return tuple(out)

    lax.fori_loop(0, n_grp, scan, tuple(jnp.zeros((GDN_DK, GDN_DV), F32) for _ in range(GDN_HPS)))

    for j in range(GDN_HPS):
        cols = slice(j * hd, (j + 1) * hd)
        s_even = seven_s[j]
        s_odd = (ea_s[j] * s_even.astype(F32) + _bmm(mpa_s[j], s_even) + na_s[j]).astype(BF16)
        qt = qt_s[j]
        o = jnp.concatenate([_bmm(qt[:, :c_len], s_even), _bmm(qt[:, c_len:], s_odd)], axis=1) + o0_s[j]
        o_ref[:, cols] = _rms(o.reshape(s_len, hd), onorm_ref[...]) * z_ref[:, cols]


def _gdn(qkv, z, small, onorm):
    b, s, _ = qkv.shape
    wid = GDN_HPS * GDN_DK
    n_hp = GDN_HEADS // GDN_HPS
    n_grp = s // GDN_GROUP
    per_pair = lambda rows, dt: pltpu.VMEM((GDN_HPS, n_grp, rows, GDN_DV), dt)
    col = lambda off: pl.BlockSpec((None, s, wid), lambda bi, h: (bi, 0, off + h))
    vec = pl.BlockSpec((1, LANES), lambda bi, h: (0, 0))
    return pl.pallas_call(
        _gdn_kernel,
        grid=(b, n_hp),
        in_specs=[col(0), col(n_hp), col(2 * n_hp),
                  pl.BlockSpec((None, s, LANES), lambda bi, h: (bi, 0, 0)),
                  col(0), vec],
        out_specs=col(0),
        out_shape=jax.ShapeDtypeStruct((b, s, GDN_HEADS * GDN_DV), F32),
        scratch_shapes=[per_pair(GDN_DK, BF16), per_pair(GDN_DK, F32), per_pair(1, F32),
                        per_pair(GDN_DK, BF16), per_pair(GDN_DK, F32), per_pair(1, F32),
                        per_pair(GDN_GROUP, BF16), per_pair(GDN_GROUP, F32), per_pair(GDN_DK, BF16)],
        compiler_params=pltpu.CompilerParams(dimension_semantics=("arbitrary", "arbitrary"),
                                             vmem_limit_bytes=VMEM_LIMIT),
        name="gdn",
    )(qkv, qkv, qkv, small, z, onorm)


def _compress_kernel(t_ref, pet_ref, peb_ref, wt_ref, wb_ref, w2_ref, o_ref):
    t = t_ref[...]
    a = _dot((t + pet_ref[...]).astype(BF16), wt_ref[...])
    bm = _dot((t + peb_ref[...]).astype(BF16), wb_ref[...])
    n_rows = t.shape[0]
    hid = a + pltpu.roll(bm, n_rows - 1, axis=0)
    out = _dot(_silu(hid).astype(BF16), w2_ref[...])
    for j in range(2 * NSA_KV):
        blk = out[:, j * LANES:(j + 1) * LANES]
        o_ref[j] = (blk if j < NSA_KV else blk.T).astype(o_ref.dtype)


def _compress(tm, pe_top, pe_bot, w_top, w_bot, w2):
    b, n_rows, width = tm.shape
    nh = 2 * NSA_KV * CMP_HIDDEN
    return pl.pallas_call(
        _compress_kernel,
        grid=(b,),
        in_specs=[pl.BlockSpec((None, n_rows, width), lambda bi: (bi, 0, 0)),
                  _const_spec((1, width)), _const_spec((1, width)),
                  _const_spec((width, nh)), _const_spec((width, nh)),
                  _const_spec((nh, 2 * NSA_KV * LANES))],
        out_specs=pl.BlockSpec((None, 2 * NSA_KV, n_rows, LANES), lambda bi: (bi, 0, 0, 0)),
        out_shape=jax.ShapeDtypeStruct((b, 2 * NSA_KV, n_rows, LANES), BF16),
        compiler_params=pltpu.CompilerParams(dimension_semantics=("arbitrary",),
                                             vmem_limit_bytes=VMEM_LIMIT),
        name="compress",
    )(tm, pe_top, pe_bot, w_top, w_bot, w2)


def _nsa_select_kernel(qp_ref, kc_ref, vct_ref, gt_ref, ocmp_ref, bias_ref):
    grp = pl.program_id(1)
    ts, hpg, dh = qp_ref.shape[2], NSA_HPG, NSA_DH
    q0 = pl.program_id(2) * ts
    n_cmp = kc_ref.shape[0]
    n_blk = n_cmp * CMP_STRIDE // SEL_BLOCK
    per_head = lambda a: jnp.concatenate([a] * hpg, axis=1)

    t_c = q0 + lax.broadcasted_iota(jnp.int32, (n_cmp, ts), 1)
    n_c = lax.broadcasted_iota(jnp.int32, (n_cmp, ts), 0)
    vis = per_head(jnp.where(n_c * CMP_STRIDE + (CMP_BLOCK - 1) <= t_c, 1.0, 0.0))
    pad = jnp.zeros((LANES - dh, ts), BF16)
    q_t = jnp.concatenate([jnp.concatenate([qp_ref[hh], pad], axis=0) for hh in range(hpg)], axis=1)
    s_cmp = jnp.where(vis > 0.5, _dot(kc_ref[...], q_t), NEG)
    p_cmp = jnp.exp(s_cmp - jnp.max(s_cmp, axis=0, keepdims=True)) * vis
    p_cmp = p_cmp * (1.0 / jnp.maximum(jnp.sum(p_cmp, axis=0, keepdims=True), 1e-30))
    o_cmp = _dot(vct_ref[...], p_cmp.astype(BF16))
    ocmp_ref[...] = jnp.concatenate(
        [gt_ref[pl.ds(GATE_LANE0 + (grp * hpg + hh) * 3, 1), :] * o_cmp[:dh, hh * ts:(hh + 1) * ts]
         for hh in range(hpg)], axis=0)

    p_sum = p_cmp[:, :ts]
    for hh in range(1, hpg):
        p_sum = p_sum + p_cmp[:, hh * ts:(hh + 1) * ts]
    oj = lax.broadcasted_iota(jnp.int32, (n_blk, n_cmp), 0) * SEL_BLOCK
    on = lax.broadcasted_iota(jnp.int32, (n_blk, n_cmp), 1) * CMP_STRIDE
    overlap_t = jnp.where(on < oj + SEL_BLOCK, jnp.where(on + CMP_BLOCK > oj, 1.0, 0.0), 0.0)
    imp_t = jnp.dot(overlap_t, p_sum, precision=lax.Precision.HIGHEST,
                    preferred_element_type=F32)
    blk = lax.broadcasted_iota(jnp.int32, (n_blk, ts), 0)
    ahead = (q0 + lax.broadcasted_iota(jnp.int32, (n_blk, ts), 1)) // SEL_BLOCK - blk
    score = jnp.where(ahead < 0, NEG, jnp.where(blk == 0, BIG, jnp.where(ahead <= 1, BIG, imp_t)))
    rank = jnp.zeros((n_blk, ts), F32)
    for i in range(n_blk):
        ci = jnp.broadcast_to(score[i:i + 1, :], (n_blk, ts))
        tie = jnp.where(blk > i, 1.0, 0.0)
        rank = rank + jnp.where(ci > score, 1.0, jnp.where(ci == score, tie, 0.0))
    bias_ref[...] = jnp.where(rank >= SEL_TOPK, NEG, 0.0).astype(bias_ref.dtype)


def _nsa_select(qp_t, kcvc, gates_t):
    b, _, _, s = qp_t.shape
    ts = NSA_TS
    n_cmp = kcvc.shape[2]
    cspec = lambda off: pl.BlockSpec((None, None, n_cmp, LANES), lambda bi, g, i: (bi, off + g, 0, 0))
    return pl.pallas_call(
        _nsa_select_kernel,
        grid=(b, NSA_KV, s // ts),
        in_specs=[pl.BlockSpec((None, NSA_HPG, NSA_DH, ts), lambda bi, g, i: (bi, g, 0, i)),
                  cspec(0), cspec(NSA_KV),
                  pl.BlockSpec((None, NSA_GATE_ROWS, ts), lambda bi, g, i: (bi, 0, i))],
        out_specs=(pl.BlockSpec((None, None, NSA_HPG * NSA_DH, ts), lambda bi, g, i: (bi, g, 0, i)),
                   pl.BlockSpec((None, None, s // SEL_BLOCK, ts), lambda bi, g, i: (bi, g, 0, i))),
        out_shape=(jax.ShapeDtypeStruct((b, NSA_KV, NSA_HPG * NSA_DH, s), F32),
                   jax.ShapeDtypeStruct((b, NSA_KV, s // SEL_BLOCK, s), BF16)),
        compiler_params=pltpu.CompilerParams(
            dimension_semantics=("arbitrary", "arbitrary", "arbitrary"),
            vmem_limit_bytes=VMEM_LIMIT),
        name="nsa_select",
    )(qp_t, kcvc, kcvc, gates_t)


def _nsa_kernel(qt_ref, bias_ref, ks_ref, vst_ref, kw_ref, vwt_ref, gt_ref, ocmp_ref, o_ref,
                *score_bufs):
    grp = pl.program_id(1)
    q0 = pl.program_id(2) * NSA_TQ
    tq, hpg, dh = NSA_TQ, NSA_HPG, NSA_DH
    bias = bias_ref[...]
    pad = jnp.zeros((LANES - dh - bias.shape[0], tq), BF16)
    q_aug = jnp.concatenate(
        [jnp.concatenate([qt_ref[hh], bias, pad], axis=0) for hh in range(hpg)], axis=1)
    per_head = lambda a: jnp.concatenate([a] * hpg, axis=1)

    def attend(*streams):
        tasks = [(i, chunk) for i, chunks in enumerate(streams) for chunk in chunks]

        def scores(n):
            k, _, mask = tasks[n][1]
            s_t = _dot(k, q_aug)
            score_bufs[n % 2][:k.shape[0], :] = s_t if mask is None else s_t + mask

        state = [(None, None)] * len(streams)
        scores(0)
        for n, (i, (k, v_t, _)) in enumerate(tasks):
            if n + 1 < len(tasks):
                scores(n + 1)
            s_t = score_bufs[n % 2][:k.shape[0], :]
            m, acc = state[i]
            m_new = jnp.max(s_t, axis=0, keepdims=True)
            if m is not None:
                m_new = jnp.maximum(m, m_new)
            part = _dot(v_t, jnp.exp2(s_t - m_new).astype(BF16))
            state[i] = (m_new, part if m is None else acc * jnp.exp2(m - m_new) + part)
        return [acc[:dh, :] * (1.0 / jnp.maximum(acc[dh:dh + 1, :], 1e-30)) for _, acc in state]

    kt = SLC_KT
    last = (q0 + tq - 1) // kt
    w0 = pl.multiple_of(jnp.maximum(q0 - WINDOW, 0), tq)
    row_k = lax.broadcasted_iota(jnp.int32, (kt, tq), 0)
    t_q = q0 + lax.broadcasted_iota(jnp.int32, (kt, tq), 1)

    def step(n_chunks):
        slc = []
        for c in range(n_chunks):
            rows = slice(c * kt, (c + 1) * kt)
            causal = per_head(jnp.where(c * kt + row_k <= t_q, 0.0, NEG)) if c == n_chunks - 1 else None
            slc.append((ks_ref[rows, :], vst_ref[:, rows], causal))
        win = []
        for off, n in ((0, WINDOW + tq - kt), (WINDOW + tq - kt, kt)):
            back = (q0 + lax.broadcasted_iota(jnp.int32, (n, tq), 1)
                    - (w0 + off + lax.broadcasted_iota(jnp.int32, (n, tq), 0)))
            band = per_head(jnp.where(back < 0, NEG, jnp.where(back < WINDOW, 0.0, NEG)))
            keys = pl.ds(pl.multiple_of(w0 + off, tq), n)
            win.append((kw_ref[keys, :], vwt_ref[:, keys], band))
        o_win, o_slc = attend(win, slc)
        mixed = []
        for hh in range(hpg):
            base = GATE_LANE0 + (grp * hpg + hh) * 3
            head = slice(hh * tq, (hh + 1) * tq)
            mixed.append(gt_ref[pl.ds(base + 1, 1), :] * o_slc[:, head]
                         + gt_ref[pl.ds(base + 2, 1), :] * o_win[:, head])
        o_ref[...] = (ocmp_ref[...] + jnp.concatenate(mixed, axis=0)).T.astype(o_ref.dtype)

    for n_chunks in range(1, ks_ref.shape[0] // kt + 1):
        pl.when(last == n_chunks - 1)(functools.partial(step, n_chunks))


def _nsa_attention(qt, sel_bias, ks, vst, kw, vwt, gates_t, o_cmp):
    b, _, _, s = qt.shape
    tq = NSA_TQ
    kspec = pl.BlockSpec((None, None, s, LANES), lambda bi, g, i: (bi, g, 0, 0))
    vspec = pl.BlockSpec((None, None, NSA_VT_ROWS, s), lambda bi, g, i: (bi, g, 0, 0))
    tok = pl.BlockSpec((None, tq, NSA_HPG * NSA_DH), lambda bi, g, i: (bi, i, g))
    return pl.pallas_call(
        _nsa_kernel,
        grid=(b, NSA_KV, s // tq),
        in_specs=[pl.BlockSpec((None, NSA_HPG, NSA_DH, tq), lambda bi, g, i: (bi, g, 0, i)),
                  pl.BlockSpec((None, None, s // SEL_BLOCK, tq), lambda bi, g, i: (bi, g, 0, i)),
                  kspec, vspec, kspec, vspec,
                  pl.BlockSpec((None, NSA_GATE_ROWS, tq), lambda bi, g, i: (bi, 0, i)),
                  pl.BlockSpec((None, None, NSA_HPG * NSA_DH, tq), lambda bi, g, i: (bi, g, 0, i))],
        out_specs=tok,
        out_shape=jax.ShapeDtypeStruct((b, s, W_NSA_Q), BF16),
        scratch_shapes=[pltpu.VMEM((WINDOW + tq - SLC_KT, NSA_HPG * tq), F32)] * 2,
        compiler_params=pltpu.CompilerParams(
            dimension_semantics=("arbitrary", "arbitrary", "arbitrary"),
            vmem_limit_bytes=VMEM_LIMIT),
        name="nsa_attn",
    )(qt, sel_bias, ks, vst, kw, vwt, gates_t, o_cmp)


def _out_ffn2_kernel(x_ref, og_ref, on_ref, wo_ref, nw_ref, wg_ref, wu_ref, wd_ref, fw_ref, o_ref):
    half = W_GDN_Z
    x = (x_ref[...] + _dot(og_ref[...].astype(BF16), wo_ref[:half, :])
         + _dot(on_ref[...], wo_ref[half:, :]))
    y = _swiglu_half_step(x, nw_ref[...], wg_ref, wu_ref, wd_ref)
    o_ref[...] = _rms(y, fw_ref[...])


def _out_ffn2(x2d, o_gdn, o_nsa, w_out, nw, wg, wu, wd, fw):
    t = x2d.shape[0]
    row = lambda w: pl.BlockSpec((TOKEN_TILE, w), lambda i: (i, 0))
    return pl.pallas_call(
        _out_ffn2_kernel,
        grid=(t // TOKEN_TILE,),
        in_specs=[row(D_MODEL), row(W_GDN_Z), row(W_NSA_Q), _const_spec((D_MODEL, D_MODEL)),
                  _const_spec((1, D_MODEL)), _const_spec((D_MODEL, D_FF)),
                  _const_spec((D_MODEL, D_FF)), _const_spec((D_FF, D_MODEL)),
                  _const_spec((1, D_MODEL))],
        out_specs=row(D_MODEL),
        out_shape=jax.ShapeDtypeStruct((t, D_MODEL), F32),
        compiler_params=pltpu.CompilerParams(dimension_semantics=("arbitrary",),
                                             vmem_limit_bytes=VMEM_LIMIT),
        name="out_ffn2",
    )(x2d, o_gdn, o_nsa, w_out, nw, wg, wu, wd, fw)


def _rope_tables(s_len):
    half = ROPE_DIM // 2
    inv = ROPE_THETA ** (-jnp.arange(half, dtype=F32) / half)
    ang = jnp.arange(s_len, dtype=F32)[:, None] * inv[None, :]
    cos, sin = jnp.cos(ang), jnp.sin(ang)
    pad = jnp.zeros((s_len, NSA_DH - ROPE_DIM), F32)
    zero = jnp.zeros_like(sin)
    cos_h = jnp.concatenate([cos, cos, pad + 1.0], axis=1)
    sa_h = jnp.concatenate([-sin, zero, pad], axis=1)
    sb_h = jnp.concatenate([zero, sin, pad], axis=1)
    rep = lambda a: jnp.tile(a, (1, LANES // NSA_DH))
    return rep(cos_h), rep(sa_h), rep(sb_h)


def _split_w_in(w_in):
    cuts = np.cumsum([W_GDN_QKV, W_GDN_Z, GDN_HEADS, GDN_HEADS, W_NSA_Q,
                      W_NSA_KV, W_NSA_KV, W_NSA_KV]).tolist()
    qkv, z, a, bb, nq, kvc, kvs, kvw, gate = jnp.split(w_in, cuts, axis=1)
    w_main = jnp.concatenate([qkv, z, nq, kvc, kvs, kvw], axis=1).astype(BF16)
    small = jnp.concatenate([a, bb, gate], axis=1)
    w_small = jnp.pad(small, ((0, 0), (0, LANES - small.shape[1]))).astype(BF16)
    return w_main, w_small


def _compress_weights(pe_k, w1_k, w2_k, pe_v, w1_v, w2_v):
    half = CMP_BLOCK // 2
    eye = jnp.eye(2 * NSA_KV, dtype=F32)

    def first_layer(lo):
        w1 = jnp.stack([w1_k, w1_v]).reshape(2, CMP_BLOCK, NSA_DH, CMP_HIDDEN)[:, lo:lo + half]
        w1 = jnp.repeat(w1, NSA_KV, axis=0)
        w = jnp.einsum('crdj,ce->rcdej', w1, eye)
        return w.reshape(half * 2 * NSA_KV * NSA_DH, 2 * NSA_KV * CMP_HIDDEN).astype(BF16)

    def pe_row(lo):
        pe = jnp.stack([pe_k, pe_v])[:, lo:lo + half]
        pe = jnp.repeat(pe, NSA_KV, axis=0)
        return pe.transpose(1, 0, 2).reshape(1, half * 2 * NSA_KV * NSA_DH)

    w2 = jnp.repeat(jnp.stack([w2_k, w2_v]), NSA_KV, axis=0)
    w2 = jnp.pad(w2, ((0, 0), (0, 0), (0, LANES - NSA_DH)))
    w2 = jnp.einsum('cjd,ce->cjed', w2, eye).reshape(2 * NSA_KV * CMP_HIDDEN, 2 * NSA_KV * LANES)
    return pe_row(0), pe_row(half), first_layer(0), first_layer(half), w2.astype(BF16)


def kernel(x, ffn1_norm, ffn1_w_gate, ffn1_w_up, ffn1_w_down, mix_norm, w_in, gdn_conv_w, gdn_a_log,
           gdn_dt_bias, gdn_out_norm, cmp_pe_k, cmp_w1_k, cmp_w2_k, cmp_pe_v, cmp_w1_v, cmp_w2_v,
           w_out, ffn2_norm, ffn2_w_gate, ffn2_w_up, ffn2_w_down, final_norm):
    b, s, d = x.shape
    assert d == D_MODEL and s % TOKEN_TILE == 0 and ffn1_norm.shape[0] == 1
    vec = lambda a: a.reshape(1, -1)
    pad_lanes = lambda a: jnp.pad(a.reshape(1, -1), ((0, 0), (0, LANES - a.size)))
    bf = lambda a: a.astype(BF16)

    x1 = _ffn1(x.reshape(b * s, d), vec(ffn1_norm[0]), bf(ffn1_w_gate[0]), bf(ffn1_w_up[0]),
               bf(ffn1_w_down[0]))

    w_main, w_small = _split_w_in(w_in[0])
    cos_t, sin_a, sin_b = _rope_tables(s)
    qkv, z, small, gates_t, qp, qt, kvc, ks, vst, kw, vwt = _in_proj(
        x1.reshape(b, s, d), vec(mix_norm[0]), w_main, w_small, cos_t, sin_a, sin_b,
        pad_lanes(gdn_a_log[0]), pad_lanes(gdn_dt_bias[0]), gdn_conv_w[0])

    o_gdn = _gdn(qkv, z, small, vec(gdn_out_norm[0]))

    pe_top, pe_bot, w_top, w_bot, w2 = _compress_weights(
        cmp_pe_k[0], cmp_w1_k[0], cmp_w2_k[0], cmp_pe_v[0], cmp_w1_v[0], cmp_w2_v[0])
    kcvc = _compress(kvc, pe_top, pe_bot, w_top, w_bot, w2)

    o_cmp, sel_bias = _nsa_select(qp, kcvc, gates_t)
    o_nsa = _nsa_attention(qt, sel_bias, ks, vst, kw, vwt, gates_t, o_cmp).reshape(b * s, W_NSA_Q)

    out = _out_ffn2(x1, o_gdn.reshape(b * s, W_GDN_Z), o_nsa, bf(w_out[0]), vec(ffn2_norm[0]),
                    bf(ffn2_w_gate[0]), bf(ffn2_w_up[0]), bf(ffn2_w_down[0]), vec(final_norm))
    return out.reshape(b, s, d)
```

```python
import functools

import numpy as np
import jax
import jax.numpy as jnp
from jax import lax
from jax.experimental import pallas as pl
from jax.experimental.pallas import tpu as pltpu

F32 = jnp.float32
BF16 = jnp.bfloat16

D_MODEL = 1024
D_FF = 2816
GDN_HEADS = 4
GDN_DK = 128
GDN_DV = 128
GDN_CONV = 4
GDN_CHUNK = 64
NSA_HEADS = 8
NSA_KV = 2
NSA_HPG = NSA_HEADS // NSA_KV
NSA_DH = 64
CMP_BLOCK = 32
CMP_STRIDE = 16
CMP_HIDDEN = 128
SEL_BLOCK = 64
SEL_TOPK = 16
WINDOW = 512
ROPE_THETA = 500000.0
ROPE_DIM = NSA_DH // 4
EPS = 1e-6
NEG = -1e30
LOG2E = 1.4426950408889634
BIG = 1e30

W_GDN_QKV = GDN_HEADS * (2 * GDN_DK + GDN_DV)
W_GDN_Z = GDN_HEADS * GDN_DV
W_NSA_Q = NSA_HEADS * NSA_DH
W_NSA_KV = 2 * NSA_KV * NSA_DH
W_NSA_GATE = 3 * NSA_HEADS
W_MAIN = W_GDN_QKV + W_GDN_Z + W_NSA_Q + 3 * W_NSA_KV
LANES = 128
GATE_LANE0 = 2 * GDN_HEADS

TOKEN_TILE = 512
MXU_TILE = 256
FF_SPLITS = (0, 6 * MXU_TILE, D_FF)
GDN_HPS = 2
GDN_GROUP = 2 * GDN_CHUNK
NSA_TQ = 256
NSA_TS = 512
SLC_KT = 256
NSA_VT_ROWS = 80
NSA_GATE_ROWS = 32
VMEM_LIMIT = 56 * 1024 * 1024


def _dot(a, b):
    return jnp.dot(a, b, preferred_element_type=F32)


def _rms(x, w):
    return x * lax.rsqrt(jnp.mean(x * x, axis=-1, keepdims=True) + EPS) * w


def _sigmoid(x):
    return 0.5 * jnp.tanh(0.5 * x) + 0.5


def _silu(x):
    return x * _sigmoid(x)


def _swiglu_half_step(x, nw, wg_ref, wu_ref, wd_ref):
    hb = _rms(x, nw).astype(BF16)
    acc = jnp.zeros_like(x)
    for lo, hi in zip(FF_SPLITS[:-1], FF_SPLITS[1:]):
        sl = slice(lo, hi)
        g = _dot(hb, wg_ref[:, sl])
        u = _dot(hb, wu_ref[:, sl])
        acc = acc + _dot((_silu(g) * u).astype(BF16), wd_ref[sl, :])
    return x + 0.5 * acc


def _ffn_kernel(x_ref, nw_ref, wg_ref, wu_ref, wd_ref, o_ref):
    o_ref[...] = _swiglu_half_step(x_ref[...], nw_ref[...], wg_ref, wu_ref, wd_ref)


def _const_spec(shape):
    nd = len(shape)
    return pl.BlockSpec(shape, lambda *_: (0,) * nd, pipeline_mode=pl.Buffered(1))


def _ffn1(x2d, nw, wg, wu, wd):
    t = x2d.shape[0]
    row = pl.BlockSpec((TOKEN_TILE, D_MODEL), lambda i: (i, 0))
    return pl.pallas_call(
        _ffn_kernel,
        grid=(t // TOKEN_TILE,),
        in_specs=[row, _const_spec((1, D_MODEL)), _const_spec((D_MODEL, D_FF)),
                  _const_spec((D_MODEL, D_FF)), _const_spec((D_FF, D_MODEL))],
        out_specs=row,
        out_shape=jax.ShapeDtypeStruct((t, D_MODEL), F32),
        compiler_params=pltpu.CompilerParams(dimension_semantics=("arbitrary",),
                                             vmem_limit_bytes=VMEM_LIMIT),
        name="ffn1",
    )(x2d, nw, wg, wu, wd)


def _rope(x, cos_t, sin_a, sin_b):
    return (x * cos_t + pltpu.roll(x, LANES - ROPE_DIM // 2, axis=1) * sin_a
            + pltpu.roll(x, ROPE_DIM // 2, axis=1) * sin_b)


def _inproj_kernel(x_ref, xp_ref, nw_ref, wm_ref, ws_ref, cos_ref, sa_ref, sb_ref, alog_ref, dt_ref, conv_ref,
                   qkv_ref, z_ref, small_ref, gt_ref, qp_ref, qt_ref, kvc_ref,
                   ks_ref, vst_ref, kw_ref, vwt_ref, kvc_s):
    hb = _rms(x_ref[...], nw_ref[...]).astype(BF16)
    hb_prev = _rms(xp_ref[...], nw_ref[...]).astype(BF16)
    tm, halo = x_ref.shape[0], xp_ref.shape[0]
    first_tile = pl.program_id(1) == 0
    lane = lax.broadcasted_iota(jnp.int32, (tm, LANES), 1)
    pos = pl.program_id(1) * tm + lax.broadcasted_iota(jnp.int32, (tm, LANES), 0)
    low_half = lane < NSA_DH
    cos_t, sin_a, sin_b = cos_ref[...], sa_ref[...], sb_ref[...]
    scale = NSA_DH ** -0.5
    part_w = GDN_HEADS * GDN_DK

    def halves(slab):
        return slab, pltpu.roll(slab, NSA_DH, axis=1)

    def gdn_part(part, cur):
        lo = part * part_w
        prev = jnp.where(first_tile, 0.0, _dot(hb_prev, wm_ref[:, lo:lo + part_w]))
        ext = jnp.concatenate([prev, cur], axis=0)
        taps = conv_ref[:, lo:lo + part_w]
        conv = cur * taps[GDN_CONV - 1:GDN_CONV, :]
        for sh in range(1, GDN_CONV):
            conv = conv + ext[halo - sh:halo - sh + tm, :] * taps[GDN_CONV - 1 - sh:GDN_CONV - sh, :]
        conv = _silu(conv)
        for hh in range(GDN_HEADS):
            blk = conv[:, hh * GDN_DK:(hh + 1) * GDN_DK]
            if part < 2:
                blk = blk * lax.rsqrt(jnp.sum(blk * blk, axis=-1, keepdims=True) + EPS)
            if part == 0:
                blk = blk * (GDN_DK ** -0.5)
            qkv_ref[:, lo + hh * GDN_DK:lo + (hh + 1) * GDN_DK] = blk

    def gate_z(y):
        z_ref[...] = _silu(y)

    def nsa_q(y):
        for s in range(W_NSA_Q // LANES):
            slab = y[:, s * LANES:(s + 1) * LANES]
            plain_t = (slab * scale).T
            rot_t = (_rope(slab, cos_t, sin_a, sin_b) * (scale * LOG2E)).T
            for j in range(2):
                qp_ref[2 * s + j] = plain_t[j * NSA_DH:(j + 1) * NSA_DH, :].astype(BF16)
                qt_ref[2 * s + j] = rot_t[j * NSA_DH:(j + 1) * NSA_DH, :].astype(BF16)

    def cmp_kv(y):
        for half in range(W_NSA_KV // LANES):
            kvc_s[half] = y[:, half * LANES:(half + 1) * LANES]
        for r in range(CMP_STRIDE):
            for half in range(W_NSA_KV // LANES):
                dst = r * W_NSA_KV + half * LANES
                kvc_ref[:, dst:dst + LANES] = kvc_s[half, pl.ds(r, tm // CMP_STRIDE, stride=CMP_STRIDE), :]

    def attn_kv(k_ref, vt_ref, selected, y):
        k_pad = jnp.where(lane - NSA_DH == pos // SEL_BLOCK, 1.0, 0.0) if selected else 0.0
        n_tail = vt_ref.shape[1] - NSA_DH
        tail = jnp.where(lax.broadcasted_iota(jnp.int32, (n_tail, tm), 0) == 0, 1.0, 0.0)
        kr = halves(_rope(y[:, :LANES], cos_t, sin_a, sin_b))
        v_t = y[:, LANES:].T
        for j in range(NSA_KV):
            k_ref[j] = jnp.where(low_half, kr[j], k_pad).astype(BF16)
            vt_ref[j] = jnp.concatenate([v_t[j * NSA_DH:(j + 1) * NSA_DH, :], tail], axis=0).astype(BF16)

    def small_proj(sm):
        xg = sm + dt_ref[...]
        softplus = jnp.maximum(xg, 0.0) + jnp.log1p(jnp.exp(-jnp.abs(xg)))
        small = jnp.where(lane < GDN_HEADS, -jnp.exp(alog_ref[...]) * softplus, _sigmoid(sm))
        small_ref[...] = small
        gt_ref[...] = small.T[:gt_ref.shape[0], :]

    stages = [(ws_ref, 0, LANES, small_proj)]
    stages += [(wm_ref, part * part_w, part_w, functools.partial(gdn_part, part))
               for part in range(W_GDN_QKV // part_w)]
    z_lo = W_GDN_QKV
    q_lo = z_lo + W_GDN_Z
    kvc_lo = q_lo + W_NSA_Q
    stages += [(wm_ref, q_lo, W_NSA_Q, nsa_q),
               (wm_ref, kvc_lo + W_NSA_KV, W_NSA_KV, functools.partial(attn_kv, ks_ref, vst_ref, True)),
               (wm_ref, kvc_lo + 2 * W_NSA_KV, W_NSA_KV, functools.partial(attn_kv, kw_ref, vwt_ref, False)),
               (wm_ref, z_lo, W_GDN_Z, gate_z),
               (wm_ref, kvc_lo, W_NSA_KV, cmp_kv)]
    for w_ref, lo, width, epilogue in stages:
        epilogue(_dot(hb, w_ref[:, lo:lo + width]))


def _in_proj(x3d, nw, w_main, w_small, cos_t, sin_a, sin_b, alog_p, dt_p, conv_w):
    b, s, _ = x3d.shape
    tm = TOKEN_TILE
    row = lambda w: pl.BlockSpec((None, tm, w), lambda bi, i: (bi, i, 0))
    heads = lambda n, w: pl.BlockSpec((None, n, tm, w), lambda bi, i: (bi, 0, i, 0))
    heads_t = lambda n, r: pl.BlockSpec((None, n, r, tm), lambda bi, i: (bi, 0, 0, i))
    tab = pl.BlockSpec((tm, LANES), lambda bi, i: (i, 0))
    halo = 2 * GDN_CONV
    prev_rows = pl.BlockSpec((None, halo, D_MODEL), lambda bi, i: (bi, jnp.maximum(i * (tm // halo) - 1, 0), 0))
    bf_heads = lambda n, w: jax.ShapeDtypeStruct((b, n, s, w), BF16)
    out_shape = (
        jax.ShapeDtypeStruct((b, s, W_GDN_QKV), F32),
        jax.ShapeDtypeStruct((b, s, W_GDN_Z), F32),
        jax.ShapeDtypeStruct((b, s, LANES), F32),
        jax.ShapeDtypeStruct((b, NSA_GATE_ROWS, s), F32),
        jax.ShapeDtypeStruct((b, NSA_HEADS, NSA_DH, s), BF16),
        jax.ShapeDtypeStruct((b, NSA_HEADS, NSA_DH, s), BF16),
        jax.ShapeDtypeStruct((b, s // CMP_STRIDE, CMP_STRIDE * W_NSA_KV), F32),
        bf_heads(NSA_KV, LANES),
        jax.ShapeDtypeStruct((b, NSA_KV, NSA_VT_ROWS, s), BF16),
        bf_heads(NSA_KV, LANES),
        jax.ShapeDtypeStruct((b, NSA_KV, NSA_VT_ROWS, s), BF16),
    )
    return pl.pallas_call(
        _inproj_kernel,
        grid=(b, s // tm),
        in_specs=[row(D_MODEL), prev_rows, _const_spec((1, D_MODEL)), _const_spec((D_MODEL, W_MAIN)),
                  _const_spec((D_MODEL, LANES)), tab, tab, tab,
                  _const_spec((1, LANES)), _const_spec((1, LANES)), _const_spec((GDN_CONV, W_GDN_QKV))],
        out_specs=(row(W_GDN_QKV), row(W_GDN_Z), row(LANES),
                   pl.BlockSpec((None, NSA_GATE_ROWS, tm), lambda bi, i: (bi, 0, i)),
                   heads_t(NSA_HEADS, NSA_DH), heads_t(NSA_HEADS, NSA_DH),
                   pl.BlockSpec((None, tm // CMP_STRIDE, CMP_STRIDE * W_NSA_KV), lambda bi, i: (bi, i, 0)),
                   heads(NSA_KV, LANES), heads_t(NSA_KV, NSA_VT_ROWS),
                   heads(NSA_KV, LANES), heads_t(NSA_KV, NSA_VT_ROWS)),
        out_shape=out_shape,
        scratch_shapes=[pltpu.VMEM((W_NSA_KV // LANES, tm, LANES), F32)],
        compiler_params=pltpu.CompilerParams(dimension_semantics=("arbitrary", "arbitrary"),
                                             vmem_limit_bytes=VMEM_LIMIT),
        name="in_proj",
    )(x3d, x3d, nw, w_main, w_small, cos_t, sin_a, sin_b, alog_p, dt_p, conv_w)


def _bmm(a, b):
    return jnp.einsum('nij,njk->nik', a, b, preferred_element_type=F32)


def _bmm_nt(a, b):
    return jnp.einsum('nid,njd->nij', a, b, preferred_element_type=F32)


def _unit_lower_inverse_minus_eye(low, ii, jj):
    mm = lambda a, b: _bmm(a.astype(BF16), b.astype(BF16))
    same16 = (ii // 16) == (jj // 16)
    same32 = (ii // 32) == (jj // 32)
    ld = jnp.where(same16, low, 0.0)
    e = -ld
    p = mm(ld, ld)
    for level in range(3):
        e = e + p + mm(e, p)
        if level < 2:
            p = mm(p, p)
    for within in (same32 & jnp.logical_not(same16), jnp.logical_not(same32)):
        c = jnp.where(within, low, 0.0)
        y = c + mm(e, c)
        e = e - y - mm(y, e)
    return e


def _gdn_kernel(xq_ref, xk_ref, xv_ref, small_ref, z_ref,
                onorm_ref, o_ref,
                p_s, qn_s, e2_s, mpa_s, na_s, ea_s, qt_s, o0_s, seven_s):
    hp = pl.program_id(1)
    s_len = xq_ref.shape[0]
    c_len = GDN_CHUNK
    grp = GDN_GROUP
    n_chunks = s_len // c_len
    n_grp = s_len // grp
    hd = GDN_DK
    row = lax.broadcasted_iota(jnp.int32, (s_len, LANES), 0)
    lane = lax.broadcasted_iota(jnp.int32, (s_len, LANES), 1)
    pos = row % c_len
    ii = lax.broadcasted_iota(jnp.int32, (grp, grp), 0)
    jj = lax.broadcasted_iota(jnp.int32, (grp, grp), 1)
    same_chunk = (ii // c_len) == (jj // c_len)
    tril = same_chunk & (ii >= jj)
    strict = same_chunk & (ii > jj)
    first = lax.broadcasted_iota(jnp.int32, (grp, hd), 0) < c_len
    small = small_ref[...]
    g3 = lambda a: a.reshape(n_grp, grp, hd)

    for j in range(GDN_HPS):
        h = hp * GDN_HPS + j
        cols = slice(j * hd, (j + 1) * hd)
        q, k, v = xq_ref[:, cols], xk_ref[:, cols], xv_ref[:, cols]

        pick = lambda ln: jnp.broadcast_to(
            jnp.sum(jnp.where(lane == ln, small, 0.0), axis=-1, keepdims=True), (s_len, LANES))
        beta = pick(GDN_HEADS + h)
        g = pick(h)
        sh = 1
        while sh < c_len:
            g = g + jnp.where(pos >= sh, pltpu.roll(g, sh, axis=0), 0.0)
            sh *= 2
        gc = g
        gc3 = gc.reshape(n_chunks, c_len, LANES)
        gl3 = gc3[:, c_len - 1:c_len, :]
        e_gl = jnp.exp(gl3).reshape(n_grp, grp // c_len, 1, LANES)
        dec_a, dec_b = e_gl[:, 0], e_gl[:, 1]
        gl = jnp.broadcast_to(gl3, gc3.shape).reshape(s_len, LANES)
        eg = jnp.exp(gc)
        kb = k * beta
        vb = g3(v * beta)
        kbg = g3(kb * eg)

        k3 = g3(k).astype(BF16)
        kk = _bmm_nt(g3(kb).astype(BF16), k3)
        qk = _bmm_nt(g3(q).astype(BF16), k3)
        gcg = g3(gc)
        diff = gcg - jnp.swapaxes(gcg, 1, 2)
        decay = jnp.where(tril, jnp.exp(jnp.where(tril, diff, 0.0)), 0.0)
        e_b = _unit_lower_inverse_minus_eye(jnp.where(strict, kk * decay, 0.0), ii, jj).astype(BF16)
        u = vb + _bmm(e_b, vb.astype(BF16))
        w = kbg + _bmm(e_b, kbg.astype(BF16))
        qkm = jnp.where(tril, qk * decay, 0.0).astype(BF16)
        r = _bmm(qkm, jnp.concatenate([w, u], axis=-1).astype(BF16))
        qt_s[j] = (g3(q * eg) - r[..., :hd]).astype(BF16)
        o0_s[j] = r[..., hd:]
        kd_t = jnp.swapaxes(g3(k * jnp.exp(gl - gc)), 1, 2).astype(BF16)
        rhs = jnp.concatenate([jnp.where(first, w, 0.0), jnp.where(first, 0.0, w),
                               jnp.where(first, u, 0.0), jnp.where(first, 0.0, u)], axis=-1)
        mn = _bmm(kd_t, rhs.astype(BF16))
        mp_a, mp_b = -mn[..., :hd], -mn[..., hd:2 * hd]
        n_a, n_b = mn[..., 2 * hd:3 * hd], mn[..., 3 * hd:]
        x = _bmm(mp_b.astype(BF16), jnp.concatenate([mp_a, n_a], axis=-1).astype(BF16))
        p_s[j] = (dec_b * mp_a + dec_a * mp_b + x[..., :hd]).astype(BF16)
        qn_s[j] = dec_b * n_a + x[..., hd:] + n_b
        e2_s[j] = dec_a * dec_b
        mpa_s[j] = mp_a.astype(BF16)
        na_s[j] = n_a
        ea_s[j] = dec_a

    def scan(n, states):
        out = []
        for j in range(GDN_HPS):
            sb = states[j].astype(BF16)
            seven_s[j, n] = sb
            out.append(e2_s[j, n] * states[j] + _dot(p_s[j, n], sb) + qn_s[j, n])
        return tuple(out)

    lax.fori_loop(0, n_grp, scan, tuple(jnp.zeros((GDN_DK, GDN_DV), F32) for _ in range(GDN_HPS)))

    for j in range(GDN_HPS):
        cols = slice(j * hd, (j + 1) * hd)
        s_even = seven_s[j]
        s_odd = (ea_s[j] * s_even.astype(F32) + _bmm(mpa_s[j], s_even) + na_s[j]).astype(BF16)
        qt = qt_s[j]
        o = jnp.concatenate([_bmm(qt[:, :c_len], s_even), _bmm(qt[:, c_len:], s_odd)], axis=1) + o0_s[j]
        o_ref[:, cols] = _rms(o.reshape(s_len, hd), onorm_ref[...]) * z_ref[:, cols]


def _gdn(qkv, z, small, onorm):
    b, s, _ = qkv.shape
    wid = GDN_HPS * GDN_DK
    n_hp = GDN_HEADS // GDN_HPS
    n_grp = s // GDN_GROUP
    per_pair = lambda rows, dt: pltpu.VMEM((GDN_HPS, n_grp, rows, GDN_DV), dt)
    col = lambda off: pl.BlockSpec((None, s, wid), lambda bi, h: (bi, 0, off + h))
    vec = pl.BlockSpec((1, LANES), lambda bi, h: (0, 0))
    return pl.pallas_call(
        _gdn_kernel,
        grid=(b, n_hp),
        in_specs=[col(0), col(n_hp), col(2 * n_hp),
                  pl.BlockSpec((None, s, LANES), lambda bi, h: (bi, 0, 0)),
                  col(0), vec],
        out_specs=col(0),
        out_shape=jax.ShapeDtypeStruct((b, s, GDN_HEADS * GDN_DV), F32),
        scratch_shapes=[per_pair(GDN_DK, BF16), per_pair(GDN_DK, F32), per_pair(1, F32),
                        per_pair(GDN_DK, BF16), per_pair(GDN_DK, F32), per_pair(1, F32),
                        per_pair(GDN_GROUP, BF16), per_pair(GDN_GROUP, F32), per_pair(GDN_DK, BF16)],
        compiler_params=pltpu.CompilerParams(dimension_semantics=("arbitrary", "arbitrary"),
                                             vmem_limit_bytes=VMEM_LIMIT),
        name="gdn",
    )(qkv, qkv, qkv, small, z, onorm)


def _compress_kernel(t_ref, pet_ref, peb_ref, wt_ref, wb_ref, w2_ref, o_ref):
    t = t_ref[...]
    a = _dot((t + pet_ref[...]).astype(BF16), wt_ref[...])
    bm = _dot((t + peb_ref[...]).astype(BF16), wb_ref[...])
    n_rows = t.shape[0]
    hid = a + pltpu.roll(bm, n_rows - 1, axis=0)
    out = _dot(_silu(hid).astype(BF16), w2_ref[...])
    for j in range(2 * NSA_KV):
        blk = out[:, j * LANES:(j + 1) * LANES]
        o_ref[j] = (blk if j < NSA_KV else blk.T).astype(o_ref.dtype)


def _compress(tm, pe_top, pe_bot, w_top, w_bot, w2):
    b, n_rows, width = tm.shape
    nh = 2 * NSA_KV * CMP_HIDDEN
    return pl.pallas_call(
        _compress_kernel,
        grid=(b,),
        in_specs=[pl.BlockSpec((None, n_rows, width), lambda bi: (bi, 0, 0)),
                  _const_spec((1, width)), _const_spec((1, width)),
                  _const_spec((width, nh)), _const_spec((width, nh)),
                  _const_spec((nh, 2 * NSA_KV * LANES))],
        out_specs=pl.BlockSpec((None, 2 * NSA_KV, n_rows, LANES), lambda bi: (bi, 0, 0, 0)),
        out_shape=jax.ShapeDtypeStruct((b, 2 * NSA_KV, n_rows, LANES), BF16),
        compiler_params=pltpu.CompilerParams(dimension_semantics=("arbitrary",),
                                             vmem_limit_bytes=VMEM_LIMIT),
        name="compress",
    )(tm, pe_top, pe_bot, w_top, w_bot, w2)


def _nsa_select_kernel(qp_ref, kc_ref, vct_ref, gt_ref, ocmp_ref, bias_ref):
    grp = pl.program_id(1)
    ts, hpg, dh = qp_ref.shape[2], NSA_HPG, NSA_DH
    q0 = pl.program_id(2) * ts
    n_cmp = kc_ref.shape[0]
    n_blk = n_cmp * CMP_STRIDE // SEL_BLOCK
    per_head = lambda a: jnp.concatenate([a] * hpg, axis=1)

    t_c = q0 + lax.broadcasted_iota(jnp.int32, (n_cmp, ts), 1)
    n_c = lax.broadcasted_iota(jnp.int32, (n_cmp, ts), 0)
    vis = per_head(jnp.where(n_c * CMP_STRIDE + (CMP_BLOCK - 1) <= t_c, 1.0, 0.0))
    pad = jnp.zeros((LANES - dh, ts), BF16)
    q_t = jnp.concatenate([jnp.concatenate([qp_ref[hh], pad], axis=0) for hh in range(hpg)], axis=1)
    s_cmp = jnp.where(vis > 0.5, _dot(kc_ref[...], q_t), NEG)
    p_cmp = jnp.exp(s_cmp - jnp.max(s_cmp, axis=0, keepdims=True)) * vis
    p_cmp = p_cmp * (1.0 / jnp.maximum(jnp.sum(p_cmp, axis=0, keepdims=True), 1e-30))
    o_cmp = _dot(vct_ref[...], p_cmp.astype(BF16))
    ocmp_ref[...] = jnp.concatenate(
        [gt_ref[pl.ds(GATE_LANE0 + (grp * hpg + hh) * 3, 1), :] * o_cmp[:dh, hh * ts:(hh + 1) * ts]
         for hh in range(hpg)], axis=0)

    p_sum = p_cmp[:, :ts]
    for hh in range(1, hpg):
        p_sum = p_sum + p_cmp[:, hh * ts:(hh + 1) * ts]
    oj = lax.broadcasted_iota(jnp.int32, (n_blk, n_cmp), 0) * SEL_BLOCK
    on = lax.broadcasted_iota(jnp.int32, (n_blk, n_cmp), 1) * CMP_STRIDE
    overlap_t = jnp.where(on < oj + SEL_BLOCK, jnp.where(on + CMP_BLOCK > oj, 1.0, 0.0), 0.0)
    imp_t = jnp.dot(overlap_t, p_sum, precision=lax.Precision.HIGHEST,
                    preferred_element_type=F32)
    blk = lax.broadcasted_iota(jnp.int32, (n_blk, ts), 0)
    ahead = (q0 + lax.broadcasted_iota(jnp.int32, (n_blk, ts), 1)) // SEL_BLOCK - blk
    score = jnp.where(ahead < 0, NEG, jnp.where(blk == 0, BIG, jnp.where(ahead <= 1, BIG, imp_t)))
    rank = jnp.zeros((n_blk, ts), F32)
    for i in range(n_blk):
        ci = jnp.broadcast_to(score[i:i + 1, :], (n_blk, ts))
        tie = jnp.where(blk > i, 1.0, 0.0)
        rank = rank + jnp.where(ci > score, 1.0, jnp.where(ci == score, tie, 0.0))
    bias_ref[...] = jnp.where(rank >= SEL_TOPK, NEG, 0.0).astype(bias_ref.dtype)


def _nsa_select(qp_t, kcvc, gates_t):
    b, _, _, s = qp_t.shape
    ts = NSA_TS
    n_cmp = kcvc.shape[2]
    cspec = lambda off: pl.BlockSpec((None, None, n_cmp, LANES), lambda bi, g, i: (bi, off + g, 0, 0))
    return pl.pallas_call(
        _nsa_select_kernel,
        grid=(b, NSA_KV, s // ts),
        in_specs=[pl.BlockSpec((None, NSA_HPG, NSA_DH, ts), lambda bi, g, i: (bi, g, 0, i)),
                  cspec(0), cspec(NSA_KV),
                  pl.BlockSpec((None, NSA_GATE_ROWS, ts), lambda bi, g, i: (bi, 0, i))],
        out_specs=(pl.BlockSpec((None, None, NSA_HPG * NSA_DH, ts), lambda bi, g, i: (bi, g, 0, i)),
                   pl.BlockSpec((None, None, s // SEL_BLOCK, ts), lambda bi, g, i: (bi, g, 0, i))),
        out_shape=(jax.ShapeDtypeStruct((b, NSA_KV, NSA_HPG * NSA_DH, s), F32),
                   jax.ShapeDtypeStruct((b, NSA_KV, s // SEL_BLOCK, s), BF16)),
        compiler_params=pltpu.CompilerParams(
            dimension_semantics=("arbitrary", "arbitrary", "arbitrary"),
            vmem_limit_bytes=VMEM_LIMIT),
        name="nsa_select",
    )(qp_t, kcvc, kcvc, gates_t)


def _nsa_kernel(qt_ref, bias_ref, ks_ref, vst_ref, kw_ref, vwt_ref, gt_ref, ocmp_ref, o_ref,
                *score_bufs):
    grp = pl.program_id(1)
    q0 = pl.program_id(2) * NSA_TQ
    tq, hpg, dh = NSA_TQ, NSA_HPG, NSA_DH
    bias = bias_ref[...]
    pad = jnp.zeros((LANES - dh - bias.shape[0], tq), BF16)
    q_aug = jnp.concatenate(
        [jnp.concatenate([qt_ref[hh], bias, pad], axis=0) for hh in range(hpg)], axis=1)
    per_head = lambda a: jnp.concatenate([a] * hpg, axis=1)

    def attend(*streams):
        tasks = [(i, chunk) for i, chunks in enumerate(streams) for chunk in chunks]

        def scores(n):
            k, _, mask = tasks[n][1]
            s_t = _dot(k, q_aug)
            score_bufs[n % 2][:k.shape[0], :] = s_t if mask is None else s_t + mask

        state = [(None, None)] * len(streams)
        scores(0)
        for n, (i, (k, v_t, _)) in enumerate(tasks):
            if n + 1 < len(tasks):
                scores(n + 1)
            s_t = score_bufs[n % 2][:k.shape[0], :]
            m, acc = state[i]
            m_new = jnp.max(s_t, axis=0, keepdims=True)
            if m is not None:
                m_new = jnp.maximum(m, m_new)
            part = _dot(v_t, jnp.exp2(s_t - m_new).astype(BF16))
            state[i] = (m_new, part if m is None else acc * jnp.exp2(m - m_new) + part)
        return [acc[:dh, :] * (1.0 / jnp.maximum(acc[dh:dh + 1, :], 1e-30)) for _, acc in state]

    kt = SLC_KT
    last = (q0 + tq - 1) // kt
    w0 = pl.multiple_of(jnp.maximum(q0 - WINDOW, 0), tq)
    row_k = lax.broadcasted_iota(jnp.int32, (kt, tq), 0)
    t_q = q0 + lax.broadcasted_iota(jnp.int32, (kt, tq), 1)

    def step(n_chunks):
        slc = []
        for c in range(n_chunks):
            rows = slice(c * kt, (c + 1) * kt)
            causal = per_head(jnp.where(c * kt + row_k <= t_q, 0.0, NEG)) if c == n_chunks - 1 else None
            slc.append((ks_ref[rows, :], vst_ref[:, rows], causal))
        win = []
        for off, n in ((0, WINDOW + tq - kt), (WINDOW + tq - kt, kt)):
            back = (q0 + lax.broadcasted_iota(jnp.int32, (n, tq), 1)
                    - (w0 + off + lax.broadcasted_iota(jnp.int32, (n, tq), 0)))
            band = per_head(jnp.where(back < 0, NEG, jnp.where(back < WINDOW, 0.0, NEG)))
            keys = pl.ds(pl.multiple_of(w0 + off, tq), n)
            win.append((kw_ref[keys, :], vwt_ref[:, keys], band))
        o_win, o_slc = attend(win, slc)
        mixed = []
        for hh in range(hpg):
            base = GATE_LANE0 + (grp * hpg + hh) * 3
            head = slice(hh * tq, (hh + 1) * tq)
            mixed.append(gt_ref[pl.ds(base + 1, 1), :] * o_slc[:, head]
                         + gt_ref[pl.ds(base + 2, 1), :] * o_win[:, head])
        o_ref[...] = (ocmp_ref[...] + jnp.concatenate(mixed, axis=0)).T.astype(o_ref.dtype)

    for n_chunks in range(1, ks_ref.shape[0] // kt + 1):
        pl.when(last == n_chunks - 1)(functools.partial(step, n_chunks))


def _nsa_attention(qt, sel_bias, ks, vst, kw, vwt, gates_t, o_cmp):
    b, _, _, s = qt.shape
    tq = NSA_TQ
    kspec = pl.BlockSpec((None, None, s, LANES), lambda bi, g, i: (bi, g, 0, 0))
    vspec = pl.BlockSpec((None, None, NSA_VT_ROWS, s), lambda bi, g, i: (bi, g, 0, 0))
    tok = pl.BlockSpec((None, tq, NSA_HPG * NSA_DH), lambda bi, g, i: (bi, i, g))
    return pl.pallas_call(
        _nsa_kernel,
        grid=(b, NSA_KV, s // tq),
        in_specs=[pl.BlockSpec((None, NSA_HPG, NSA_DH, tq), lambda bi, g, i: (bi, g, 0, i)),
                  pl.BlockSpec((None, None, s // SEL_BLOCK, tq), lambda bi, g, i: (bi, g, 0, i)),
                  kspec, vspec, kspec, vspec,
                  pl.BlockSpec((None, NSA_GATE_ROWS, tq), lambda bi, g, i: (bi, 0, i)),
                  pl.BlockSpec((None, None, NSA_HPG * NSA_DH, tq), lambda bi, g, i: (bi, g, 0, i))],
        out_specs=tok,
        out_shape=jax.ShapeDtypeStruct((b, s, W_NSA_Q), BF16),
        scratch_shapes=[pltpu.VMEM((WINDOW + tq - SLC_KT, NSA_HPG * tq), F32)] * 2,
        compiler_params=pltpu.CompilerParams(
            dimension_semantics=("arbitrary", "arbitrary", "arbitrary"),
            vmem_limit_bytes=VMEM_LIMIT),
        name="nsa_attn",
    )(qt, sel_bias, ks, vst, kw, vwt, gates_t, o_cmp)


def _out_ffn2_kernel(x_ref, og_ref, on_ref, wo_ref, nw_ref, wg_ref, wu_ref, wd_ref, fw_ref, o_ref):
    half = W_GDN_Z
    x = (x_ref[...] + _dot(og_ref[...].astype(BF16), wo_ref[:half, :])
         + _dot(on_ref[...], wo_ref[half:, :]))
    y = _swiglu_half_step(x, nw_ref[...], wg_ref, wu_ref, wd_ref)
    o_ref[...] = _rms(y, fw_ref[...])


def _out_ffn2(x2d, o_gdn, o_nsa, w_out, nw, wg, wu, wd, fw):
    t = x2d.shape[0]
    row = lambda w: pl.BlockSpec((TOKEN_TILE, w), lambda i: (i, 0))
    return pl.pallas_call(
        _out_ffn2_kernel,
        grid=(t // TOKEN_TILE,),
        in_specs=[row(D_MODEL), row(W_GDN_Z), row(W_NSA_Q), _const_spec((D_MODEL, D_MODEL)),
                  _const_spec((1, D_MODEL)), _const_spec((D_MODEL, D_FF)),
                  _const_spec((D_MODEL, D_FF)), _const_spec((D_FF, D_MODEL)),
                  _const_spec((1, D_MODEL))],
        out_specs=row(D_MODEL),
        out_shape=jax.ShapeDtypeStruct((t, D_MODEL), F32),
        compiler_params=pltpu.CompilerParams(dimension_semantics=("arbitrary",),
                                             vmem_limit_bytes=VMEM_LIMIT),
        name="out_ffn2",
    )(x2d, o_gdn, o_nsa, w_out, nw, wg, wu, wd, fw)


def _rope_tables(s_len):
    half = ROPE_DIM // 2
    inv = ROPE_THETA ** (-jnp.arange(half, dtype=F32) / half)
    ang = jnp.arange(s_len, dtype=F32)[:, None] * inv[None, :]
    cos, sin = jnp.cos(ang), jnp.sin(ang)
    pad = jnp.zeros((s_len, NSA_DH - ROPE_DIM), F32)
    zero = jnp.zeros_like(sin)
    cos_h = jnp.concatenate([cos, cos, pad + 1.0], axis=1)
    sa_h = jnp.concatenate([-sin, zero, pad], axis=1)
    sb_h = jnp.concatenate([zero, sin, pad], axis=1)
    rep = lambda a: jnp.tile(a, (1, LANES // NSA_DH))
    return rep(cos_h), rep(sa_h), rep(sb_h)


def _split_w_in(w_in):
    cuts = np.cumsum([W_GDN_QKV, W_GDN_Z, GDN_HEADS, GDN_HEADS, W_NSA_Q,
                      W_NSA_KV, W_NSA_KV, W_NSA_KV]).tolist()
    qkv, z, a, bb, nq, kvc, kvs, kvw, gate = jnp.split(w_in, cuts, axis=1)
    w_main = jnp.concatenate([qkv, z, nq, kvc, kvs, kvw], axis=1).astype(BF16)
    small = jnp.concatenate([a, bb, gate], axis=1)
    w_small = jnp.pad(small, ((0, 0), (0, LANES - small.shape[1]))).astype(BF16)
    return w_main, w_small


def _compress_weights(pe_k, w1_k, w2_k, pe_v, w1_v, w2_v):
    half = CMP_BLOCK // 2
    eye = jnp.eye(2 * NSA_KV, dtype=F32)

    def first_layer(lo):
        w1 = jnp.stack([w1_k, w1_v]).reshape(2, CMP_BLOCK, NSA_DH, CMP_HIDDEN)[:, lo:lo + half]
        w1 = jnp.repeat(w1, NSA_KV, axis=0)
        w = jnp.einsum('crdj,ce->rcdej', w1, eye)
        return w.reshape(half * 2 * NSA_KV * NSA_DH, 2 * NSA_KV * CMP_HIDDEN).astype(BF16)

    def pe_row(lo):
        pe = jnp.stack([pe_k, pe_v])[:, lo:lo + half]
        pe = jnp.repeat(pe, NSA_KV, axis=0)
        return pe.transpose(1, 0, 2).reshape(1, half * 2 * NSA_KV * NSA_DH)

    w2 = jnp.repeat(jnp.stack([w2_k, w2_v]), NSA_KV, axis=0)
    w2 = jnp.pad(w2, ((0, 0), (0, 0), (0, LANES - NSA_DH)))
    w2 = jnp.einsum('cjd,ce->cjed', w2, eye).reshape(2 * NSA_KV * CMP_HIDDEN, 2 * NSA_KV * LANES)
    return pe_row(0), pe_row(half), first_layer(0), first_layer(half), w2.astype(BF16)


def kernel(x, ffn1_norm, ffn1_w_gate, ffn1_w_up, ffn1_w_down, mix_norm, w_in, gdn_conv_w, gdn_a_log,
           gdn_dt_bias, gdn_out_norm, cmp_pe_k, cmp_w1_k, cmp_w2_k, cmp_pe_v, cmp_w1_v, cmp_w2_v,
           w_out, ffn2_norm, ffn2_w_gate, ffn2_w_up, ffn2_w_down, final_norm):
    b, s, d = x.shape
    assert d == D_MODEL and s % TOKEN_TILE == 0 and ffn1_norm.shape[0] == 1
    vec = lambda a: a.reshape(1, -1)
    pad_lanes = lambda a: jnp.pad(a.reshape(1, -1), ((0, 0), (0, LANES - a.size)))
    bf = lambda a: a.astype(BF16)

    x1 = _ffn1(x.reshape(b * s, d), vec(ffn1_norm[0]), bf(ffn1_w_gate[0]), bf(ffn1_w_up[0]),
               bf(ffn1_w_down[0]))

    w_main, w_small = _split_w_in(w_in[0])
    cos_t, sin_a, sin_b = _rope_tables(s)
    qkv, z, small, gates_t, qp, qt, kvc, ks, vst, kw, vwt = _in_proj(
        x1.reshape(b, s, d), vec(mix_norm[0]), w_main, w_small, cos_t, sin_a, sin_b,
        pad_lanes(gdn_a_log[0]), pad_lanes(gdn_dt_bias[0]), gdn_conv_w[0])

    o_gdn = _gdn(qkv, z, small, vec(gdn_out_norm[0]))

    pe_top, pe_bot, w_top, w_bot, w2 = _compress_weights(
        cmp_pe_k[0], cmp_w1_k[0], cmp_w2_k[0], cmp_pe_v[0], cmp_w1_v[0], cmp_w2_v[0])
    kcvc = _compress(kvc, pe_top, pe_bot, w_top, w_bot, w2)

    o_cmp, sel_bias = _nsa_select(qp, kcvc, gates_t)
    o_nsa = _nsa_attention(qt, sel_bias, ks, vst, kw, vwt, gates_t, o_cmp).reshape(b * s, W_NSA_Q)

    out = _out_ffn2(x1, o_gdn.reshape(b * s, W_GDN_Z), o_nsa, bf(w_out[0]), vec(ffn2_norm[0]),
                    bf(ffn2_w_gate[0]), bf(ffn2_w_up[0]), bf(ffn2_w_down[0]), vec(final_norm))
    return out.reshape(b, s, d)
```

```python
import functools

import numpy as np
import jax
import jax.numpy as jnp
from jax import lax
from jax.experimental import pallas as pl
from jax.experimental.pallas import tpu as pltpu

F32 = jnp.float32
BF16 = jnp.bfloat16

D_MODEL = 1024
D_FF = 2816
GDN_HEADS = 4
GDN_DK = 128
GDN_DV = 128
GDN_CONV = 4
GDN_CHUNK = 64
NSA_HEADS = 8
NSA_KV = 2
NSA_HPG = NSA_HEADS // NSA_KV
NSA_DH = 64
CMP_BLOCK = 32
CMP_STRIDE = 16
CMP_HIDDEN = 128
SEL_BLOCK = 64
SEL_TOPK = 16
WINDOW = 512
ROPE_THETA = 500000.0
ROPE_DIM = NSA_DH // 4
EPS = 1e-6
NEG = -1e30
LOG2E = 1.4426950408889634
BIG = 1e30

W_GDN_QKV = GDN_HEADS * (2 * GDN_DK + GDN_DV)
W_GDN_Z = GDN_HEADS * GDN_DV
W_NSA_Q = NSA_HEADS * NSA_DH
W_NSA_KV = 2 * NSA_KV * NSA_DH
W_NSA_GATE = 3 * NSA_HEADS
W_MAIN = W_GDN_QKV + W_GDN_Z + W_NSA_Q + 3 * W_NSA_KV
LANES = 128
GATE_LANE0 = 2 * GDN_HEADS

TOKEN_TILE = 512
MXU_TILE = 256
FF_SPLITS = (0, 6 * MXU_TILE, D_FF)
GDN_HPS = 2
GDN_GROUP = 2 * GDN_CHUNK
NSA_TQ = 256
NSA_TS = 512
NSA_VT_ROWS = 80
NSA_GATE_ROWS = 32
VMEM_LIMIT = 56 * 1024 * 1024


def _dot(a, b):
    return jnp.dot(a, b, preferred_element_type=F32)


def _rms(x, w):
    return x * lax.rsqrt(jnp.mean(x * x, axis=-1, keepdims=True) + EPS) * w


def _sigmoid(x):
    return 0.5 * jnp.tanh(0.5 * x) + 0.5


def _silu(x):
    return x * _sigmoid(x)


def _swiglu_half_step(x, nw, wg_ref, wu_ref, wd_ref):
    hb = _rms(x, nw).astype(BF16)
    acc = jnp.zeros_like(x)
    for lo, hi in zip(FF_SPLITS[:-1], FF_SPLITS[1:]):
        sl = slice(lo, hi)
        g = _dot(hb, wg_ref[:, sl])
        u = _dot(hb, wu_ref[:, sl])
        acc = acc + _dot((_silu(g) * u).astype(BF16), wd_ref[sl, :])
    return x + 0.5 * acc


def _ffn_kernel(x_ref, nw_ref, wg_ref, wu_ref, wd_ref, o_ref):
    o_ref[...] = _swiglu_half_step(x_ref[...], nw_ref[...], wg_ref, wu_ref, wd_ref)


def _const_spec(shape):
    nd = len(shape)
    return pl.BlockSpec(shape, lambda *_: (0,) * nd, pipeline_mode=pl.Buffered(1))


def _ffn1(x2d, nw, wg, wu, wd):
    t = x2d.shape[0]
    row = pl.BlockSpec((TOKEN_TILE, D_MODEL), lambda i: (i, 0))
    return pl.pallas_call(
        _ffn_kernel,
        grid=(t // TOKEN_TILE,),
        in_specs=[row, _const_spec((1, D_MODEL)), _const_spec((D_MODEL, D_FF)),
                  _const_spec((D_MODEL, D_FF)), _const_spec((D_FF, D_MODEL))],
        out_specs=row,
        out_shape=jax.ShapeDtypeStruct((t, D_MODEL), F32),
        compiler_params=pltpu.CompilerParams(dimension_semantics=("arbitrary",),
                                             vmem_limit_bytes=VMEM_LIMIT),
        name="ffn1",
    )(x2d, nw, wg, wu, wd)


def _rope(x, cos_t, sin_a, sin_b):
    return (x * cos_t + pltpu.roll(x, LANES - ROPE_DIM // 2, axis=1) * sin_a
            + pltpu.roll(x, ROPE_DIM // 2, axis=1) * sin_b)


def _inproj_kernel(x_ref, xp_ref, nw_ref, wm_ref, ws_ref, cos_ref, sa_ref, sb_ref, alog_ref, dt_ref, conv_ref,
                   qkv_ref, z_ref, small_ref, gt_ref, qp_ref, qt_ref, kvc_ref,
                   ks_ref, vst_ref, kw_ref, vwt_ref, kvc_s):
    hb = _rms(x_ref[...], nw_ref[...]).astype(BF16)
    hb_prev = _rms(xp_ref[...], nw_ref[...]).astype(BF16)
    tm, halo = x_ref.shape[0], xp_ref.shape[0]
    first_tile = pl.program_id(1) == 0
    lane = lax.broadcasted_iota(jnp.int32, (tm, LANES), 1)
    pos = pl.program_id(1) * tm + lax.broadcasted_iota(jnp.int32, (tm, LANES), 0)
    low_half = lane < NSA_DH
    cos_t, sin_a, sin_b = cos_ref[...], sa_ref[...], sb_ref[...]
    scale = NSA_DH ** -0.5
    part_w = GDN_HEADS * GDN_DK

    def halves(slab):
        return slab, pltpu.roll(slab, NSA_DH, axis=1)

    def gdn_part(part, cur):
        lo = part * part_w
        prev = jnp.where(first_tile, 0.0, _dot(hb_prev, wm_ref[:, lo:lo + part_w]))
        ext = jnp.concatenate([prev, cur], axis=0)
        taps = conv_ref[:, lo:lo + part_w]
        conv = cur * taps[GDN_CONV - 1:GDN_CONV, :]
        for sh in range(1, GDN_CONV):
            conv = conv + ext[halo - sh:halo - sh + tm, :] * taps[GDN_CONV - 1 - sh:GDN_CONV - sh, :]
        conv = _silu(conv)
        for hh in range(GDN_HEADS):
            blk = conv[:, hh * GDN_DK:(hh + 1) * GDN_DK]
            if part < 2:
                blk = blk * lax.rsqrt(jnp.sum(blk * blk, axis=-1, keepdims=True) + EPS)
            if part == 0:
                blk = blk * (GDN_DK ** -0.5)
            qkv_ref[:, lo + hh * GDN_DK:lo + (hh + 1) * GDN_DK] = blk

    def gate_z(y):
        z_ref[...] = _silu(y)

    def nsa_q(y):
        for s in range(W_NSA_Q // LANES):
            slab = y[:, s * LANES:(s + 1) * LANES]
            plain_t = (slab * scale).T
            rot_t = (_rope(slab, cos_t, sin_a, sin_b) * (scale * LOG2E)).T
            for j in range(2):
                qp_ref[2 * s + j] = plain_t[j * NSA_DH:(j + 1) * NSA_DH, :].astype(BF16)
                qt_ref[2 * s + j] = rot_t[j * NSA_DH:(j + 1) * NSA_DH, :].astype(BF16)

    def cmp_kv(y):
        for half in range(W_NSA_KV // LANES):
            kvc_s[half] = y[:, half * LANES:(half + 1) * LANES]
        for r in range(CMP_STRIDE):
            for half in range(W_NSA_KV // LANES):
                dst = r * W_NSA_KV + half * LANES
                kvc_ref[:, dst:dst + LANES] = kvc_s[half, pl.ds(r, tm // CMP_STRIDE, stride=CMP_STRIDE), :]

    def attn_kv(k_ref, vt_ref, selected, y):
        k_pad = jnp.where(lane - NSA_DH == pos // SEL_BLOCK, 1.0, 0.0) if selected else 0.0
        n_tail = vt_ref.shape[1] - NSA_DH
        tail = jnp.where(lax.broadcasted_iota(jnp.int32, (n_tail, tm), 0) == 0, 1.0, 0.0)
        kr = halves(_rope(y[:, :LANES], cos_t, sin_a, sin_b))
        v_t = y[:, LANES:].T
        for j in range(NSA_KV):
            k_ref[j] = jnp.where(low_half, kr[j], k_pad).astype(BF16)
            vt_ref[j] = jnp.concatenate([v_t[j * NSA_DH:(j + 1) * NSA_DH, :], tail], axis=0).astype(BF16)

    def small_proj(sm):
        xg = sm + dt_ref[...]
        softplus = jnp.maximum(xg, 0.0) + jnp.log1p(jnp.exp(-jnp.abs(xg)))
        small = jnp.where(lane < GDN_HEADS, -jnp.exp(alog_ref[...]) * softplus, _sigmoid(sm))
        small_ref[...] = small
        gt_ref[...] = small.T[:gt_ref.shape[0], :]

    stages = [(ws_ref, 0, LANES, small_proj)]
    stages += [(wm_ref, part * part_w, part_w, functools.partial(gdn_part, part))
               for part in range(W_GDN_QKV // part_w)]
    z_lo = W_GDN_QKV
    q_lo = z_lo + W_GDN_Z
    kvc_lo = q_lo + W_NSA_Q
    stages += [(wm_ref, q_lo, W_NSA_Q, nsa_q),
               (wm_ref, kvc_lo + W_NSA_KV, W_NSA_KV, functools.partial(attn_kv, ks_ref, vst_ref, True)),
               (wm_ref, kvc_lo + 2 * W_NSA_KV, W_NSA_KV, functools.partial(attn_kv, kw_ref, vwt_ref, False)),
               (wm_ref, z_lo, W_GDN_Z, gate_z),
               (wm_ref, kvc_lo, W_NSA_KV, cmp_kv)]
    for w_ref, lo, width, epilogue in stages:
        epilogue(_dot(hb, w_ref[:, lo:lo + width]))


def _in_proj(x3d, nw, w_main, w_small, cos_t, sin_a, sin_b, alog_p, dt_p, conv_w):
    b, s, _ = x3d.shape
    tm = TOKEN_TILE
    row = lambda w: pl.BlockSpec((None, tm, w), lambda bi, i: (bi, i, 0))
    heads = lambda n, w: pl.BlockSpec((None, n, tm, w), lambda bi, i: (bi, 0, i, 0))
    heads_t = lambda n, r: pl.BlockSpec((None, n, r, tm), lambda bi, i: (bi, 0, 0, i))
    tab = pl.BlockSpec((tm, LANES), lambda bi, i: (i, 0))
    halo = 2 * GDN_CONV
    prev_rows = pl.BlockSpec((None, halo, D_MODEL), lambda bi, i: (bi, jnp.maximum(i * (tm // halo) - 1, 0), 0))
    bf_heads = lambda n, w: jax.ShapeDtypeStruct((b, n, s, w), BF16)
    out_shape = (
        jax.ShapeDtypeStruct((b, s, W_GDN_QKV), F32),
        jax.ShapeDtypeStruct((b, s, W_GDN_Z), F32),
        jax.ShapeDtypeStruct((b, s, LANES), F32),
        jax.ShapeDtypeStruct((b, NSA_GATE_ROWS, s), F32),
        jax.ShapeDtypeStruct((b, NSA_HEADS, NSA_DH, s), BF16),
        jax.ShapeDtypeStruct((b, NSA_HEADS, NSA_DH, s), BF16),
        jax.ShapeDtypeStruct((b, s // CMP_STRIDE, CMP_STRIDE * W_NSA_KV), F32),
        bf_heads(NSA_KV, LANES),
        jax.ShapeDtypeStruct((b, NSA_KV, NSA_VT_ROWS, s), BF16),
        bf_heads(NSA_KV, LANES),
        jax.ShapeDtypeStruct((b, NSA_KV, NSA_VT_ROWS, s), BF16),
    )
    return pl.pallas_call(
        _inproj_kernel,
        grid=(b, s // tm),
        in_specs=[row(D_MODEL), prev_rows, _const_spec((1, D_MODEL)), _const_spec((D_MODEL, W_MAIN)),
                  _const_spec((D_MODEL, LANES)), tab, tab, tab,
                  _const_spec((1, LANES)), _const_spec((1, LANES)), _const_spec((GDN_CONV, W_GDN_QKV))],
        out_specs=(row(W_GDN_QKV), row(W_GDN_Z), row(LANES),
                   pl.BlockSpec((None, NSA_GATE_ROWS, tm), lambda bi, i: (bi, 0, i)),
                   heads_t(NSA_HEADS, NSA_DH), heads_t(NSA_HEADS, NSA_DH),
                   pl.BlockSpec((None, tm // CMP_STRIDE, CMP_STRIDE * W_NSA_KV), lambda bi, i: (bi, i, 0)),
                   heads(NSA_KV, LANES), heads_t(NSA_KV, NSA_VT_ROWS),
                   heads(NSA_KV, LANES), heads_t(NSA_KV, NSA_VT_ROWS)),
        out_shape=out_shape,
        scratch_shapes=[pltpu.VMEM((W_NSA_KV // LANES, tm, LANES), F32)],
        compiler_params=pltpu.CompilerParams(dimension_semantics=("arbitrary", "arbitrary"),
                                             vmem_limit_bytes=VMEM_LIMIT),
        name="in_proj",
    )(x3d, x3d, nw, w_main, w_small, cos_t, sin_a, sin_b, alog_p, dt_p, conv_w)


def _bmm(a, b):
    return jnp.einsum('nij,njk->nik', a, b, preferred_element_type=F32)


def _bmm_nt(a, b):
    return jnp.einsum('nid,njd->nij', a, b, preferred_element_type=F32)


def _unit_lower_inverse_minus_eye(low, ii, jj):
    mm = lambda a, b: _bmm(a.astype(BF16), b.astype(BF16))
    same16 = (ii // 16) == (jj // 16)
    same32 = (ii // 32) == (jj // 32)
    ld = jnp.where(same16, low, 0.0)
    e = -ld
    p = mm(ld, ld)
    for level in range(3):
        e = e + p + mm(e, p)
        if level < 2:
            p = mm(p, p)
    for within in (same32 & jnp.logical_not(same16), jnp.logical_not(same32)):
        c = jnp.where(within, low, 0.0)
        y = c + mm(e, c)
        e = e - y - mm(y, e)
    return e


def _gdn_kernel(xq_ref, xk_ref, xv_ref, small_ref, z_ref,
                onorm_ref, o_ref,
                p_s, qn_s, e2_s, mpa_s, na_s, ea_s, qt_s, o0_s, seven_s):
    hp = pl.program_id(1)
    s_len = xq_ref.shape[0]
    c_len = GDN_CHUNK
    grp = GDN_GROUP
    n_chunks = s_len // c_len
    n_grp = s_len // grp
    hd = GDN_DK
    row = lax.broadcasted_iota(jnp.int32, (s_len, LANES), 0)
    lane = lax.broadcasted_iota(jnp.int32, (s_len, LANES), 1)
    pos = row % c_len
    ii = lax.broadcasted_iota(jnp.int32, (grp, grp), 0)
    jj = lax.broadcasted_iota(jnp.int32, (grp, grp), 1)
    same_chunk = (ii // c_len) == (jj // c_len)
    tril = same_chunk & (ii >= jj)
    strict = same_chunk & (ii > jj)
    first = lax.broadcasted_iota(jnp.int32, (grp, hd), 0) < c_len
    small = small_ref[...]
    g3 = lambda a: a.reshape(n_grp, grp, hd)

    for j in range(GDN_HPS):
        h = hp * GDN_HPS + j
        cols = slice(j * hd, (j + 1) * hd)
        q, k, v = xq_ref[:, cols], xk_ref[:, cols], xv_ref[:, cols]

        pick = lambda ln: jnp.broadcast_to(
            jnp.sum(jnp.where(lane == ln, small, 0.0), axis=-1, keepdims=True), (s_len, LANES))
        beta = pick(GDN_HEADS + h)
        g = pick(h)
        sh = 1
        while sh < c_len:
            g = g + jnp.where(pos >= sh, pltpu.roll(g, sh, axis=0), 0.0)
            sh *= 2
        gc = g
        gc3 = gc.reshape(n_chunks, c_len, LANES)
        gl3 = gc3[:, c_len - 1:c_len, :]
        e_gl = jnp.exp(gl3).reshape(n_grp, grp // c_len, 1, LANES)
        dec_a, dec_b = e_gl[:, 0], e_gl[:, 1]
        gl = jnp.broadcast_to(gl3, gc3.shape).reshape(s_len, LANES)
        eg = jnp.exp(gc)
        kb = k * beta
        vb = g3(v * beta)
        kbg = g3(kb * eg)

        k3 = g3(k).astype(BF16)
        kk = _bmm_nt(g3(kb).astype(BF16), k3)
        qk = _bmm_nt(g3(q).astype(BF16), k3)
        gcg = g3(gc)
        diff = gcg - jnp.swapaxes(gcg, 1, 2)
        decay = jnp.where(tril, jnp.exp(jnp.where(tril, diff, 0.0)), 0.0)
        e_b = _unit_lower_inverse_minus_eye(jnp.where(strict, kk * decay, 0.0), ii, jj).astype(BF16)
        u = vb + _bmm(e_b, vb.astype(BF16))
        w = kbg + _bmm(e_b, kbg.astype(BF16))
        qkm = jnp.where(tril, qk * decay, 0.0).astype(BF16)
        r = _bmm(qkm, jnp.concatenate([w, u], axis=-1).astype(BF16))
        qt_s[j] = (g3(q * eg) - r[..., :hd]).astype(BF16)
        o0_s[j] = r[..., hd:]
        kd_t = jnp.swapaxes(g3(k * jnp.exp(gl - gc)), 1, 2).astype(BF16)
        rhs = jnp.concatenate([jnp.where(first, w, 0.0), jnp.where(first, 0.0, w),
                               jnp.where(first, u, 0.0), jnp.where(first, 0.0, u)], axis=-1)
        mn = _bmm(kd_t, rhs.astype(BF16))
        mp_a, mp_b = -mn[..., :hd], -mn[..., hd:2 * hd]
        n_a, n_b = mn[..., 2 * hd:3 * hd], mn[..., 3 * hd:]
        x = _bmm(mp_b.astype(BF16), jnp.concatenate([mp_a, n_a], axis=-1).astype(BF16))
        p_s[j] = (dec_b * mp_a + dec_a * mp_b + x[..., :hd]).astype(BF16)
        qn_s[j] = dec_b * n_a + x[..., hd:] + n_b
        e2_s[j] = dec_a * dec_b
        mpa_s[j] = mp_a.astype(BF16)
        na_s[j] = n_a
        ea_s[j] = dec_a

    def scan(n, states):
        out = []
        for j in range(GDN_HPS):
            sb = states[j].astype(BF16)
            seven_s[j, n] = sb
            out.append(e2_s[j, n] * states[j] + _dot(p_s[j, n], sb) + qn_s[j, n])
        return tuple(out)

    lax.fori_loop(0, n_grp, scan, tuple(jnp.zeros((GDN_DK, GDN_DV), F32) for _ in range(GDN_HPS)))

    for j in range(GDN_HPS):
        cols = slice(j * hd, (j + 1) * hd)
        s_even = seven_s[j]
        s_odd = (ea_s[j] * s_even.astype(F32) + _bmm(mpa_s[j], s_even) + na_s[j]).astype(BF16)
        qt = qt_s[j]
        o = jnp.concatenate([_bmm(qt[:, :c_len], s_even), _bmm(qt[:, c_len:], s_odd)], axis=1) + o0_s[j]
        o_ref[:, cols] = _rms(o.reshape(s_len, hd), onorm_ref[...]) * z_ref[:, cols]


def _gdn(qkv, z, small, onorm):
    b, s, _ = qkv.shape
    wid = GDN_HPS * GDN_DK
    n_hp = GDN_HEADS // GDN_HPS
    n_grp = s // GDN_GROUP
    per_pair = lambda rows, dt: pltpu.VMEM((GDN_HPS, n_grp, rows, GDN_DV), dt)
    col = lambda off: pl.BlockSpec((None, s, wid), lambda bi, h: (bi, 0, off + h))
    vec = pl.BlockSpec((1, LANES), lambda bi, h: (0, 0))
    return pl.pallas_call(
        _gdn_kernel,
        grid=(b, n_hp),
        in_specs=[col(0), col(n_hp), col(2 * n_hp),
                  pl.BlockSpec((None, s, LANES), lambda bi, h: (bi, 0, 0)),
                  col(0), vec],
        out_specs=col(0),
        out_shape=jax.ShapeDtypeStruct((b, s, GDN_HEADS * GDN_DV), F32),
        scratch_shapes=[per_pair(GDN_DK, BF16), per_pair(GDN_DK, F32), per_pair(1, F32),
                        per_pair(GDN_DK, BF16), per_pair(GDN_DK, F32), per_pair(1, F32),
                        per_pair(GDN_GROUP, BF16), per_pair(GDN_GROUP, F32), per_pair(GDN_DK, BF16)],
        compiler_params=pltpu.CompilerParams(dimension_semantics=("arbitrary", "arbitrary"),
                                             vmem_limit_bytes=VMEM_LIMIT),
        name="gdn",
    )(qkv, qkv, qkv, small, z, onorm)


def _compress_kernel(t_ref, pet_ref, peb_ref, wt_ref, wb_ref, w2_ref, o_ref):
    t = t_ref[...]
    a = _dot((t + pet_ref[...]).astype(BF16), wt_ref[...])
    bm = _dot((t + peb_ref[...]).astype(BF16), wb_ref[...])
    n_rows = t.shape[0]
    hid = a + pltpu.roll(bm, n_rows - 1, axis=0)
    out = _dot(_silu(hid).astype(BF16), w2_ref[...])
    for j in range(2 * NSA_KV):
        blk = out[:, j * LANES:(j + 1) * LANES]
        o_ref[j] = (blk if j < NSA_KV else blk.T).astype(o_ref.dtype)


def _compress(tm, pe_top, pe_bot, w_top, w_bot, w2):
    b, n_rows, width = tm.shape
    nh = 2 * NSA_KV * CMP_HIDDEN
    return pl.pallas_call(
        _compress_kernel,
        grid=(b,),
        in_specs=[pl.BlockSpec((None, n_rows, width), lambda bi: (bi, 0, 0)),
                  _const_spec((1, width)), _const_spec((1, width)),
                  _const_spec((width, nh)), _const_spec((width, nh)),
                  _const_spec((nh, 2 * NSA_KV * LANES))],
        out_specs=pl.BlockSpec((None, 2 * NSA_KV, n_rows, LANES), lambda bi: (bi, 0, 0, 0)),
        out_shape=jax.ShapeDtypeStruct((b, 2 * NSA_KV, n_rows, LANES), BF16),
        compiler_params=pltpu.CompilerParams(dimension_semantics=("arbitrary",),
                                             vmem_limit_bytes=VMEM_LIMIT),
        name="compress",
    )(tm, pe_top, pe_bot, w_top, w_bot, w2)


def _nsa_select_kernel(qp_ref, kc_ref, vct_ref, gt_ref, ocmp_ref, bias_ref):
    grp = pl.program_id(1)
    ts, hpg, dh = qp_ref.shape[2], NSA_HPG, NSA_DH
    q0 = pl.program_id(2) * ts
    n_cmp = kc_ref.shape[0]
    n_blk = n_cmp * CMP_STRIDE // SEL_BLOCK
    per_head = lambda a: jnp.concatenate([a] * hpg, axis=1)

    t_c = q0 + lax.broadcasted_iota(jnp.int32, (n_cmp, ts), 1)
    n_c = lax.broadcasted_iota(jnp.int32, (n_cmp, ts), 0)
    vis = per_head(jnp.where(n_c * CMP_STRIDE + (CMP_BLOCK - 1) <= t_c, 1.0, 0.0))
    pad = jnp.zeros((LANES - dh, ts), BF16)
    q_t = jnp.concatenate([jnp.concatenate([qp_ref[hh], pad], axis=0) for hh in range(hpg)], axis=1)
    s_cmp = jnp.where(vis > 0.5, _dot(kc_ref[...], q_t), NEG)
    p_cmp = jnp.exp(s_cmp - jnp.max(s_cmp, axis=0, keepdims=True)) * vis
    p_cmp = p_cmp * (1.0 / jnp.maximum(jnp.sum(p_cmp, axis=0, keepdims=True), 1e-30))
    o_cmp = _dot(vct_ref[...], p_cmp.astype(BF16))
    ocmp_ref[...] = jnp.concatenate(
        [gt_ref[pl.ds(GATE_LANE0 + (grp * hpg + hh) * 3, 1), :] * o_cmp[:dh, hh * ts:(hh + 1) * ts]
         for hh in range(hpg)], axis=0)

    p_sum = p_cmp[:, :ts]
    for hh in range(1, hpg):
        p_sum = p_sum + p_cmp[:, hh * ts:(hh + 1) * ts]
    oj = lax.broadcasted_iota(jnp.int32, (n_blk, n_cmp), 0) * SEL_BLOCK
    on = lax.broadcasted_iota(jnp.int32, (n_blk, n_cmp), 1) * CMP_STRIDE
    overlap_t = jnp.where(on < oj + SEL_BLOCK, jnp.where(on + CMP_BLOCK > oj, 1.0, 0.0), 0.0)
    imp_t = jnp.dot(overlap_t, p_sum, precision=lax.Precision.HIGHEST,
                    preferred_element_type=F32)
    blk = lax.broadcasted_iota(jnp.int32, (n_blk, ts), 0)
    ahead = (q0 + lax.broadcasted_iota(jnp.int32, (n_blk, ts), 1)) // SEL_BLOCK - blk
    score = jnp.where(ahead < 0, NEG, jnp.where(blk == 0, BIG, jnp.where(ahead <= 1, BIG, imp_t)))
    rank = jnp.zeros((n_blk, ts), F32)
    for i in range(n_blk):
        ci = jnp.broadcast_to(score[i:i + 1, :], (n_blk, ts))
        tie = jnp.where(blk > i, 1.0, 0.0)
        rank = rank + jnp.where(ci > score, 1.0, jnp.where(ci == score, tie, 0.0))
    bias_ref[...] = jnp.where(rank >= SEL_TOPK, NEG, 0.0).astype(bias_ref.dtype)


def _nsa_select(qp_t, kcvc, gates_t):
    b, _, _, s = qp_t.shape
    ts = NSA_TS
    n_cmp = kcvc.shape[2]
    cspec = lambda off: pl.BlockSpec((None, None, n_cmp, LANES), lambda bi, g, i: (bi, off + g, 0, 0))
    return pl.pallas_call(
        _nsa_select_kernel,
        grid=(b, NSA_KV, s // ts),
        in_specs=[pl.BlockSpec((None, NSA_HPG, NSA_DH, ts), lambda bi, g, i: (bi, g, 0, i)),
                  cspec(0), cspec(NSA_KV),
                  pl.BlockSpec((None, NSA_GATE_ROWS, ts), lambda bi, g, i: (bi, 0, i))],
        out_specs=(pl.BlockSpec((None, None, NSA_HPG * NSA_DH, ts), lambda bi, g, i: (bi, g, 0, i)),
                   pl.BlockSpec((None, None, s // SEL_BLOCK, ts), lambda bi, g, i: (bi, g, 0, i))),
        out_shape=(jax.ShapeDtypeStruct((b, NSA_KV, NSA_HPG * NSA_DH, s), F32),
                   jax.ShapeDtypeStruct((b, NSA_KV, s // SEL_BLOCK, s), BF16)),
        compiler_params=pltpu.CompilerParams(
            dimension_semantics=("arbitrary", "arbitrary", "arbitrary"),
            vmem_limit_bytes=VMEM_LIMIT),
        name="nsa_select",
    )(qp_t, kcvc, kcvc, gates_t)


def _nsa_kernel(qt_ref, bias_ref, ks_ref, vst_ref, kw_ref, vwt_ref, gt_ref, ocmp_ref, o_ref,
                *score_bufs):
    grp = pl.program_id(1)
    tq, hpg, dh = NSA_TQ, NSA_HPG, NSA_DH
    bias = bias_ref[...]
    pad = jnp.zeros((LANES - dh - bias.shape[0], tq), BF16)
    q_aug = jnp.concatenate(
        [jnp.concatenate([qt_ref[hh], bias, pad], axis=0) for hh in range(hpg)], axis=1)
    per_head = lambda a: jnp.concatenate([a] * hpg, axis=1)

    def attend(*streams):
        tasks = [(i, chunk) for i, chunks in enumerate(streams) for chunk in chunks]

        def scores(n):
            k, _, mask = tasks[n][1]
            s_t = _dot(k, q_aug)
            score_bufs[n % 2][:k.shape[0], :] = s_t if mask is None else s_t + mask

        state = [(None, None)] * len(streams)
        scores(0)
        for n, (i, (k, v_t, _)) in enumerate(tasks):
            if n + 1 < len(tasks):
                scores(n + 1)
            s_t = score_bufs[n % 2][:k.shape[0], :]
            m, acc = state[i]
            m_new = jnp.max(s_t, axis=0, keepdims=True)
            if m is not None:
                m_new = jnp.maximum(m, m_new)
            part = _dot(v_t, jnp.exp2(s_t - m_new).astype(BF16))
            state[i] = (m_new, part if m is None else acc * jnp.exp2(m - m_new) + part)
        return [acc[:dh, :] * (1.0 / jnp.maximum(acc[dh:dh + 1, :], 1e-30)) for _, acc in state]

    kt = tq
    ahead = (lax.broadcasted_iota(jnp.int32, (kt, tq), 0)
             - lax.broadcasted_iota(jnp.int32, (kt, tq), 1))

    def step(tile):
        t0 = tile * tq
        slc = []
        for c in range(tile + 1):
            rows = slice(c * kt, (c + 1) * kt)
            causal = per_head(jnp.where(ahead <= 0, 0.0, NEG)) if c == tile else None
            slc.append((ks_ref[rows, :], vst_ref[:, rows], causal))
        win = []
        for k0 in range(max(t0 - WINDOW, 0), t0 + tq, kt):
            back = (t0 - k0) - ahead
            crosses = k0 + kt - 1 > t0 or k0 <= t0 + tq - 1 - WINDOW
            band = per_head(jnp.where(back < 0, NEG, jnp.where(back < WINDOW, 0.0, NEG))) if crosses else None
            win.append((kw_ref[k0:k0 + kt, :], vwt_ref[:, k0:k0 + kt], band))
        o_win, o_slc = attend(win, slc)
        mixed = []
        for hh in range(hpg):
            base = GATE_LANE0 + (grp * hpg + hh) * 3
            head = slice(hh * tq, (hh + 1) * tq)
            mixed.append(gt_ref[pl.ds(base + 1, 1), :] * o_slc[:, head]
                         + gt_ref[pl.ds(base + 2, 1), :] * o_win[:, head])
        o_ref[...] = (ocmp_ref[...] + jnp.concatenate(mixed, axis=0)).T.astype(o_ref.dtype)

    for tile in range(ks_ref.shape[0] // tq):
        pl.when(pl.program_id(2) == tile)(functools.partial(step, tile))


def _nsa_attention(qt, sel_bias, ks, vst, kw, vwt, gates_t, o_cmp):
    b, _, _, s = qt.shape
    tq = NSA_TQ
    kspec = pl.BlockSpec((None, None, s, LANES), lambda bi, g, i: (bi, g, 0, 0))
    vspec = pl.BlockSpec((None, None, NSA_VT_ROWS, s), lambda bi, g, i: (bi, g, 0, 0))
    tok = pl.BlockSpec((None, tq, NSA_HPG * NSA_DH), lambda bi, g, i: (bi, i, g))
    return pl.pallas_call(
        _nsa_kernel,
        grid=(b, NSA_KV, s // tq),
        in_specs=[pl.BlockSpec((None, NSA_HPG, NSA_DH, tq), lambda bi, g, i: (bi, g, 0, i)),
                  pl.BlockSpec((None, None, s // SEL_BLOCK, tq), lambda bi, g, i: (bi, g, 0, i)),
                  kspec, vspec, kspec, vspec,
                  pl.BlockSpec((None, NSA_GATE_ROWS, tq), lambda bi, g, i: (bi, 0, i)),
                  pl.BlockSpec((None, None, NSA_HPG * NSA_DH, tq), lambda bi, g, i: (bi, g, 0, i))],
        out_specs=tok,
        out_shape=jax.ShapeDtypeStruct((b, s, W_NSA_Q), BF16),
        scratch_shapes=[pltpu.VMEM((tq, NSA_HPG * tq), F32)] * 2,
        compiler_params=pltpu.CompilerParams(
            dimension_semantics=("arbitrary", "arbitrary", "arbitrary"),
            vmem_limit_bytes=VMEM_LIMIT),
        name="nsa_attn",
    )(qt, sel_bias, ks, vst, kw, vwt, gates_t, o_cmp)


def _out_ffn2_kernel(x_ref, og_ref, on_ref, wo_ref, nw_ref, wg_ref, wu_ref, wd_ref, fw_ref, o_ref):
    half = W_GDN_Z
    x = (x_ref[...] + _dot(og_ref[...].astype(BF16), wo_ref[:half, :])
         + _dot(on_ref[...], wo_ref[half:, :]))
    y = _swiglu_half_step(x, nw_ref[...], wg_ref, wu_ref, wd_ref)
    o_ref[...] = _rms(y, fw_ref[...])


def _out_ffn2(x2d, o_gdn, o_nsa, w_out, nw, wg, wu, wd, fw):
    t = x2d.shape[0]
    row = lambda w: pl.BlockSpec((TOKEN_TILE, w), lambda i: (i, 0))
    return pl.pallas_call(
        _out_ffn2_kernel,
        grid=(t // TOKEN_TILE,),
        in_specs=[row(D_MODEL), row(W_GDN_Z), row(W_NSA_Q), _const_spec((D_MODEL, D_MODEL)),
                  _const_spec((1, D_MODEL)), _const_spec((D_MODEL, D_FF)),
                  _const_spec((D_MODEL, D_FF)), _const_spec((D_FF, D_MODEL)),
                  _const_spec((1, D_MODEL))],
        out_specs=row(D_MODEL),
        out_shape=jax.ShapeDtypeStruct((t, D_MODEL), F32),
        compiler_params=pltpu.CompilerParams(dimension_semantics=("arbitrary",),
                                             vmem_limit_bytes=VMEM_LIMIT),
        name="out_ffn2",
    )(x2d, o_gdn, o_nsa, w_out, nw, wg, wu, wd, fw)


def _rope_tables(s_len):
    half = ROPE_DIM // 2
    inv = ROPE_THETA ** (-jnp.arange(half, dtype=F32) / half)
    ang = jnp.arange(s_len, dtype=F32)[:, None] * inv[None, :]
    cos, sin = jnp.cos(ang), jnp.sin(ang)
    pad = jnp.zeros((s_len, NSA_DH - ROPE_DIM), F32)
    zero = jnp.zeros_like(sin)
    cos_h = jnp.concatenate([cos, cos, pad + 1.0], axis=1)
    sa_h = jnp.concatenate([-sin, zero, pad], axis=1)
    sb_h = jnp.concatenate([zero, sin, pad], axis=1)
    rep = lambda a: jnp.tile(a, (1, LANES // NSA_DH))
    return rep(cos_h), rep(sa_h), rep(sb_h)


def _split_w_in(w_in):
    cuts = np.cumsum([W_GDN_QKV, W_GDN_Z, GDN_HEADS, GDN_HEADS, W_NSA_Q,
                      W_NSA_KV, W_NSA_KV, W_NSA_KV]).tolist()
    qkv, z, a, bb, nq, kvc, kvs, kvw, gate = jnp.split(w_in, cuts, axis=1)
    w_main = jnp.concatenate([qkv, z, nq, kvc, kvs, kvw], axis=1).astype(BF16)
    small = jnp.concatenate([a, bb, gate], axis=1)
    w_small = jnp.pad(small, ((0, 0), (0, LANES - small.shape[1]))).astype(BF16)
    return w_main, w_small


def _compress_weights(pe_k, w1_k, w2_k, pe_v, w1_v, w2_v):
    half = CMP_BLOCK // 2
    eye = jnp.eye(2 * NSA_KV, dtype=F32)

    def first_layer(lo):
        w1 = jnp.stack([w1_k, w1_v]).reshape(2, CMP_BLOCK, NSA_DH, CMP_HIDDEN)[:, lo:lo + half]
        w1 = jnp.repeat(w1, NSA_KV, axis=0)
        w = jnp.einsum('crdj,ce->rcdej', w1, eye)
        return w.reshape(half * 2 * NSA_KV * NSA_DH, 2 * NSA_KV * CMP_HIDDEN).astype(BF16)

    def pe_row(lo):
        pe = jnp.stack([pe_k, pe_v])[:, lo:lo + half]
        pe = jnp.repeat(pe, NSA_KV, axis=0)
        return pe.transpose(1, 0, 2).reshape(1, half * 2 * NSA_KV * NSA_DH)

    w2 = jnp.repeat(jnp.stack([w2_k, w2_v]), NSA_KV, axis=0)
    w2 = jnp.pad(w2, ((0, 0), (0, 0), (0, LANES - NSA_DH)))
    w2 = jnp.einsum('cjd,ce->cjed', w2, eye).reshape(2 * NSA_KV * CMP_HIDDEN, 2 * NSA_KV * LANES)
    return pe_row(0), pe_row(half), first_layer(0), first_layer(half), w2.astype(BF16)


def kernel(x, ffn1_norm, ffn1_w_gate, ffn1_w_up, ffn1_w_down, mix_norm, w_in, gdn_conv_w, gdn_a_log,
           gdn_dt_bias, gdn_out_norm, cmp_pe_k, cmp_w1_k, cmp_w2_k, cmp_pe_v, cmp_w1_v, cmp_w2_v,
           w_out, ffn2_norm, ffn2_w_gate, ffn2_w_up, ffn2_w_down, final_norm):
    b, s, d = x.shape
    assert d == D_MODEL and s % TOKEN_TILE == 0 and ffn1_norm.shape[0] == 1
    vec = lambda a: a.reshape(1, -1)
    pad_lanes = lambda a: jnp.pad(a.reshape(1, -1), ((0, 0), (0, LANES - a.size)))
    bf = lambda a: a.astype(BF16)

    x1 = _ffn1(x.reshape(b * s, d), vec(ffn1_norm[0]), bf(ffn1_w_gate[0]), bf(ffn1_w_up[0]),
               bf(ffn1_w_down[0]))

    w_main, w_small = _split_w_in(w_in[0])
    cos_t, sin_a, sin_b = _rope_tables(s)
    qkv, z, small, gates_t, qp, qt, kvc, ks, vst, kw, vwt = _in_proj(
        x1.reshape(b, s, d), vec(mix_norm[0]), w_main, w_small, cos_t, sin_a, sin_b,
        pad_lanes(gdn_a_log[0]), pad_lanes(gdn_dt_bias[0]), gdn_conv_w[0])

    o_gdn = _gdn(qkv, z, small, vec(gdn_out_norm[0]))

    pe_top, pe_bot, w_top, w_bot, w2 = _compress_weights(
        cmp_pe_k[0], cmp_w1_k[0], cmp_w2_k[0], cmp_pe_v[0], cmp_w1_v[0], cmp_w2_v[0])
    kcvc = _compress(kvc, pe_top, pe_bot, w_top, w_bot, w2)

    o_cmp, sel_bias = _nsa_select(qp, kcvc, gates_t)
    o_nsa = _nsa_attention(qt, sel_bias, ks, vst, kw, vwt, gates_t, o_cmp).reshape(b * s, W_NSA_Q)

    out = _out_ffn2(x1, o_gdn.reshape(b * s, W_GDN_Z), o_nsa, bf(w_out[0]), vec(ffn2_norm[0]),
                    bf(ffn2_w_gate[0]), bf(ffn2_w_up[0]), bf(ffn2_w_down[0]), vec(final_norm))
    return out.reshape(b, s, d)
```

```python
import functools

import numpy as np
import jax
import jax.numpy as jnp
from jax import lax
from jax.experimental import pallas as pl
from jax.experimental.pallas import tpu as pltpu

F32 = jnp.float32
BF16 = jnp.bfloat16

D_MODEL = 1024
D_FF = 2816
GDN_HEADS = 4
GDN_DK = 128
GDN_DV = 128
GDN_CONV = 4
GDN_CHUNK = 64
NSA_HEADS = 8
NSA_KV = 2
NSA_HPG = NSA_HEADS // NSA_KV
NSA_DH = 64
CMP_BLOCK = 32
CMP_STRIDE = 16
CMP_HIDDEN = 128
SEL_BLOCK = 64
SEL_TOPK = 16
WINDOW = 512
ROPE_THETA = 500000.0
ROPE_DIM = NSA_DH // 4
EPS = 1e-6
NEG = -1e30
LOG2E = 1.4426950408889634
BIG = 1e30

W_GDN_QKV = GDN_HEADS * (2 * GDN_DK + GDN_DV)
W_GDN_Z = GDN_HEADS * GDN_DV
W_NSA_Q = NSA_HEADS * NSA_DH
W_NSA_KV = 2 * NSA_KV * NSA_DH
W_NSA_GATE = 3 * NSA_HEADS
W_MAIN = W_GDN_QKV + W_GDN_Z + W_NSA_Q + 3 * W_NSA_KV
LANES = 128
GATE_LANE0 = 2 * GDN_HEADS

TOKEN_TILE = 512
MXU_TILE = 256
FF_SPLITS = (0, 6 * MXU_TILE, D_FF)
GDN_HPS = 2
GDN_GROUP = 2 * GDN_CHUNK
NSA_TQ = 256
NSA_TS = 512
NSA_VT_ROWS = 80
NSA_GATE_ROWS = GATE_LANE0 + W_NSA_GATE
VMEM_LIMIT = 56 * 1024 * 1024


def _dot(a, b):
    return jnp.dot(a, b, preferred_element_type=F32)


def _rms(x, w):
    return x * lax.rsqrt(jnp.mean(x * x, axis=-1, keepdims=True) + EPS) * w


def _sigmoid(x):
    return 0.5 * jnp.tanh(0.5 * x) + 0.5


def _silu(x):
    return x * _sigmoid(x)


def _swiglu_half_step(x, nw, wg_ref, wu_ref, wd_ref):
    hb = _rms(x, nw).astype(BF16)
    acc = jnp.zeros_like(x)
    for lo, hi in zip(FF_SPLITS[:-1], FF_SPLITS[1:]):
        sl = slice(lo, hi)
        g = _dot(hb, wg_ref[:, sl])
        u = _dot(hb, wu_ref[:, sl])
        acc = acc + _dot((_silu(g) * u).astype(BF16), wd_ref[sl, :])
    return x + 0.5 * acc


def _ffn_kernel(x_ref, nw_ref, wg_ref, wu_ref, wd_ref, o_ref):
    o_ref[...] = _swiglu_half_step(x_ref[...], nw_ref[...], wg_ref, wu_ref, wd_ref)


def _const_spec(shape):
    nd = len(shape)
    return pl.BlockSpec(shape, lambda *_: (0,) * nd, pipeline_mode=pl.Buffered(1))


def _ffn1(x2d, nw, wg, wu, wd):
    t = x2d.shape[0]
    row = pl.BlockSpec((TOKEN_TILE, D_MODEL), lambda i: (i, 0))
    return pl.pallas_call(
        _ffn_kernel,
        grid=(t // TOKEN_TILE,),
        in_specs=[row, _const_spec((1, D_MODEL)), _const_spec((D_MODEL, D_FF)),
                  _const_spec((D_MODEL, D_FF)), _const_spec((D_FF, D_MODEL))],
        out_specs=row,
        out_shape=jax.ShapeDtypeStruct((t, D_MODEL), F32),
        compiler_params=pltpu.CompilerParams(dimension_semantics=("arbitrary",),
                                             vmem_limit_bytes=VMEM_LIMIT),
        name="ffn1",
    )(x2d, nw, wg, wu, wd)


def _rope(x, cos_t, sin_a, sin_b):
    return (x * cos_t + pltpu.roll(x, LANES - ROPE_DIM // 2, axis=1) * sin_a
            + pltpu.roll(x, ROPE_DIM // 2, axis=1) * sin_b)


def _inproj_kernel(x_ref, xp_ref, nw_ref, wm_ref, ws_ref, cos_ref, sa_ref, sb_ref, alog_ref, dt_ref, conv_ref,
                   qkv_ref, z_ref, small_ref, gt_ref, qp_ref, qt_ref, kvc_ref,
                   ks_ref, vst_ref, kw_ref, vwt_ref, kvc_s):
    hb = _rms(x_ref[...], nw_ref[...]).astype(BF16)
    hb_prev = _rms(xp_ref[...], nw_ref[...]).astype(BF16)
    tm, halo = x_ref.shape[0], xp_ref.shape[0]
    first_tile = pl.program_id(1) == 0
    lane = lax.broadcasted_iota(jnp.int32, (tm, LANES), 1)
    pos = pl.program_id(1) * tm + lax.broadcasted_iota(jnp.int32, (tm, LANES), 0)
    low_half = lane < NSA_DH
    cos_t, sin_a, sin_b = cos_ref[...], sa_ref[...], sb_ref[...]
    scale = NSA_DH ** -0.5
    part_w = GDN_HEADS * GDN_DK

    def halves(slab):
        return slab, pltpu.roll(slab, NSA_DH, axis=1)

    def gdn_part(part, cur):
        lo = part * part_w
        prev = jnp.where(first_tile, 0.0, _dot(hb_prev, wm_ref[:, lo:lo + part_w]))
        ext = jnp.concatenate([prev, cur], axis=0)
        taps = conv_ref[:, lo:lo + part_w]
        conv = cur * taps[GDN_CONV - 1:GDN_CONV, :]
        for sh in range(1, GDN_CONV):
            conv = conv + ext[halo - sh:halo - sh + tm, :] * taps[GDN_CONV - 1 - sh:GDN_CONV - sh, :]
        conv = _silu(conv)
        for hh in range(GDN_HEADS):
            blk = conv[:, hh * GDN_DK:(hh + 1) * GDN_DK]
            if part < 2:
                blk = blk * lax.rsqrt(jnp.sum(blk * blk, axis=-1, keepdims=True) + EPS)
            if part == 0:
                blk = blk * (GDN_DK ** -0.5)
            qkv_ref[:, lo + hh * GDN_DK:lo + (hh + 1) * GDN_DK] = blk

    def gate_z(y):
        z_ref[...] = y

    def nsa_q(y):
        for s in range(W_NSA_Q // LANES):
            slab = y[:, s * LANES:(s + 1) * LANES]
            plain_t = (slab * scale).T
            rot_t = (_rope(slab, cos_t, sin_a, sin_b) * (scale * LOG2E)).T
            for j in range(2):
                qp_ref[2 * s + j] = plain_t[j * NSA_DH:(j + 1) * NSA_DH, :].astype(BF16)
                qt_ref[2 * s + j] = rot_t[j * NSA_DH:(j + 1) * NSA_DH, :].astype(BF16)

    def cmp_kv(y):
        for half in range(W_NSA_KV // LANES):
            kvc_s[half] = y[:, half * LANES:(half + 1) * LANES]
        for r in range(CMP_STRIDE):
            for half in range(W_NSA_KV // LANES):
                dst = r * W_NSA_KV + half * LANES
                kvc_ref[:, dst:dst + LANES] = kvc_s[half, pl.ds(r, tm // CMP_STRIDE, stride=CMP_STRIDE), :]

    def attn_kv(k_ref, vt_ref, selected, y):
        k_pad = jnp.where(lane - NSA_DH == pos // SEL_BLOCK, 1.0, 0.0) if selected else 0.0
        n_tail = vt_ref.shape[1] - NSA_DH
        tail = jnp.where(lax.broadcasted_iota(jnp.int32, (n_tail, tm), 0) == 0, 1.0, 0.0)
        kr = halves(_rope(y[:, :LANES], cos_t, sin_a, sin_b))
        v_t = y[:, LANES:].T
        for j in range(NSA_KV):
            k_ref[j] = jnp.where(low_half, kr[j], k_pad).astype(BF16)
            vt_ref[j] = jnp.concatenate([v_t[j * NSA_DH:(j + 1) * NSA_DH, :], tail], axis=0).astype(BF16)

    def small_proj(sm):
        xg = sm + dt_ref[...]
        softplus = jnp.maximum(xg, 0.0) + jnp.log1p(jnp.exp(-jnp.abs(xg)))
        small = jnp.where(lane < GDN_HEADS, -jnp.exp(alog_ref[...]) * softplus, _sigmoid(sm))
        small_ref[...] = small
        gt_ref[...] = small.T[:gt_ref.shape[0], :]

    stages = [(ws_ref, 0, LANES, small_proj)]
    stages += [(wm_ref, part * part_w, part_w, functools.partial(gdn_part, part))
               for part in range(W_GDN_QKV // part_w)]
    z_lo = W_GDN_QKV
    q_lo = z_lo + W_GDN_Z
    kvc_lo = q_lo + W_NSA_Q
    stages += [(wm_ref, q_lo, W_NSA_Q, nsa_q),
               (wm_ref, kvc_lo + W_NSA_KV, W_NSA_KV, functools.partial(attn_kv, ks_ref, vst_ref, True)),
               (wm_ref, kvc_lo + 2 * W_NSA_KV, W_NSA_KV, functools.partial(attn_kv, kw_ref, vwt_ref, False)),
               (wm_ref, z_lo, W_GDN_Z, gate_z),
               (wm_ref, kvc_lo, W_NSA_KV, cmp_kv)]
    for w_ref, lo, width, epilogue in stages:
        epilogue(_dot(hb, w_ref[:, lo:lo + width]))


def _in_proj(x3d, nw, w_main, w_small, cos_t, sin_a, sin_b, alog_p, dt_p, conv_w):
    b, s, _ = x3d.shape
    tm = TOKEN_TILE
    row = lambda w: pl.BlockSpec((None, tm, w), lambda bi, i: (bi, i, 0))
    heads = lambda n, w: pl.BlockSpec((None, n, tm, w), lambda bi, i: (bi, 0, i, 0))
    heads_t = lambda n, r: pl.BlockSpec((None, n, r, tm), lambda bi, i: (bi, 0, 0, i))
    tab = pl.BlockSpec((tm, LANES), lambda bi, i: (i, 0))
    halo = 2 * GDN_CONV
    prev_rows = pl.BlockSpec((None, halo, D_MODEL), lambda bi, i: (bi, jnp.maximum(i * (tm // halo) - 1, 0), 0))
    bf_heads = lambda n, w: jax.ShapeDtypeStruct((b, n, s, w), BF16)
    out_shape = (
        jax.ShapeDtypeStruct((b, s, W_GDN_QKV), F32),
        jax.ShapeDtypeStruct((b, s, W_GDN_Z), F32),
        jax.ShapeDtypeStruct((b, s, LANES), F32),
        jax.ShapeDtypeStruct((b, NSA_GATE_ROWS, s), F32),
        jax.ShapeDtypeStruct((b, NSA_HEADS, NSA_DH, s), BF16),
        jax.ShapeDtypeStruct((b, NSA_HEADS, NSA_DH, s), BF16),
        jax.ShapeDtypeStruct((b, s // CMP_STRIDE, CMP_STRIDE * W_NSA_KV), F32),
        bf_heads(NSA_KV, LANES),
        jax.ShapeDtypeStruct((b, NSA_KV, NSA_VT_ROWS, s), BF16),
        bf_heads(NSA_KV, LANES),
        jax.ShapeDtypeStruct((b, NSA_KV, NSA_VT_ROWS, s), BF16),
    )
    return pl.pallas_call(
        _inproj_kernel,
        grid=(b, s // tm),
        in_specs=[row(D_MODEL), prev_rows, _const_spec((1, D_MODEL)), _const_spec((D_MODEL, W_MAIN)),
                  _const_spec((D_MODEL, LANES)), tab, tab, tab,
                  _const_spec((1, LANES)), _const_spec((1, LANES)), _const_spec((GDN_CONV, W_GDN_QKV))],
        out_specs=(row(W_GDN_QKV), row(W_GDN_Z), row(LANES),
                   pl.BlockSpec((None, NSA_GATE_ROWS, tm), lambda bi, i: (bi, 0, i)),
                   heads_t(NSA_HEADS, NSA_DH), heads_t(NSA_HEADS, NSA_DH),
                   pl.BlockSpec((None, tm // CMP_STRIDE, CMP_STRIDE * W_NSA_KV), lambda bi, i: (bi, i, 0)),
                   heads(NSA_KV, LANES), heads_t(NSA_KV, NSA_VT_ROWS),
                   heads(NSA_KV, LANES), heads_t(NSA_KV, NSA_VT_ROWS)),
        out_shape=out_shape,
        scratch_shapes=[pltpu.VMEM((W_NSA_KV // LANES, tm, LANES), F32)],
        compiler_params=pltpu.CompilerParams(dimension_semantics=("arbitrary", "arbitrary"),
                                             vmem_limit_bytes=VMEM_LIMIT),
        name="in_proj",
    )(x3d, x3d, nw, w_main, w_small, cos_t, sin_a, sin_b, alog_p, dt_p, conv_w)


def _bmm(a, b):
    return jnp.einsum('nij,njk->nik', a, b, preferred_element_type=F32)


def _bmm_nt(a, b):
    return jnp.einsum('nid,njd->nij', a, b, preferred_element_type=F32)


def _unit_lower_inverse_minus_eye(low, ii, jj):
    mm = lambda a, b: _bmm(a.astype(BF16), b.astype(BF16))
    same16 = (ii // 16) == (jj // 16)
    same32 = (ii // 32) == (jj // 32)
    ld = jnp.where(same16, low, 0.0)
    e = -ld
    p = mm(ld, ld)
    for level in range(3):
        e = e + p + mm(e, p)
        if level < 2:
            p = mm(p, p)
    for within in (same32 & jnp.logical_not(same16), jnp.logical_not(same32)):
        c = jnp.where(within, low, 0.0)
        y = c + mm(e, c)
        e = e - y - mm(y, e)
    return e


def _gdn_kernel(xq_ref, xk_ref, xv_ref, small_ref, o_ref,
                p_s, qn_s, e2_s, mpa_s, na_s, ea_s, qt_s, o0_s, seven_s):
    hp = pl.program_id(1)
    s_len = xq_ref.shape[0]
    c_len = GDN_CHUNK
    grp = GDN_GROUP
    n_chunks = s_len // c_len
    n_grp = s_len // grp
    hd = GDN_DK
    row = lax.broadcasted_iota(jnp.int32, (s_len, LANES), 0)
    lane = lax.broadcasted_iota(jnp.int32, (s_len, LANES), 1)
    pos = row % c_len
    ii = lax.broadcasted_iota(jnp.int32, (grp, grp), 0)
    jj = lax.broadcasted_iota(jnp.int32, (grp, grp), 1)
    same_chunk = (ii // c_len) == (jj // c_len)
    tril = same_chunk & (ii >= jj)
    strict = same_chunk & (ii > jj)
    first = lax.broadcasted_iota(jnp.int32, (grp, hd), 0) < c_len
    small = small_ref[...]
    g3 = lambda a: a.reshape(n_grp, grp, hd)

    for j in range(GDN_HPS):
        h = hp * GDN_HPS + j
        cols = slice(j * hd, (j + 1) * hd)
        q, k, v = xq_ref[:, cols], xk_ref[:, cols], xv_ref[:, cols]

        pick = lambda ln: jnp.broadcast_to(
            jnp.sum(jnp.where(lane == ln, small, 0.0), axis=-1, keepdims=True), (s_len, LANES))
        beta = pick(GDN_HEADS + h)
        g = pick(h)
        sh = 1
        while sh < c_len:
            g = g + jnp.where(pos >= sh, pltpu.roll(g, sh, axis=0), 0.0)
            sh *= 2
        gc = g
        gc3 = gc.reshape(n_chunks, c_len, LANES)
        gl3 = gc3[:, c_len - 1:c_len, :]
        e_gl = jnp.exp(gl3).reshape(n_grp, grp // c_len, 1, LANES)
        dec_a, dec_b = e_gl[:, 0], e_gl[:, 1]
        gl = jnp.broadcast_to(gl3, gc3.shape).reshape(s_len, LANES)
        eg = jnp.exp(gc)
        kb = k * beta
        vb = g3(v * beta)
        kbg = g3(kb * eg)

        k3 = g3(k).astype(BF16)
        kk = _bmm_nt(g3(kb).astype(BF16), k3)
        qk = _bmm_nt(g3(q).astype(BF16), k3)
        gcg = g3(gc)
        diff = gcg - jnp.swapaxes(gcg, 1, 2)
        decay = jnp.where(tril, jnp.exp(jnp.where(tril, diff, 0.0)), 0.0)
        e_b = _unit_lower_inverse_minus_eye(jnp.where(strict, kk * decay, 0.0), ii, jj).astype(BF16)
        u = vb + _bmm(e_b, vb.astype(BF16))
        w = kbg + _bmm(e_b, kbg.astype(BF16))
        qkm = jnp.where(tril, qk * decay, 0.0).astype(BF16)
        r = _bmm(qkm, jnp.concatenate([w, u], axis=-1).astype(BF16))
        qt_s[j] = (g3(q * eg) - r[..., :hd]).astype(BF16)
        o0_s[j] = r[..., hd:]
        kd_t = jnp.swapaxes(g3(k * jnp.exp(gl - gc)), 1, 2).astype(BF16)
        rhs = jnp.concatenate([jnp.where(first, w, 0.0), jnp.where(first, 0.0, w),
                               jnp.where(first, u, 0.0), jnp.where(first, 0.0, u)], axis=-1)
        mn = _bmm(kd_t, rhs.astype(BF16))
        mp_a, mp_b = -mn[..., :hd], -mn[..., hd:2 * hd]
        n_a, n_b = mn[..., 2 * hd:3 * hd], mn[..., 3 * hd:]
        x = _bmm(mp_b.astype(BF16), jnp.concatenate([mp_a, n_a], axis=-1).astype(BF16))
        p_s[j] = (dec_b * mp_a + dec_a * mp_b + x[..., :hd]).astype(BF16)
        qn_s[j] = dec_b * n_a + x[..., hd:] + n_b
        e2_s[j] = dec_a * dec_b
        mpa_s[j] = mp_a.astype(BF16)
        na_s[j] = n_a
        ea_s[j] = dec_a

    def scan(n, states):
        out = []
        for j in range(GDN_HPS):
            sb = states[j].astype(BF16)
            seven_s[j, n] = sb
            out.append(e2_s[j, n] * states[j] + _dot(p_s[j, n], sb) + qn_s[j, n])
        return tuple(out)

    lax.fori_loop(0, n_grp, scan, tuple(jnp.zeros((GDN_DK, GDN_DV), F32) for _ in range(GDN_HPS)))

    for j in range(GDN_HPS):
        cols = slice(j * hd, (j + 1) * hd)
        s_even = seven_s[j]
        s_odd = (ea_s[j] * s_even.astype(F32) + _bmm(mpa_s[j], s_even) + na_s[j]).astype(BF16)
        qt = qt_s[j]
        o = jnp.concatenate([_bmm(qt[:, :c_len], s_even), _bmm(qt[:, c_len:], s_odd)], axis=1) + o0_s[j]
        o_ref[:, cols] = o.reshape(s_len, hd)


def _gdn(qkv, small):
    b, s, _ = qkv.shape
    wid = GDN_HPS * GDN_DK
    n_hp = GDN_HEADS // GDN_HPS
    n_grp = s // GDN_GROUP
    per_pair = lambda rows, dt: pltpu.VMEM((GDN_HPS, n_grp, rows, GDN_DV), dt)
    col = lambda off: pl.BlockSpec((None, s, wid), lambda bi, h: (bi, 0, off + h))
    return pl.pallas_call(
        _gdn_kernel,
        grid=(b, n_hp),
        in_specs=[col(0), col(n_hp), col(2 * n_hp),
                  pl.BlockSpec((None, s, LANES), lambda bi, h: (bi, 0, 0))],
        out_specs=col(0),
        out_shape=jax.ShapeDtypeStruct((b, s, GDN_HEADS * GDN_DV), F32),
        scratch_shapes=[per_pair(GDN_DK, BF16), per_pair(GDN_DK, F32), per_pair(1, F32),
                        per_pair(GDN_DK, BF16), per_pair(GDN_DK, F32), per_pair(1, F32),
                        per_pair(GDN_GROUP, BF16), per_pair(GDN_GROUP, F32), per_pair(GDN_DK, BF16)],
        compiler_params=pltpu.CompilerParams(dimension_semantics=("arbitrary", "arbitrary"),
                                             vmem_limit_bytes=VMEM_LIMIT),
        name="gdn",
    )(qkv, qkv, qkv, small)


def _compress_kernel(t_ref, pet_ref, peb_ref, wt_ref, wb_ref, w2_ref, o_ref):
    t = t_ref[...]
    a = _dot((t + pet_ref[...]).astype(BF16), wt_ref[...])
    bm = _dot((t + peb_ref[...]).astype(BF16), wb_ref[...])
    n_rows = t.shape[0]
    hid = a + pltpu.roll(bm, n_rows - 1, axis=0)
    out = _dot(_silu(hid).astype(BF16), w2_ref[...])
    for j in range(2 * NSA_KV):
        blk = out[:, j * LANES:(j + 1) * LANES]
        o_ref[j] = (blk if j < NSA_KV else blk.T).astype(o_ref.dtype)


def _compress(tm, pe_top, pe_bot, w_top, w_bot, w2):
    b, n_rows, width = tm.shape
    nh = 2 * NSA_KV * CMP_HIDDEN
    return pl.pallas_call(
        _compress_kernel,
        grid=(b,),
        in_specs=[pl.BlockSpec((None, n_rows, width), lambda bi: (bi, 0, 0)),
                  _const_spec((1, width)), _const_spec((1, width)),
                  _const_spec((width, nh)), _const_spec((width, nh)),
                  _const_spec((nh, 2 * NSA_KV * LANES))],
        out_specs=pl.BlockSpec((None, 2 * NSA_KV, n_rows, LANES), lambda bi: (bi, 0, 0, 0)),
        out_shape=jax.ShapeDtypeStruct((b, 2 * NSA_KV, n_rows, LANES), BF16),
        compiler_params=pltpu.CompilerParams(dimension_semantics=("arbitrary",),
                                             vmem_limit_bytes=VMEM_LIMIT),
        name="compress",
    )(tm, pe_top, pe_bot, w_top, w_bot, w2)


def _nsa_select_kernel(qp_ref, kc_ref, vct_ref, gt_ref, ocmp_ref, bias_ref):
    grp = pl.program_id(1)
    ts, hpg, dh = qp_ref.shape[2], NSA_HPG, NSA_DH
    q0 = pl.program_id(2) * ts
    n_cmp = kc_ref.shape[0]
    n_blk = n_cmp * CMP_STRIDE // SEL_BLOCK
    per_head = lambda a: jnp.concatenate([a] * hpg, axis=1)

    t_c = q0 + lax.broadcasted_iota(jnp.int32, (n_cmp, ts), 1)
    n_c = lax.broadcasted_iota(jnp.int32, (n_cmp, ts), 0)
    vis = per_head(jnp.where(n_c * CMP_STRIDE + (CMP_BLOCK - 1) <= t_c, 1.0, 0.0))
    pad = jnp.zeros((LANES - dh, ts), BF16)
    q_t = jnp.concatenate([jnp.concatenate([qp_ref[hh], pad], axis=0) for hh in range(hpg)], axis=1)
    s_cmp = jnp.where(vis > 0.5, _dot(kc_ref[...], q_t), NEG)
    p_cmp = jnp.exp(s_cmp - jnp.max(s_cmp, axis=0, keepdims=True)) * vis
    p_cmp = p_cmp * (1.0 / jnp.maximum(jnp.sum(p_cmp, axis=0, keepdims=True), 1e-30))
    o_cmp = _dot(vct_ref[...], p_cmp.astype(BF16))
    ocmp_ref[...] = jnp.concatenate(
        [gt_ref[pl.ds(GATE_LANE0 + (grp * hpg + hh) * 3, 1), :] * o_cmp[:dh, hh * ts:(hh + 1) * ts]
         for hh in range(hpg)], axis=0)

    p_sum = p_cmp[:, :ts]
    for hh in range(1, hpg):
        p_sum = p_sum + p_cmp[:, hh * ts:(hh + 1) * ts]
    oj = lax.broadcasted_iota(jnp.int32, (n_blk, n_cmp), 0) * SEL_BLOCK
    on = lax.broadcasted_iota(jnp.int32, (n_blk, n_cmp), 1) * CMP_STRIDE
    overlap_t = jnp.where(on < oj + SEL_BLOCK, jnp.where(on + CMP_BLOCK > oj, 1.0, 0.0), 0.0)
    imp_t = jnp.dot(overlap_t, p_sum, precision=lax.Precision.HIGHEST,
                    preferred_element_type=F32)
    blk = lax.broadcasted_iota(jnp.int32, (n_blk, ts), 0)
    ahead = (q0 + lax.broadcasted_iota(jnp.int32, (n_blk, ts), 1)) // SEL_BLOCK - blk
    score = jnp.where(ahead < 0, NEG, jnp.where(blk == 0, BIG, jnp.where(ahead <= 1, BIG, imp_t)))
    rank = jnp.zeros((n_blk, ts), F32)
    for i in range(n_blk):
        ci = jnp.broadcast_to(score[i:i + 1, :], (n_blk, ts))
        tie = jnp.where(blk > i, 1.0, 0.0)
        rank = rank + jnp.where(ci > score, 1.0, jnp.where(ci == score, tie, 0.0))
    bias_ref[...] = jnp.where(rank >= SEL_TOPK, NEG, 0.0).astype(bias_ref.dtype)


def _nsa_select(qp_t, kcvc, gates_t):
    b, _, _, s = qp_t.shape
    ts = NSA_TS
    n_cmp = kcvc.shape[2]
    cspec = lambda off: pl.BlockSpec((None, None, n_cmp, LANES), lambda bi, g, i: (bi, off + g, 0, 0))
    return pl.pallas_call(
        _nsa_select_kernel,
        grid=(b, NSA_KV, s // ts),
        in_specs=[pl.BlockSpec((None, NSA_HPG, NSA_DH, ts), lambda bi, g, i: (bi, g, 0, i)),
                  cspec(0), cspec(NSA_KV),
                  pl.BlockSpec((None, NSA_GATE_ROWS, ts), lambda bi, g, i: (bi, 0, i))],
        out_specs=(pl.BlockSpec((None, None, NSA_HPG * NSA_DH, ts), lambda bi, g, i: (bi, g, 0, i)),
                   pl.BlockSpec((None, None, s // SEL_BLOCK, ts), lambda bi, g, i: (bi, g, 0, i))),
        out_shape=(jax.ShapeDtypeStruct((b, NSA_KV, NSA_HPG * NSA_DH, s), F32),
                   jax.ShapeDtypeStruct((b, NSA_KV, s // SEL_BLOCK, s), BF16)),
        compiler_params=pltpu.CompilerParams(
            dimension_semantics=("arbitrary", "arbitrary", "arbitrary"),
            vmem_limit_bytes=VMEM_LIMIT),
        name="nsa_select",
    )(qp_t, kcvc, kcvc, gates_t)


def _nsa_kernel(qt_ref, bias_ref, ks_ref, vst_ref, kw_ref, vwt_ref, gt_ref, ocmp_ref, o_ref,
                *score_bufs):
    grp = pl.program_id(1)
    tq, hpg, dh = NSA_TQ, NSA_HPG, NSA_DH
    bias = bias_ref[...]
    pad = jnp.zeros((LANES - dh - bias.shape[0], tq), BF16)
    q_aug = jnp.concatenate(
        [jnp.concatenate([qt_ref[hh], bias, pad], axis=0) for hh in range(hpg)], axis=1)
    per_head = lambda a: jnp.concatenate([a] * hpg, axis=1)

    def attend(*streams):
        tasks = [(i, chunk) for i, chunks in enumerate(streams) for chunk in chunks]

        def scores(n):
            k, _, mask = tasks[n][1]
            s_t = _dot(k, q_aug)
            score_bufs[n % 2][:k.shape[0], :] = s_t if mask is None else s_t + mask

        state = [(None, None)] * len(streams)
        scores(0)
        for n, (i, (k, v_t, _)) in enumerate(tasks):
            if n + 1 < len(tasks):
                scores(n + 1)
            s_t = score_bufs[n % 2][:k.shape[0], :]
            m, acc = state[i]
            m_new = jnp.max(s_t, axis=0, keepdims=True)
            if m is not None:
                m_new = jnp.maximum(m, m_new)
            part = _dot(v_t, jnp.exp2(s_t - m_new).astype(BF16))
            state[i] = (m_new, part if m is None else acc * jnp.exp2(m - m_new) + part)
        return [acc[:dh, :] * (1.0 / jnp.maximum(acc[dh:dh + 1, :], 1e-30)) for _, acc in state]

    kt = tq
    ahead = (lax.broadcasted_iota(jnp.int32, (kt, tq), 0)
             - lax.broadcasted_iota(jnp.int32, (kt, tq), 1))

    def step(tile):
        t0 = tile * tq
        slc = []
        for c in range(tile + 1):
            rows = slice(c * kt, (c + 1) * kt)
            causal = per_head(jnp.where(ahead <= 0, 0.0, NEG)) if c == tile else None
            slc.append((ks_ref[rows, :], vst_ref[:, rows], causal))
        win = []
        for k0 in range(max(t0 - WINDOW, 0), t0 + tq, kt):
            back = (t0 - k0) - ahead
            crosses = k0 + kt - 1 > t0 or k0 <= t0 + tq - 1 - WINDOW
            band = per_head(jnp.where(back < 0, NEG, jnp.where(back < WINDOW, 0.0, NEG))) if crosses else None
            win.append((kw_ref[k0:k0 + kt, :], vwt_ref[:, k0:k0 + kt], band))
        o_win, o_slc = attend(win, slc)
        mixed = []
        for hh in range(hpg):
            base = GATE_LANE0 + (grp * hpg + hh) * 3
            head = slice(hh * tq, (hh + 1) * tq)
            mixed.append(gt_ref[pl.ds(base + 1, 1), :] * o_slc[:, head]
                         + gt_ref[pl.ds(base + 2, 1), :] * o_win[:, head])
        o_ref[...] = (ocmp_ref[...] + jnp.concatenate(mixed, axis=0)).T.astype(o_ref.dtype)

    for tile in range(ks_ref.shape[0] // tq):
        pl.when(pl.program_id(2) == tile)(functools.partial(step, tile))


def _nsa_attention(qt, sel_bias, ks, vst, kw, vwt, gates_t, o_cmp):
    b, _, _, s = qt.shape
    tq = NSA_TQ
    kspec = pl.BlockSpec((None, None, s, LANES), lambda bi, g, i: (bi, g, 0, 0))
    vspec = pl.BlockSpec((None, None, NSA_VT_ROWS, s), lambda bi, g, i: (bi, g, 0, 0))
    tok = pl.BlockSpec((None, tq, NSA_HPG * NSA_DH), lambda bi, g, i: (bi, i, g))
    return pl.pallas_call(
        _nsa_kernel,
        grid=(b, NSA_KV, s // tq),
        in_specs=[pl.BlockSpec((None, NSA_HPG, NSA_DH, tq), lambda bi, g, i: (bi, g, 0, i)),
                  pl.BlockSpec((None, None, s // SEL_BLOCK, tq), lambda bi, g, i: (bi, g, 0, i)),
                  kspec, vspec, kspec, vspec,
                  pl.BlockSpec((None, NSA_GATE_ROWS, tq), lambda bi, g, i: (bi, 0, i)),
                  pl.BlockSpec((None, None, NSA_HPG * NSA_DH, tq), lambda bi, g, i: (bi, g, 0, i))],
        out_specs=tok,
        out_shape=jax.ShapeDtypeStruct((b, s, W_NSA_Q), BF16),
        scratch_shapes=[pltpu.VMEM((tq, NSA_HPG * tq), F32)] * 2,
        compiler_params=pltpu.CompilerParams(
            dimension_semantics=("arbitrary", "arbitrary", "arbitrary"),
            vmem_limit_bytes=VMEM_LIMIT),
        name="nsa_attn",
    )(qt, sel_bias, ks, vst, kw, vwt, gates_t, o_cmp)


def _out_ffn2_kernel(x_ref, og_ref, z_ref, on_ref, gn_ref, wo_ref, nw_ref, wg_ref, wu_ref, wd_ref, fw_ref,
                     o_ref):
    half = W_GDN_Z
    gate = _silu(z_ref[...])
    o_gdn = jnp.concatenate(
        [_rms(og_ref[:, hh * GDN_DV:(hh + 1) * GDN_DV], gn_ref[...]) for hh in range(GDN_HEADS)], axis=1)
    x = (x_ref[...] + _dot((o_gdn * gate).astype(BF16), wo_ref[:half, :])
         + _dot(on_ref[...], wo_ref[half:, :]))
    y = _swiglu_half_step(x, nw_ref[...], wg_ref, wu_ref, wd_ref)
    o_ref[...] = _rms(y, fw_ref[...])


def _out_ffn2(x2d, o_gdn, z, o_nsa, gdn_norm, w_out, nw, wg, wu, wd, fw):
    t = x2d.shape[0]
    row = lambda w: pl.BlockSpec((TOKEN_TILE, w), lambda i: (i, 0))
    return pl.pallas_call(
        _out_ffn2_kernel,
        grid=(t // TOKEN_TILE,),
        in_specs=[row(D_MODEL), row(W_GDN_Z), row(W_GDN_Z), row(W_NSA_Q), _const_spec((1, GDN_DV)),
                  _const_spec((D_MODEL, D_MODEL)),
                  _const_spec((1, D_MODEL)), _const_spec((D_MODEL, D_FF)),
                  _const_spec((D_MODEL, D_FF)), _const_spec((D_FF, D_MODEL)),
                  _const_spec((1, D_MODEL))],
        out_specs=row(D_MODEL),
        out_shape=jax.ShapeDtypeStruct((t, D_MODEL), F32),
        compiler_params=pltpu.CompilerParams(dimension_semantics=("arbitrary",),
                                             vmem_limit_bytes=VMEM_LIMIT),
        name="out_ffn2",
    )(x2d, o_gdn, z, o_nsa, gdn_norm, w_out, nw, wg, wu, wd, fw)


def _rope_tables(s_len):
    half = ROPE_DIM // 2
    inv = ROPE_THETA ** (-jnp.arange(half, dtype=F32) / half)
    ang = jnp.arange(s_len, dtype=F32)[:, None] * inv[None, :]
    cos, sin = jnp.cos(ang), jnp.sin(ang)
    pad = jnp.zeros((s_len, NSA_DH - ROPE_DIM), F32)
    zero = jnp.zeros_like(sin)
    cos_h = jnp.concatenate([cos, cos, pad + 1.0], axis=1)
    sa_h = jnp.concatenate([-sin, zero, pad], axis=1)
    sb_h = jnp.concatenate([zero, sin, pad], axis=1)
    rep = lambda a: jnp.tile(a, (1, LANES // NSA_DH))
    return rep(cos_h), rep(sa_h), rep(sb_h)


def _split_w_in(w_in):
    cuts = np.cumsum([W_GDN_QKV, W_GDN_Z, GDN_HEADS, GDN_HEADS, W_NSA_Q,
                      W_NSA_KV, W_NSA_KV, W_NSA_KV]).tolist()
    qkv, z, a, bb, nq, kvc, kvs, kvw, gate = jnp.split(w_in, cuts, axis=1)
    w_main = jnp.concatenate([qkv, z, nq, kvc, kvs, kvw], axis=1).astype(BF16)
    small = jnp.concatenate([a, bb, gate], axis=1)
    w_small = jnp.pad(small, ((0, 0), (0, LANES - small.shape[1]))).astype(BF16)
    return w_main, w_small


def _compress_weights(pe_k, w1_k, w2_k, pe_v, w1_v, w2_v):
    half = CMP_BLOCK // 2
    eye = jnp.eye(2 * NSA_KV, dtype=F32)

    def first_layer(lo):
        w1 = jnp.stack([w1_k, w1_v]).reshape(2, CMP_BLOCK, NSA_DH, CMP_HIDDEN)[:, lo:lo + half]
        w1 = jnp.repeat(w1, NSA_KV, axis=0)
        w = jnp.einsum('crdj,ce->rcdej', w1, eye)
        return w.reshape(half * 2 * NSA_KV * NSA_DH, 2 * NSA_KV * CMP_HIDDEN).astype(BF16)

    def pe_row(lo):
        pe = jnp.stack([pe_k, pe_v])[:, lo:lo + half]
        pe = jnp.repeat(pe, NSA_KV, axis=0)
        return pe.transpose(1, 0, 2).reshape(1, half * 2 * NSA_KV * NSA_DH)

    w2 = jnp.repeat(jnp.stack([w2_k, w2_v]), NSA_KV, axis=0)
    w2 = jnp.pad(w2, ((0, 0), (0, 0), (0, LANES - NSA_DH)))
    w2 = jnp.einsum('cjd,ce->cjed', w2, eye).reshape(2 * NSA_KV * CMP_HIDDEN, 2 * NSA_KV * LANES)
    return pe_row(0), pe_row(half), first_layer(0), first_layer(half), w2.astype(BF16)


def kernel(x, ffn1_norm, ffn1_w_gate, ffn1_w_up, ffn1_w_down, mix_norm, w_in, gdn_conv_w, gdn_a_log,
           gdn_dt_bias, gdn_out_norm, cmp_pe_k, cmp_w1_k, cmp_w2_k, cmp_pe_v, cmp_w1_v, cmp_w2_v,
           w_out, ffn2_norm, ffn2_w_gate, ffn2_w_up, ffn2_w_down, final_norm):
    b, s, d = x.shape
    assert d == D_MODEL and s % TOKEN_TILE == 0 and ffn1_norm.shape[0] == 1
    vec = lambda a: a.reshape(1, -1)
    pad_lanes = lambda a: jnp.pad(a.reshape(1, -1), ((0, 0), (0, LANES - a.size)))
    bf = lambda a: a.astype(BF16)

    x1 = _ffn1(x.reshape(b * s, d), vec(ffn1_norm[0]), bf(ffn1_w_gate[0]), bf(ffn1_w_up[0]),
               bf(ffn1_w_down[0]))

    w_main, w_small = _split_w_in(w_in[0])
    cos_t, sin_a, sin_b = _rope_tables(s)
    qkv, z, small, gates_t, qp, qt, kvc, ks, vst, kw, vwt = _in_proj(
        x1.reshape(b, s, d), vec(mix_norm[0]), w_main, w_small, cos_t, sin_a, sin_b,
        pad_lanes(gdn_a_log[0]), pad_lanes(gdn_dt_bias[0]), gdn_conv_w[0])

    o_gdn = _gdn(qkv, small)

    pe_top, pe_bot, w_top, w_bot, w2 = _compress_weights(
        cmp_pe_k[0], cmp_w1_k[0], cmp_w2_k[0], cmp_pe_v[0], cmp_w1_v[0], cmp_w2_v[0])
    kcvc = _compress(kvc, pe_top, pe_bot, w_top, w_bot, w2)

    o_cmp, sel_bias = _nsa_select(qp, kcvc, gates_t)
    o_nsa = _nsa_attention(qt, sel_bias, ks, vst, kw, vwt, gates_t, o_cmp).reshape(b * s, W_NSA_Q)

    out = _out_ffn2(x1, o_gdn.reshape(b * s, W_GDN_Z), z.reshape(b * s, W_GDN_Z), o_nsa,
                    vec(gdn_out_norm[0]), bf(w_out[0]), vec(ffn2_norm[0]),
                    bf(ffn2_w_gate[0]), bf(ffn2_w_up[0]), bf(ffn2_w_down[0]), vec(final_norm))
    return out.reshape(b, s, d)
```

```python
import functools

import numpy as np
import jax
import jax.numpy as jnp
from jax import lax
from jax.experimental import pallas as pl
from jax.experimental.pallas import tpu as pltpu

F32 = jnp.float32
BF16 = jnp.bfloat16

D_MODEL = 1024
D_FF = 2816
GDN_HEADS = 4
GDN_DK = 128
GDN_DV = 128
GDN_CONV = 4
GDN_CHUNK = 64
NSA_HEADS = 8
NSA_KV = 2
NSA_HPG = NSA_HEADS // NSA_KV
NSA_DH = 64
CMP_BLOCK = 32
CMP_STRIDE = 16
CMP_HIDDEN = 128
SEL_BLOCK = 64
SEL_TOPK = 16
WINDOW = 512
ROPE_THETA = 500000.0
ROPE_DIM = NSA_DH // 4
EPS = 1e-6
NEG = -1e30
LOG2E = 1.4426950408889634
BIG = 1e30

W_GDN_QKV = GDN_HEADS * (2 * GDN_DK + GDN_DV)
W_GDN_Z = GDN_HEADS * GDN_DV
W_NSA_Q = NSA_HEADS * NSA_DH
W_NSA_KV = 2 * NSA_KV * NSA_DH
W_NSA_GATE = 3 * NSA_HEADS
W_MAIN = W_GDN_QKV + W_GDN_Z + W_NSA_Q + 3 * W_NSA_KV
LANES = 128
GATE_LANE0 = 2 * GDN_HEADS

TOKEN_TILE = 512
MXU_TILE = 256
FF_SPLITS = (0, 6 * MXU_TILE, D_FF)
GDN_HPS = 2
GDN_GROUP = 2 * GDN_CHUNK
NSA_TQ = 256
NSA_TS = 512
NSA_VT_ROWS = 80
NSA_GATE_ROWS = GATE_LANE0 + W_NSA_GATE
VMEM_LIMIT = 56 * 1024 * 1024


def _dot(a, b):
    return jnp.dot(a, b, preferred_element_type=F32)


def _rms(x, w):
    return x * lax.rsqrt(jnp.mean(x * x, axis=-1, keepdims=True) + EPS) * w


def _sigmoid(x):
    return 0.5 * jnp.tanh(0.5 * x) + 0.5


def _silu(x):
    return x * _sigmoid(x)


def _swiglu_half_step(x, nw, wg_ref, wu_ref, wd_ref):
    hb = _rms(x, nw).astype(BF16)
    acc = jnp.zeros_like(x)
    for lo, hi in zip(FF_SPLITS[:-1], FF_SPLITS[1:]):
        sl = slice(lo, hi)
        g = _dot(hb, wg_ref[:, sl])
        u = _dot(hb, wu_ref[:, sl])
        acc = acc + _dot((_silu(g) * u).astype(BF16), wd_ref[sl, :])
    return x + 0.5 * acc


def _ffn_kernel(x_ref, nw_ref, wg_ref, wu_ref, wd_ref, o_ref):
    o_ref[...] = _swiglu_half_step(x_ref[...], nw_ref[...], wg_ref, wu_ref, wd_ref)


def _const_spec(shape):
    nd = len(shape)
    return pl.BlockSpec(shape, lambda *_: (0,) * nd, pipeline_mode=pl.Buffered(1))


def _ffn1(x2d, nw, wg, wu, wd):
    t = x2d.shape[0]
    row = pl.BlockSpec((TOKEN_TILE, D_MODEL), lambda i: (i, 0))
    return pl.pallas_call(
        _ffn_kernel,
        grid=(t // TOKEN_TILE,),
        in_specs=[row, _const_spec((1, D_MODEL)), _const_spec((D_MODEL, D_FF)),
                  _const_spec((D_MODEL, D_FF)), _const_spec((D_FF, D_MODEL))],
        out_specs=row,
        out_shape=jax.ShapeDtypeStruct((t, D_MODEL), F32),
        compiler_params=pltpu.CompilerParams(dimension_semantics=("arbitrary",),
                                             vmem_limit_bytes=VMEM_LIMIT),
        name="ffn1",
    )(x2d, nw, wg, wu, wd)


def _rope(x, cos_t, sin_a, sin_b):
    return (x * cos_t + pltpu.roll(x, LANES - ROPE_DIM // 2, axis=1) * sin_a
            + pltpu.roll(x, ROPE_DIM // 2, axis=1) * sin_b)


def _inproj_kernel(x_ref, xp_ref, nw_ref, wm_ref, ws_ref, cos_ref, sa_ref, sb_ref, alog_ref, dt_ref, conv_ref,
                   qkv_ref, z_ref, small_ref, gt_ref, qp_ref, qt_ref, kvc_ref,
                   ks_ref, vst_ref, kw_ref, vwt_ref, kvc_s):
    hb = _rms(x_ref[...], nw_ref[...]).astype(BF16)
    hb_prev = _rms(xp_ref[...], nw_ref[...]).astype(BF16)
    tm, halo = x_ref.shape[0], xp_ref.shape[0]
    first_tile = pl.program_id(1) == 0
    lane = lax.broadcasted_iota(jnp.int32, (tm, LANES), 1)
    pos = pl.program_id(1) * tm + lax.broadcasted_iota(jnp.int32, (tm, LANES), 0)
    low_half = lane < NSA_DH
    cos_t, sin_a, sin_b = cos_ref[...], sa_ref[...], sb_ref[...]
    scale = NSA_DH ** -0.5
    part_w = GDN_HEADS * GDN_DK

    def halves(slab):
        return slab, pltpu.roll(slab, NSA_DH, axis=1)

    def gdn_part(part, cur):
        lo = part * part_w
        prev = jnp.where(first_tile, 0.0, _dot(hb_prev, wm_ref[:, lo:lo + part_w]))
        ext = jnp.concatenate([prev, cur], axis=0)
        taps = conv_ref[:, lo:lo + part_w]
        conv = cur * taps[GDN_CONV - 1:GDN_CONV, :]
        for sh in range(1, GDN_CONV):
            conv = conv + ext[halo - sh:halo - sh + tm, :] * taps[GDN_CONV - 1 - sh:GDN_CONV - sh, :]
        conv = _silu(conv)
        for hh in range(GDN_HEADS):
            blk = conv[:, hh * GDN_DK:(hh + 1) * GDN_DK]
            if part < 2:
                blk = blk * lax.rsqrt(jnp.sum(blk * blk, axis=-1, keepdims=True) + EPS)
            if part == 0:
                blk = blk * (GDN_DK ** -0.5)
            qkv_ref[:, lo + hh * GDN_DK:lo + (hh + 1) * GDN_DK] = blk

    def gate_z(y):
        z_ref[...] = y

    def nsa_q(y):
        for s in range(W_NSA_Q // LANES):
            slab = y[:, s * LANES:(s + 1) * LANES]
            plain_t = (slab * scale).T
            rot_t = (_rope(slab, cos_t, sin_a, sin_b) * (scale * LOG2E)).T
            for j in range(2):
                qp_ref[2 * s + j] = plain_t[j * NSA_DH:(j + 1) * NSA_DH, :].astype(BF16)
                qt_ref[2 * s + j] = rot_t[j * NSA_DH:(j + 1) * NSA_DH, :].astype(BF16)

    def cmp_kv(y):
        for half in range(W_NSA_KV // LANES):
            kvc_s[half] = y[:, half * LANES:(half + 1) * LANES]
        for r in range(CMP_STRIDE):
            for half in range(W_NSA_KV // LANES):
                dst = r * W_NSA_KV + half * LANES
                kvc_ref[:, dst:dst + LANES] = kvc_s[half, pl.ds(r, tm // CMP_STRIDE, stride=CMP_STRIDE), :]

    def attn_kv(k_ref, vt_ref, selected, y):
        k_pad = jnp.where(lane - NSA_DH == pos // SEL_BLOCK, 1.0, 0.0) if selected else 0.0
        n_tail = vt_ref.shape[1] - NSA_DH
        tail = jnp.where(lax.broadcasted_iota(jnp.int32, (n_tail, tm), 0) == 0, 1.0, 0.0)
        kr = halves(_rope(y[:, :LANES], cos_t, sin_a, sin_b))
        v_t = y[:, LANES:].T
        for j in range(NSA_KV):
            k_ref[j] = jnp.where(low_half, kr[j], k_pad).astype(BF16)
            vt_ref[j] = jnp.concatenate([v_t[j * NSA_DH:(j + 1) * NSA_DH, :], tail], axis=0).astype(BF16)

    def small_proj(sm):
        xg = sm + dt_ref[...]
        softplus = jnp.maximum(xg, 0.0) + jnp.log1p(jnp.exp(-jnp.abs(xg)))
        small = jnp.where(lane < GDN_HEADS, -jnp.exp(alog_ref[...]) * softplus, _sigmoid(sm))
        small_ref[...] = small
        gt_ref[...] = small.T[:gt_ref.shape[0], :]

    stages = [(ws_ref, 0, LANES, small_proj)]
    stages += [(wm_ref, part * part_w, part_w, functools.partial(gdn_part, part))
               for part in range(W_GDN_QKV // part_w)]
    z_lo = W_GDN_QKV
    q_lo = z_lo + W_GDN_Z
    kvc_lo = q_lo + W_NSA_Q
    stages += [(wm_ref, q_lo, W_NSA_Q, nsa_q),
               (wm_ref, kvc_lo + W_NSA_KV, W_NSA_KV, functools.partial(attn_kv, ks_ref, vst_ref, True)),
               (wm_ref, kvc_lo + 2 * W_NSA_KV, W_NSA_KV, functools.partial(attn_kv, kw_ref, vwt_ref, False)),
               (wm_ref, z_lo, W_GDN_Z, gate_z),
               (wm_ref, kvc_lo, W_NSA_KV, cmp_kv)]
    for w_ref, lo, width, epilogue in stages:
        epilogue(_dot(hb, w_ref[:, lo:lo + width]))


def _in_proj(x3d, nw, w_main, w_small, cos_t, sin_a, sin_b, alog_p, dt_p, conv_w):
    b, s, _ = x3d.shape
    tm = TOKEN_TILE
    row = lambda w: pl.BlockSpec((None, tm, w), lambda bi, i: (bi, i, 0))
    heads = lambda n, w: pl.BlockSpec((None, n, tm, w), lambda bi, i: (bi, 0, i, 0))
    heads_t = lambda n, r: pl.BlockSpec((None, n, r, tm), lambda bi, i: (bi, 0, 0, i))
    tab = pl.BlockSpec((tm, LANES), lambda bi, i: (i, 0))
    halo = 2 * GDN_CONV
    prev_rows = pl.BlockSpec((None, halo, D_MODEL), lambda bi, i: (bi, jnp.maximum(i * (tm // halo) - 1, 0), 0))
    bf_heads = lambda n, w: jax.ShapeDtypeStruct((b, n, s, w), BF16)
    out_shape = (
        jax.ShapeDtypeStruct((b, s, W_GDN_QKV), F32),
        jax.ShapeDtypeStruct((b, s, W_GDN_Z), F32),
        jax.ShapeDtypeStruct((b, s, LANES), F32),
        jax.ShapeDtypeStruct((b, NSA_GATE_ROWS, s), F32),
        jax.ShapeDtypeStruct((b, NSA_HEADS, NSA_DH, s), BF16),
        jax.ShapeDtypeStruct((b, NSA_HEADS, NSA_DH, s), BF16),
        jax.ShapeDtypeStruct((b, s // CMP_STRIDE, CMP_STRIDE * W_NSA_KV), F32),
        bf_heads(NSA_KV, LANES),
        jax.ShapeDtypeStruct((b, NSA_KV, NSA_VT_ROWS, s), BF16),
        bf_heads(NSA_KV, LANES),
        jax.ShapeDtypeStruct((b, NSA_KV, NSA_VT_ROWS, s), BF16),
    )
    return pl.pallas_call(
        _inproj_kernel,
        grid=(b, s // tm),
        in_specs=[row(D_MODEL), prev_rows, _const_spec((1, D_MODEL)), _const_spec((D_MODEL, W_MAIN)),
                  _const_spec((D_MODEL, LANES)), tab, tab, tab,
                  _const_spec((1, LANES)), _const_spec((1, LANES)), _const_spec((GDN_CONV, W_GDN_QKV))],
        out_specs=(row(W_GDN_QKV), row(W_GDN_Z), row(LANES),
                   pl.BlockSpec((None, NSA_GATE_ROWS, tm), lambda bi, i: (bi, 0, i)),
                   heads_t(NSA_HEADS, NSA_DH), heads_t(NSA_HEADS, NSA_DH),
                   pl.BlockSpec((None, tm // CMP_STRIDE, CMP_STRIDE * W_NSA_KV), lambda bi, i: (bi, i, 0)),
                   heads(NSA_KV, LANES), heads_t(NSA_KV, NSA_VT_ROWS),
                   heads(NSA_KV, LANES), heads_t(NSA_KV, NSA_VT_ROWS)),
        out_shape=out_shape,
        scratch_shapes=[pltpu.VMEM((W_NSA_KV // LANES, tm, LANES), F32)],
        compiler_params=pltpu.CompilerParams(dimension_semantics=("arbitrary", "arbitrary"),
                                             vmem_limit_bytes=VMEM_LIMIT),
        name="in_proj",
    )(x3d, x3d, nw, w_main, w_small, cos_t, sin_a, sin_b, alog_p, dt_p, conv_w)


def _bmm(a, b):
    return jnp.einsum('nij,njk->nik', a, b, preferred_element_type=F32)


def _bmm_nt(a, b):
    return jnp.einsum('nid,njd->nij', a, b, preferred_element_type=F32)


def _unit_lower_inverse_minus_eye(low, ii, jj):
    mm = lambda a, b: _bmm(a.astype(BF16), b.astype(BF16))
    same16 = (ii // 16) == (jj // 16)
    same32 = (ii // 32) == (jj // 32)
    ld = jnp.where(same16, low, 0.0)
    e = -ld
    p = mm(ld, ld)
    for level in range(3):
        e = e + p + mm(e, p)
        if level < 2:
            p = mm(p, p)
    for within in (same32 & jnp.logical_not(same16), jnp.logical_not(same32)):
        c = jnp.where(within, low, 0.0)
        y = c + mm(e, c)
        e = e - y - mm(y, e)
    return e


def _gdn_kernel(xq_ref, xk_ref, xv_ref, small_ref, o_ref,
                p_s, qn_s, e2_s, mpa_s, na_s, ea_s, qt_s, o0_s, seven_s):
    hp = pl.program_id(1)
    s_len = xq_ref.shape[0]
    c_len = GDN_CHUNK
    grp = GDN_GROUP
    n_chunks = s_len // c_len
    n_grp = s_len // grp
    hd = GDN_DK
    row = lax.broadcasted_iota(jnp.int32, (s_len, LANES), 0)
    lane = lax.broadcasted_iota(jnp.int32, (s_len, LANES), 1)
    pos = row % c_len
    ii = lax.broadcasted_iota(jnp.int32, (grp, grp), 0)
    jj = lax.broadcasted_iota(jnp.int32, (grp, grp), 1)
    same_chunk = (ii // c_len) == (jj // c_len)
    tril = same_chunk & (ii >= jj)
    strict = same_chunk & (ii > jj)
    first = lax.broadcasted_iota(jnp.int32, (grp, hd), 0) < c_len
    small = small_ref[...]
    g3 = lambda a: a.reshape(n_grp, grp, hd)

    for j in range(GDN_HPS):
        h = hp * GDN_HPS + j
        cols = slice(j * hd, (j + 1) * hd)
        q, k, v = xq_ref[:, cols], xk_ref[:, cols], xv_ref[:, cols]

        pick = lambda ln: jnp.broadcast_to(
            jnp.sum(jnp.where(lane == ln, small, 0.0), axis=-1, keepdims=True), (s_len, LANES))
        beta = pick(GDN_HEADS + h)
        g = pick(h)
        sh = 1
        while sh < c_len:
            g = g + jnp.where(pos >= sh, pltpu.roll(g, sh, axis=0), 0.0)
            sh *= 2
        gc = g
        gc3 = gc.reshape(n_chunks, c_len, LANES)
        gl3 = gc3[:, c_len - 1:c_len, :]
        e_gl = jnp.exp(gl3).reshape(n_grp, grp // c_len, 1, LANES)
        dec_a, dec_b = e_gl[:, 0], e_gl[:, 1]
        gl = jnp.broadcast_to(gl3, gc3.shape).reshape(s_len, LANES)
        eg = jnp.exp(gc)
        kb = k * beta
        vb = g3(v * beta)
        kbg = g3(kb * eg)

        k3 = g3(k).astype(BF16)
        kk = _bmm_nt(g3(kb).astype(BF16), k3)
        qk = _bmm_nt(g3(q).astype(BF16), k3)
        gcg = g3(gc)
        diff = gcg - jnp.swapaxes(gcg, 1, 2)
        decay = jnp.where(tril, jnp.exp(jnp.where(tril, diff, 0.0)), 0.0)
        e_b = _unit_lower_inverse_minus_eye(jnp.where(strict, kk * decay, 0.0), ii, jj).astype(BF16)
        u = vb + _bmm(e_b, vb.astype(BF16))
        w = kbg + _bmm(e_b, kbg.astype(BF16))
        qkm = jnp.where(tril, qk * decay, 0.0).astype(BF16)
        r = _bmm(qkm, jnp.concatenate([w, u], axis=-1).astype(BF16))
        qt_s[j] = (g3(q * eg) - r[..., :hd]).astype(BF16)
        o0_s[j] = r[..., hd:]
        kd_t = jnp.swapaxes(g3(k * jnp.exp(gl - gc)), 1, 2).astype(BF16)
        rhs = jnp.concatenate([jnp.where(first, w, 0.0), jnp.where(first, 0.0, w),
                               jnp.where(first, u, 0.0), jnp.where(first, 0.0, u)], axis=-1)
        mn = _bmm(kd_t, rhs.astype(BF16))
        mp_a, mp_b = -mn[..., :hd], -mn[..., hd:2 * hd]
        n_a, n_b = mn[..., 2 * hd:3 * hd], mn[..., 3 * hd:]
        x = _bmm(mp_b.astype(BF16), jnp.concatenate([mp_a, n_a], axis=-1).astype(BF16))
        p_s[j] = (dec_b * mp_a + dec_a * mp_b + x[..., :hd]).astype(BF16)
        qn_s[j] = dec_b * n_a + x[..., hd:] + n_b
        e2_s[j] = dec_a * dec_b
        mpa_s[j] = mp_a.astype(BF16)
        na_s[j] = n_a
        ea_s[j] = dec_a

    def scan(n, states):
        out = []
        for j in range(GDN_HPS):
            sb = states[j].astype(BF16)
            seven_s[j, n] = sb
            out.append(e2_s[j, n] * states[j] + _dot(p_s[j, n], sb) + qn_s[j, n])
        return tuple(out)

    lax.fori_loop(0, n_grp, scan, tuple(jnp.zeros((GDN_DK, GDN_DV), F32) for _ in range(GDN_HPS)))

    for j in range(GDN_HPS):
        cols = slice(j * hd, (j + 1) * hd)
        s_even = seven_s[j]
        s_odd = (ea_s[j] * s_even.astype(F32) + _bmm(mpa_s[j], s_even) + na_s[j]).astype(BF16)
        qt = qt_s[j]
        o = jnp.concatenate([_bmm(qt[:, :c_len], s_even), _bmm(qt[:, c_len:], s_odd)], axis=1) + o0_s[j]
        o_ref[:, cols] = o.reshape(s_len, hd)


def _gdn(qkv, small):
    b, s, _ = qkv.shape
    wid = GDN_HPS * GDN_DK
    n_hp = GDN_HEADS // GDN_HPS
    n_grp = s // GDN_GROUP
    per_pair = lambda rows, dt: pltpu.VMEM((GDN_HPS, n_grp, rows, GDN_DV), dt)
    col = lambda off: pl.BlockSpec((None, s, wid), lambda bi, h: (bi, 0, off + h))
    return pl.pallas_call(
        _gdn_kernel,
        grid=(b, n_hp),
        in_specs=[col(0), col(n_hp), col(2 * n_hp),
                  pl.BlockSpec((None, s, LANES), lambda bi, h: (bi, 0, 0))],
        out_specs=col(0),
        out_shape=jax.ShapeDtypeStruct((b, s, GDN_HEADS * GDN_DV), F32),
        scratch_shapes=[per_pair(GDN_DK, BF16), per_pair(GDN_DK, F32), per_pair(1, F32),
                        per_pair(GDN_DK, BF16), per_pair(GDN_DK, F32), per_pair(1, F32),
                        per_pair(GDN_GROUP, BF16), per_pair(GDN_GROUP, F32), per_pair(GDN_DK, BF16)],
        compiler_params=pltpu.CompilerParams(dimension_semantics=("arbitrary", "arbitrary"),
                                             vmem_limit_bytes=VMEM_LIMIT),
        name="gdn",
    )(qkv, qkv, qkv, small)


def _compress_kernel(t_ref, pet_ref, peb_ref, wt_ref, wb_ref, w2_ref, o_ref):
    t = t_ref[...]
    a = _dot((t + pet_ref[...]).astype(BF16), wt_ref[...])
    bm = _dot((t + peb_ref[...]).astype(BF16), wb_ref[...])
    n_rows = t.shape[0]
    hid = a + pltpu.roll(bm, n_rows - 1, axis=0)
    out = _dot(_silu(hid).astype(BF16), w2_ref[...])
    for j in range(2 * NSA_KV):
        blk = out[:, j * LANES:(j + 1) * LANES]
        o_ref[j] = (blk if j < NSA_KV else blk.T).astype(o_ref.dtype)


def _compress(tm, pe_top, pe_bot, w_top, w_bot, w2):
    b, n_rows, width = tm.shape
    nh = 2 * NSA_KV * CMP_HIDDEN
    return pl.pallas_call(
        _compress_kernel,
        grid=(b,),
        in_specs=[pl.BlockSpec((None, n_rows, width), lambda bi: (bi, 0, 0)),
                  _const_spec((1, width)), _const_spec((1, width)),
                  _const_spec((width, nh)), _const_spec((width, nh)),
                  _const_spec((nh, 2 * NSA_KV * LANES))],
        out_specs=pl.BlockSpec((None, 2 * NSA_KV, n_rows, LANES), lambda bi: (bi, 0, 0, 0)),
        out_shape=jax.ShapeDtypeStruct((b, 2 * NSA_KV, n_rows, LANES), BF16),
        compiler_params=pltpu.CompilerParams(dimension_semantics=("arbitrary",),
                                             vmem_limit_bytes=VMEM_LIMIT),
        name="compress",
    )(tm, pe_top, pe_bot, w_top, w_bot, w2)


def _nsa_select_kernel(qp_ref, kc_ref, vct_ref, gt_ref, ocmp_ref, bias_ref):
    grp = pl.program_id(1)
    ts, hpg, dh = qp_ref.shape[2], NSA_HPG, NSA_DH
    q0 = pl.program_id(2) * ts
    n_cmp = kc_ref.shape[0]
    n_blk = n_cmp * CMP_STRIDE // SEL_BLOCK
    per_head = lambda a: jnp.concatenate([a] * hpg, axis=1)

    t_c = q0 + lax.broadcasted_iota(jnp.int32, (n_cmp, ts), 1)
    n_c = lax.broadcasted_iota(jnp.int32, (n_cmp, ts), 0)
    vis = per_head(jnp.where(n_c * CMP_STRIDE + (CMP_BLOCK - 1) <= t_c, 1.0, 0.0))
    pad = jnp.zeros((LANES - dh, ts), BF16)
    q_t = jnp.concatenate([jnp.concatenate([qp_ref[hh], pad], axis=0) for hh in range(hpg)], axis=1)
    s_cmp = jnp.where(vis > 0.5, _dot(kc_ref[...], q_t), NEG)
    p_cmp = jnp.exp(s_cmp - jnp.max(s_cmp, axis=0, keepdims=True)) * vis
    p_cmp = p_cmp * (1.0 / jnp.maximum(jnp.sum(p_cmp, axis=0, keepdims=True), 1e-30))
    o_cmp = _dot(vct_ref[...], p_cmp.astype(BF16))
    ocmp_ref[...] = jnp.concatenate(
        [gt_ref[pl.ds(GATE_LANE0 + (grp * hpg + hh) * 3, 1), :] * o_cmp[:dh, hh * ts:(hh + 1) * ts]
         for hh in range(hpg)], axis=0)

    p_sum = p_cmp[:, :ts]
    for hh in range(1, hpg):
        p_sum = p_sum + p_cmp[:, hh * ts:(hh + 1) * ts]
    oj = lax.broadcasted_iota(jnp.int32, (n_blk, n_cmp), 0) * SEL_BLOCK
    on = lax.broadcasted_iota(jnp.int32, (n_blk, n_cmp), 1) * CMP_STRIDE
    overlap_t = jnp.where(on < oj + SEL_BLOCK, jnp.where(on + CMP_BLOCK > oj, 1.0, 0.0), 0.0)
    imp_t = jnp.dot(overlap_t, p_sum, precision=lax.Precision.HIGHEST,
                    preferred_element_type=F32)
    blk = lax.broadcasted_iota(jnp.int32, (n_blk, ts), 0)
    ahead = (q0 + lax.broadcasted_iota(jnp.int32, (n_blk, ts), 1)) // SEL_BLOCK - blk
    score = jnp.where(ahead < 0, NEG, jnp.where(blk == 0, BIG, jnp.where(ahead <= 1, BIG, imp_t)))

    def rank_and_select(n_live):
        rank = jnp.zeros((n_blk, ts), F32)
        for i in range(n_live):
            ci = jnp.broadcast_to(score[i:i + 1, :], (n_blk, ts))
            tie = jnp.where(blk > i, 1.0, 0.0)
            rank = rank + jnp.where(ci > score, 1.0, jnp.where(ci == score, tie, 0.0))
        bias_ref[...] = jnp.where(rank >= SEL_TOPK, NEG, 0.0).astype(bias_ref.dtype)

    for tile in range(n_blk * SEL_BLOCK // ts):
        pl.when(pl.program_id(2) == tile)(
            functools.partial(rank_and_select, (tile + 1) * ts // SEL_BLOCK))


def _nsa_select(qp_t, kcvc, gates_t):
    b, _, _, s = qp_t.shape
    ts = NSA_TS
    n_cmp = kcvc.shape[2]
    cspec = lambda off: pl.BlockSpec((None, None, n_cmp, LANES), lambda bi, g, i: (bi, off + g, 0, 0))
    return pl.pallas_call(
        _nsa_select_kernel,
        grid=(b, NSA_KV, s // ts),
        in_specs=[pl.BlockSpec((None, NSA_HPG, NSA_DH, ts), lambda bi, g, i: (bi, g, 0, i)),
                  cspec(0), cspec(NSA_KV),
                  pl.BlockSpec((None, NSA_GATE_ROWS, ts), lambda bi, g, i: (bi, 0, i))],
        out_specs=(pl.BlockSpec((None, None, NSA_HPG * NSA_DH, ts), lambda bi, g, i: (bi, g, 0, i)),
                   pl.BlockSpec((None, None, s // SEL_BLOCK, ts), lambda bi, g, i: (bi, g, 0, i))),
        out_shape=(jax.ShapeDtypeStruct((b, NSA_KV, NSA_HPG * NSA_DH, s), F32),
                   jax.ShapeDtypeStruct((b, NSA_KV, s // SEL_BLOCK, s), BF16)),
        compiler_params=pltpu.CompilerParams(
            dimension_semantics=("arbitrary", "arbitrary", "arbitrary"),
            vmem_limit_bytes=VMEM_LIMIT),
        name="nsa_select",
    )(qp_t, kcvc, kcvc, gates_t)


def _nsa_kernel(qt_ref, bias_ref, ks_ref, vst_ref, kw_ref, vwt_ref, gt_ref, ocmp_ref, o_ref,
                *score_bufs):
    grp = pl.program_id(1)
    tq, hpg, dh = NSA_TQ, NSA_HPG, NSA_DH
    bias = bias_ref[...]
    pad = jnp.zeros((LANES - dh - bias.shape[0], tq), BF16)
    q_aug = jnp.concatenate(
        [jnp.concatenate([qt_ref[hh], bias, pad], axis=0) for hh in range(hpg)], axis=1)
    per_head = lambda a: jnp.concatenate([a] * hpg, axis=1)

    def attend(*streams):
        tasks = [(i, chunk) for i, chunks in enumerate(streams) for chunk in chunks]

        def scores(n):
            k, _, mask = tasks[n][1]
            s_t = _dot(k, q_aug)
            score_bufs[n % 2][:k.shape[0], :] = s_t if mask is None else s_t + mask

        state = [(None, None)] * len(streams)
        scores(0)
        for n, (i, (k, v_t, _)) in enumerate(tasks):
            if n + 1 < len(tasks):
                scores(n + 1)
            s_t = score_bufs[n % 2][:k.shape[0], :]
            m, acc = state[i]
            m_new = jnp.max(s_t, axis=0, keepdims=True)
            if m is not None:
                m_new = jnp.maximum(m, m_new)
            part = _dot(v_t, jnp.exp2(s_t - m_new).astype(BF16))
            state[i] = (m_new, part if m is None else acc * jnp.exp2(m - m_new) + part)
        return [acc[:dh, :] * (1.0 / jnp.maximum(acc[dh:dh + 1, :], 1e-30)) for _, acc in state]

    kt = tq
    ahead = (lax.broadcasted_iota(jnp.int32, (kt, tq), 0)
             - lax.broadcasted_iota(jnp.int32, (kt, tq), 1))

    def step(tile):
        t0 = tile * tq
        slc = []
        for c in range(tile + 1):
            rows = slice(c * kt, (c + 1) * kt)
            causal = per_head(jnp.where(ahead <= 0, 0.0, NEG)) if c == tile else None
            slc.append((ks_ref[rows, :], vst_ref[:, rows], causal))
        win = []
        for k0 in range(max(t0 - WINDOW, 0), t0 + tq, kt):
            back = (t0 - k0) - ahead
            crosses = k0 + kt - 1 > t0 or k0 <= t0 + tq - 1 - WINDOW
            band = per_head(jnp.where(back < 0, NEG, jnp.where(back < WINDOW, 0.0, NEG))) if crosses else None
            win.append((kw_ref[k0:k0 + kt, :], vwt_ref[:, k0:k0 + kt], band))
        o_win, o_slc = attend(win, slc)
        mixed = []
        for hh in range(hpg):
            base = GATE_LANE0 + (grp * hpg + hh) * 3
            head = slice(hh * tq, (hh + 1) * tq)
            mixed.append(gt_ref[pl.ds(base + 1, 1), :] * o_slc[:, head]
                         + gt_ref[pl.ds(base + 2, 1), :] * o_win[:, head])
        o_ref[...] = (ocmp_ref[...] + jnp.concatenate(mixed, axis=0)).T.astype(o_ref.dtype)

    for tile in range(ks_ref.shape[0] // tq):
        pl.when(pl.program_id(2) == tile)(functools.partial(step, tile))


def _nsa_attention(qt, sel_bias, ks, vst, kw, vwt, gates_t, o_cmp):
    b, _, _, s = qt.shape
    tq = NSA_TQ
    kspec = pl.BlockSpec((None, None, s, LANES), lambda bi, g, i: (bi, g, 0, 0))
    vspec = pl.BlockSpec((None, None, NSA_VT_ROWS, s), lambda bi, g, i: (bi, g, 0, 0))
    tok = pl.BlockSpec((None, tq, NSA_HPG * NSA_DH), lambda bi, g, i: (bi, i, g))
    return pl.pallas_call(
        _nsa_kernel,
        grid=(b, NSA_KV, s // tq),
        in_specs=[pl.BlockSpec((None, NSA_HPG, NSA_DH, tq), lambda bi, g, i: (bi, g, 0, i)),
                  pl.BlockSpec((None, None, s // SEL_BLOCK, tq), lambda bi, g, i: (bi, g, 0, i)),
                  kspec, vspec, kspec, vspec,
                  pl.BlockSpec((None, NSA_GATE_ROWS, tq), lambda bi, g, i: (bi, 0, i)),
                  pl.BlockSpec((None, None, NSA_HPG * NSA_DH, tq), lambda bi, g, i: (bi, g, 0, i))],
        out_specs=tok,
        out_shape=jax.ShapeDtypeStruct((b, s, W_NSA_Q), BF16),
        scratch_shapes=[pltpu.VMEM((tq, NSA_HPG * tq), F32)] * 2,
        compiler_params=pltpu.CompilerParams(
            dimension_semantics=("arbitrary", "arbitrary", "arbitrary"),
            vmem_limit_bytes=VMEM_LIMIT),
        name="nsa_attn",
    )(qt, sel_bias, ks, vst, kw, vwt, gates_t, o_cmp)


def _out_ffn2_kernel(x_ref, og_ref, z_ref, on_ref, gn_ref, wo_ref, nw_ref, wg_ref, wu_ref, wd_ref, fw_ref,
                     o_ref):
    half = W_GDN_Z
    x = x_ref[...] + _dot(on_ref[...], wo_ref[half:, :])
    gate = _silu(z_ref[...])
    o_gdn = jnp.concatenate(
        [_rms(og_ref[:, hh * GDN_DV:(hh + 1) * GDN_DV], gn_ref[...]) for hh in range(GDN_HEADS)], axis=1)
    x = x + _dot((o_gdn * gate).astype(BF16), wo_ref[:half, :])
    y = _swiglu_half_step(x, nw_ref[...], wg_ref, wu_ref, wd_ref)
    o_ref[...] = _rms(y, fw_ref[...])


def _out_ffn2(x2d, o_gdn, z, o_nsa, gdn_norm, w_out, nw, wg, wu, wd, fw):
    t = x2d.shape[0]
    row = lambda w: pl.BlockSpec((TOKEN_TILE, w), lambda i: (i, 0))
    return pl.pallas_call(
        _out_ffn2_kernel,
        grid=(t // TOKEN_TILE,),
        in_specs=[row(D_MODEL), row(W_GDN_Z), row(W_GDN_Z), row(W_NSA_Q), _const_spec((1, GDN_DV)),
                  _const_spec((D_MODEL, D_MODEL)),
                  _const_spec((1, D_MODEL)), _const_spec((D_MODEL, D_FF)),
                  _const_spec((D_MODEL, D_FF)), _const_spec((D_FF, D_MODEL)),
                  _const_spec((1, D_MODEL))],
        out_specs=row(D_MODEL),
        out_shape=jax.ShapeDtypeStruct((t, D_MODEL), F32),
        compiler_params=pltpu.CompilerParams(dimension_semantics=("arbitrary",),
                                             vmem_limit_bytes=VMEM_LIMIT),
        name="out_ffn2",
    )(x2d, o_gdn, z, o_nsa, gdn_norm, w_out, nw, wg, wu, wd, fw)


def _rope_tables(s_len):
    half = ROPE_DIM // 2
    inv = ROPE_THETA ** (-jnp.arange(half, dtype=F32) / half)
    ang = jnp.arange(s_len, dtype=F32)[:, None] * inv[None, :]
    cos, sin = jnp.cos(ang), jnp.sin(ang)
    pad = jnp.zeros((s_len, NSA_DH - ROPE_DIM), F32)
    zero = jnp.zeros_like(sin)
    cos_h = jnp.concatenate([cos, cos, pad + 1.0], axis=1)
    sa_h = jnp.concatenate([-sin, zero, pad], axis=1)
    sb_h = jnp.concatenate([zero, sin, pad], axis=1)
    rep = lambda a: jnp.tile(a, (1, LANES // NSA_DH))
    return rep(cos_h), rep(sa_h), rep(sb_h)


def _split_w_in(w_in):
    cuts = np.cumsum([W_GDN_QKV, W_GDN_Z, GDN_HEADS, GDN_HEADS, W_NSA_Q,
                      W_NSA_KV, W_NSA_KV, W_NSA_KV]).tolist()
    qkv, z, a, bb, nq, kvc, kvs, kvw, gate = jnp.split(w_in, cuts, axis=1)
    w_main = jnp.concatenate([qkv, z, nq, kvc, kvs, kvw], axis=1).astype(BF16)
    small = jnp.concatenate([a, bb, gate], axis=1)
    w_small = jnp.pad(small, ((0, 0), (0, LANES - small.shape[1]))).astype(BF16)
    return w_main, w_small


def _compress_weights(pe_k, w1_k, w2_k, pe_v, w1_v, w2_v):
    half = CMP_BLOCK // 2
    eye = jnp.eye(2 * NSA_KV, dtype=F32)

    def first_layer(lo):
        w1 = jnp.stack([w1_k, w1_v]).reshape(2, CMP_BLOCK, NSA_DH, CMP_HIDDEN)[:, lo:lo + half]
        w1 = jnp.repeat(w1, NSA_KV, axis=0)
        w = jnp.einsum('crdj,ce->rcdej', w1, eye)
        return w.reshape(half * 2 * NSA_KV * NSA_DH, 2 * NSA_KV * CMP_HIDDEN).astype(BF16)

    def pe_row(lo):
        pe = jnp.stack([pe_k, pe_v])[:, lo:lo + half]
        pe = jnp.repeat(pe, NSA_KV, axis=0)
        return pe.transpose(1, 0, 2).reshape(1, half * 2 * NSA_KV * NSA_DH)

    w2 = jnp.repeat(jnp.stack([w2_k, w2_v]), NSA_KV, axis=0)
    w2 = jnp.pad(w2, ((0, 0), (0, 0), (0, LANES - NSA_DH)))
    w2 = jnp.einsum('cjd,ce->cjed', w2, eye).reshape(2 * NSA_KV * CMP_HIDDEN, 2 * NSA_KV * LANES)
    return pe_row(0), pe_row(half), first_layer(0), first_layer(half), w2.astype(BF16)


def kernel(x, ffn1_norm, ffn1_w_gate, ffn1_w_up, ffn1_w_down, mix_norm, w_in, gdn_conv_w, gdn_a_log,
           gdn_dt_bias, gdn_out_norm, cmp_pe_k, cmp_w1_k, cmp_w2_k, cmp_pe_v, cmp_w1_v, cmp_w2_v,
           w_out, ffn2_norm, ffn2_w_gate, ffn2_w_up, ffn2_w_down, final_norm):
    b, s, d = x.shape
    assert d == D_MODEL and s % TOKEN_TILE == 0 and ffn1_norm.shape[0] == 1
    vec = lambda a: a.reshape(1, -1)
    pad_lanes = lambda a: jnp.pad(a.reshape(1, -1), ((0, 0), (0, LANES - a.size)))
    bf = lambda a: a.astype(BF16)

    x1 = _ffn1(x.reshape(b * s, d), vec(ffn1_norm[0]), bf(ffn1_w_gate[0]), bf(ffn1_w_up[0]),
               bf(ffn1_w_down[0]))

    w_main, w_small = _split_w_in(w_in[0])
    cos_t, sin_a, sin_b = _rope_tables(s)
    qkv, z, small, gates_t, qp, qt, kvc, ks, vst, kw, vwt = _in_proj(
        x1.reshape(b, s, d), vec(mix_norm[0]), w_main, w_small, cos_t, sin_a, sin_b,
        pad_lanes(gdn_a_log[0]), pad_lanes(gdn_dt_bias[0]), gdn_conv_w[0])

    o_gdn = _gdn(qkv, small)

    pe_top, pe_bot, w_top, w_bot, w2 = _compress_weights(
        cmp_pe_k[0], cmp_w1_k[0], cmp_w2_k[0], cmp_pe_v[0], cmp_w1_v[0], cmp_w2_v[0])
    kcvc = _compress(kvc, pe_top, pe_bot, w_top, w_bot, w2)

    o_cmp, sel_bias = _nsa_select(qp, kcvc, gates_t)
    o_nsa = _nsa_attention(qt, sel_bias, ks, vst, kw, vwt, gates_t, o_cmp).reshape(b * s, W_NSA_Q)

    out = _out_ffn2(x1, o_gdn.reshape(b * s, W_GDN_Z), z.reshape(b * s, W_GDN_Z), o_nsa,
                    vec(gdn_out_norm[0]), bf(w_out[0]), vec(ffn2_norm[0]),
                    bf(ffn2_w_gate[0]), bf(ffn2_w_up[0]), bf(ffn2_w_down[0]), vec(final_norm))
    return out.reshape(b, s, d)
```

```python
import functools

import numpy as np
import jax
import jax.numpy as jnp
from jax import lax
from jax.experimental import pallas as pl
from jax.experimental.pallas import tpu as pltpu

F32 = jnp.float32
BF16 = jnp.bfloat16

D_MODEL = 1024
D_FF = 2816
GDN_HEADS = 4
GDN_DK = 128
GDN_DV = 128
GDN_CONV = 4
GDN_CHUNK = 64
NSA_HEADS = 8
NSA_KV = 2
NSA_HPG = NSA_HEADS // NSA_KV
NSA_DH = 64
CMP_BLOCK = 32
CMP_STRIDE = 16
CMP_HIDDEN = 128
SEL_BLOCK = 64
SEL_TOPK = 16
WINDOW = 512
ROPE_THETA = 500000.0
ROPE_DIM = NSA_DH // 4
EPS = 1e-6
NEG = -1e30
LOG2E = 1.4426950408889634
BIG = 1e30

W_GDN_QKV = GDN_HEADS * (2 * GDN_DK + GDN_DV)
W_GDN_Z = GDN_HEADS * GDN_DV
W_NSA_Q = NSA_HEADS * NSA_DH
W_NSA_KV = 2 * NSA_KV * NSA_DH
W_NSA_GATE = 3 * NSA_HEADS
W_MAIN = W_GDN_QKV + W_GDN_Z + W_NSA_Q + 3 * W_NSA_KV
LANES = 128
GATE_LANE0 = 2 * GDN_HEADS

TOKEN_TILE = 512
MXU_TILE = 256
FF_SPLITS = (0, 6 * MXU_TILE, D_FF)
GDN_HPS = 2
GDN_GROUP = 2 * GDN_CHUNK
NSA_TQ = 256
NSA_TS = 512
NSA_VT_ROWS = 80
NSA_GATE_ROWS = GATE_LANE0 + W_NSA_GATE
VMEM_LIMIT = 56 * 1024 * 1024


def _dot(a, b):
    return jnp.dot(a, b, preferred_element_type=F32)


def _rms(x, w):
    return x * lax.rsqrt(jnp.mean(x * x, axis=-1, keepdims=True) + EPS) * w


def _sigmoid(x):
    return 0.5 * jnp.tanh(0.5 * x) + 0.5


def _silu(x):
    return x * _sigmoid(x)


def _swiglu_half_step(x, nw, wg_ref, wu_ref, wd_ref):
    hb = _rms(x, nw).astype(BF16)
    acc = jnp.zeros_like(x)
    for lo, hi in zip(FF_SPLITS[:-1], FF_SPLITS[1:]):
        sl = slice(lo, hi)
        g = _dot(hb, wg_ref[:, sl])
        u = _dot(hb, wu_ref[:, sl])
        acc = acc + _dot((_silu(g) * u).astype(BF16), wd_ref[sl, :])
    return x + 0.5 * acc


def _ffn_kernel(x_ref, nw_ref, wg_ref, wu_ref, wd_ref, o_ref):
    o_ref[...] = _swiglu_half_step(x_ref[...], nw_ref[...], wg_ref, wu_ref, wd_ref)


def _const_spec(shape):
    nd = len(shape)
    return pl.BlockSpec(shape, lambda *_: (0,) * nd, pipeline_mode=pl.Buffered(1))


def _ffn1(x2d, nw, wg, wu, wd):
    t = x2d.shape[0]
    row = pl.BlockSpec((TOKEN_TILE, D_MODEL), lambda i: (i, 0))
    return pl.pallas_call(
        _ffn_kernel,
        grid=(t // TOKEN_TILE,),
        in_specs=[row, _const_spec((1, D_MODEL)), _const_spec((D_MODEL, D_FF)),
                  _const_spec((D_MODEL, D_FF)), _const_spec((D_FF, D_MODEL))],
        out_specs=row,
        out_shape=jax.ShapeDtypeStruct((t, D_MODEL), F32),
        compiler_params=pltpu.CompilerParams(dimension_semantics=("arbitrary",),
                                             vmem_limit_bytes=VMEM_LIMIT),
        name="ffn1",
    )(x2d, nw, wg, wu, wd)


def _rope(x, cos_t, sin_a, sin_b):
    return (x * cos_t + pltpu.roll(x, LANES - ROPE_DIM // 2, axis=1) * sin_a
            + pltpu.roll(x, ROPE_DIM // 2, axis=1) * sin_b)


def _inproj_kernel(x_ref, xp_ref, nw_ref, wm_ref, ws_ref, cos_ref, sa_ref, sb_ref, alog_ref, dt_ref, conv_ref,
                   qkv_ref, z_ref, small_ref, gt_ref, qp_ref, qt_ref, kvc_ref,
                   ks_ref, vst_ref, kw_ref, vwt_ref, kvc_s):
    hb = _rms(x_ref[...], nw_ref[...]).astype(BF16)
    tm, halo = x_ref.shape[0], xp_ref.shape[0]
    hb_prev = jnp.where(pl.program_id(1) == 0, 0.0, _rms(xp_ref[...], nw_ref[...])).astype(BF16)
    hb_ext = jnp.concatenate([hb_prev, hb], axis=0)
    lane = lax.broadcasted_iota(jnp.int32, (tm, LANES), 1)
    pos = pl.program_id(1) * tm + lax.broadcasted_iota(jnp.int32, (tm, LANES), 0)
    low_half = lane < NSA_DH
    cos_t, sin_a, sin_b = cos_ref[...], sa_ref[...], sb_ref[...]
    scale = NSA_DH ** -0.5
    part_w = GDN_HEADS * GDN_DK

    def halves(slab):
        return slab, pltpu.roll(slab, NSA_DH, axis=1)

    def gdn_part(part, ext):
        lo = part * part_w
        cur = ext[halo:, :]
        taps = conv_ref[:, lo:lo + part_w]
        conv = cur * taps[GDN_CONV - 1:GDN_CONV, :]
        for sh in range(1, GDN_CONV):
            conv = conv + ext[halo - sh:halo - sh + tm, :] * taps[GDN_CONV - 1 - sh:GDN_CONV - sh, :]
        conv = _silu(conv)
        for hh in range(GDN_HEADS):
            blk = conv[:, hh * GDN_DK:(hh + 1) * GDN_DK]
            if part < 2:
                blk = blk * lax.rsqrt(jnp.sum(blk * blk, axis=-1, keepdims=True) + EPS)
            if part == 0:
                blk = blk * (GDN_DK ** -0.5)
            qkv_ref[:, lo + hh * GDN_DK:lo + (hh + 1) * GDN_DK] = blk

    def gate_z(y):
        z_ref[...] = y

    def nsa_q(y):
        for s in range(W_NSA_Q // LANES):
            slab = y[:, s * LANES:(s + 1) * LANES]
            plain_t = (slab * scale).T
            rot_t = (_rope(slab, cos_t, sin_a, sin_b) * (scale * LOG2E)).T
            for j in range(2):
                qp_ref[2 * s + j] = plain_t[j * NSA_DH:(j + 1) * NSA_DH, :].astype(BF16)
                qt_ref[2 * s + j] = rot_t[j * NSA_DH:(j + 1) * NSA_DH, :].astype(BF16)

    def cmp_kv(y):
        for half in range(W_NSA_KV // LANES):
            kvc_s[half] = y[:, half * LANES:(half + 1) * LANES]
        for r in range(CMP_STRIDE):
            for half in range(W_NSA_KV // LANES):
                dst = r * W_NSA_KV + half * LANES
                kvc_ref[:, dst:dst + LANES] = kvc_s[half, pl.ds(r, tm // CMP_STRIDE, stride=CMP_STRIDE), :]

    def attn_kv(k_ref, vt_ref, selected, y):
        k_pad = jnp.where(lane - NSA_DH == pos // SEL_BLOCK, 1.0, 0.0) if selected else 0.0
        n_tail = vt_ref.shape[1] - NSA_DH
        tail = jnp.where(lax.broadcasted_iota(jnp.int32, (n_tail, tm), 0) == 0, 1.0, 0.0)
        kr = halves(_rope(y[:, :LANES], cos_t, sin_a, sin_b))
        v_t = y[:, LANES:].T
        for j in range(NSA_KV):
            k_ref[j] = jnp.where(low_half, kr[j], k_pad).astype(BF16)
            vt_ref[j] = jnp.concatenate([v_t[j * NSA_DH:(j + 1) * NSA_DH, :], tail], axis=0).astype(BF16)

    def small_proj(sm):
        xg = sm + dt_ref[...]
        softplus = jnp.maximum(xg, 0.0) + jnp.log1p(jnp.exp(-jnp.abs(xg)))
        small = jnp.where(lane < GDN_HEADS, -jnp.exp(alog_ref[...]) * softplus, _sigmoid(sm))
        small_ref[...] = small
        gt_ref[...] = small.T[:gt_ref.shape[0], :]

    stages = [(hb, ws_ref, 0, LANES, small_proj)]
    stages += [(hb_ext, wm_ref, part * part_w, part_w, functools.partial(gdn_part, part))
               for part in range(W_GDN_QKV // part_w)]
    z_lo = W_GDN_QKV
    q_lo = z_lo + W_GDN_Z
    kvc_lo = q_lo + W_NSA_Q
    stages += [(hb, wm_ref, q_lo, W_NSA_Q, nsa_q),
               (hb, wm_ref, kvc_lo + W_NSA_KV, W_NSA_KV, functools.partial(attn_kv, ks_ref, vst_ref, True)),
               (hb, wm_ref, kvc_lo + 2 * W_NSA_KV, W_NSA_KV, functools.partial(attn_kv, kw_ref, vwt_ref, False)),
               (hb, wm_ref, z_lo, W_GDN_Z, gate_z),
               (hb, wm_ref, kvc_lo, W_NSA_KV, cmp_kv)]
    for lhs, w_ref, lo, width, epilogue in stages:
        epilogue(_dot(lhs, w_ref[:, lo:lo + width]))


def _in_proj(x3d, nw, w_main, w_small, cos_t, sin_a, sin_b, alog_p, dt_p, conv_w):
    b, s, _ = x3d.shape
    tm = TOKEN_TILE
    row = lambda w: pl.BlockSpec((None, tm, w), lambda bi, i: (bi, i, 0))
    heads = lambda n, w: pl.BlockSpec((None, n, tm, w), lambda bi, i: (bi, 0, i, 0))
    heads_t = lambda n, r: pl.BlockSpec((None, n, r, tm), lambda bi, i: (bi, 0, 0, i))
    tab = pl.BlockSpec((tm, LANES), lambda bi, i: (i, 0))
    halo = 16
    prev_rows = pl.BlockSpec((None, halo, D_MODEL), lambda bi, i: (bi, jnp.maximum(i * (tm // halo) - 1, 0), 0))
    bf_heads = lambda n, w: jax.ShapeDtypeStruct((b, n, s, w), BF16)
    out_shape = (
        jax.ShapeDtypeStruct((b, s, W_GDN_QKV), F32),
        jax.ShapeDtypeStruct((b, s, W_GDN_Z), F32),
        jax.ShapeDtypeStruct((b, s, LANES), F32),
        jax.ShapeDtypeStruct((b, NSA_GATE_ROWS, s), F32),
        jax.ShapeDtypeStruct((b, NSA_HEADS, NSA_DH, s), BF16),
        jax.ShapeDtypeStruct((b, NSA_HEADS, NSA_DH, s), BF16),
        jax.ShapeDtypeStruct((b, s // CMP_STRIDE, CMP_STRIDE * W_NSA_KV), F32),
        bf_heads(NSA_KV, LANES),
        jax.ShapeDtypeStruct((b, NSA_KV, NSA_VT_ROWS, s), BF16),
        bf_heads(NSA_KV, LANES),
        jax.ShapeDtypeStruct((b, NSA_KV, NSA_VT_ROWS, s), BF16),
    )
    return pl.pallas_call(
        _inproj_kernel,
        grid=(b, s // tm),
        in_specs=[row(D_MODEL), prev_rows, _const_spec((1, D_MODEL)), _const_spec((D_MODEL, W_MAIN)),
                  _const_spec((D_MODEL, LANES)), tab, tab, tab,
                  _const_spec((1, LANES)), _const_spec((1, LANES)), _const_spec((GDN_CONV, W_GDN_QKV))],
        out_specs=(row(W_GDN_QKV), row(W_GDN_Z), row(LANES),
                   pl.BlockSpec((None, NSA_GATE_ROWS, tm), lambda bi, i: (bi, 0, i)),
                   heads_t(NSA_HEADS, NSA_DH), heads_t(NSA_HEADS, NSA_DH),
                   pl.BlockSpec((None, tm // CMP_STRIDE, CMP_STRIDE * W_NSA_KV), lambda bi, i: (bi, i, 0)),
                   heads(NSA_KV, LANES), heads_t(NSA_KV, NSA_VT_ROWS),
                   heads(NSA_KV, LANES), heads_t(NSA_KV, NSA_VT_ROWS)),
        out_shape=out_shape,
        scratch_shapes=[pltpu.VMEM((W_NSA_KV // LANES, tm, LANES), F32)],
        compiler_params=pltpu.CompilerParams(dimension_semantics=("arbitrary", "arbitrary"),
                                             vmem_limit_bytes=VMEM_LIMIT),
        name="in_proj",
    )(x3d, x3d, nw, w_main, w_small, cos_t, sin_a, sin_b, alog_p, dt_p, conv_w)


def _bmm(a, b):
    return jnp.einsum('nij,njk->nik', a, b, preferred_element_type=F32)


def _bmm_nt(a, b):
    return jnp.einsum('nid,njd->nij', a, b, preferred_element_type=F32)


def _unit_lower_inverse_minus_eye(low, ii, jj):
    mm = lambda a, b: _bmm(a.astype(BF16), b.astype(BF16))
    same16 = (ii // 16) == (jj // 16)
    same32 = (ii // 32) == (jj // 32)
    ld = jnp.where(same16, low, 0.0)
    e = -ld
    p = mm(ld, ld)
    for level in range(3):
        e = e + p + mm(e, p)
        if level < 2:
            p = mm(p, p)
    for within in (same32 & jnp.logical_not(same16), jnp.logical_not(same32)):
        c = jnp.where(within, low, 0.0)
        y = c + mm(e, c)
        e = e - y - mm(y, e)
    return e


def _gdn_kernel(xq_ref, xk_ref, xv_ref, small_ref, o_ref,
                p_s, qn_s, e2_s, mpa_s, na_s, ea_s, qt_s, o0_s, seven_s):
    hp = pl.program_id(1)
    s_len = xq_ref.shape[0]
    c_len = GDN_CHUNK
    grp = GDN_GROUP
    n_chunks = s_len // c_len
    n_grp = s_len // grp
    hd = GDN_DK
    row = lax.broadcasted_iota(jnp.int32, (s_len, LANES), 0)
    lane = lax.broadcasted_iota(jnp.int32, (s_len, LANES), 1)
    pos = row % c_len
    ii = lax.broadcasted_iota(jnp.int32, (grp, grp), 0)
    jj = lax.broadcasted_iota(jnp.int32, (grp, grp), 1)
    same_chunk = (ii // c_len) == (jj // c_len)
    tril = same_chunk & (ii >= jj)
    strict = same_chunk & (ii > jj)
    first = lax.broadcasted_iota(jnp.int32, (grp, hd), 0) < c_len
    small = small_ref[...]
    g3 = lambda a: a.reshape(n_grp, grp, hd)

    for j in range(GDN_HPS):
        h = hp * GDN_HPS + j
        cols = slice(j * hd, (j + 1) * hd)
        q, k, v = xq_ref[:, cols], xk_ref[:, cols], xv_ref[:, cols]

        pick = lambda ln: jnp.broadcast_to(
            jnp.sum(jnp.where(lane == ln, small, 0.0), axis=-1, keepdims=True), (s_len, LANES))
        beta = pick(GDN_HEADS + h)
        g = pick(h)
        sh = 1
        while sh < c_len:
            g = g + jnp.where(pos >= sh, pltpu.roll(g, sh, axis=0), 0.0)
            sh *= 2
        gc = g
        gc3 = gc.reshape(n_chunks, c_len, LANES)
        gl3 = gc3[:, c_len - 1:c_len, :]
        e_gl = jnp.exp(gl3).reshape(n_grp, grp // c_len, 1, LANES)
        dec_a, dec_b = e_gl[:, 0], e_gl[:, 1]
        gl = jnp.broadcast_to(gl3, gc3.shape).reshape(s_len, LANES)
        eg = jnp.exp(gc)
        kb = k * beta
        vb = g3(v * beta)
        kbg = g3(kb * eg)

        k3 = g3(k).astype(BF16)
        kk = _bmm_nt(g3(kb).astype(BF16), k3)
        qk = _bmm_nt(g3(q).astype(BF16), k3)
        gcg = g3(gc)
        diff = gcg - jnp.swapaxes(gcg, 1, 2)
        decay = jnp.where(tril, jnp.exp(jnp.where(tril, diff, 0.0)), 0.0)
        e_b = _unit_lower_inverse_minus_eye(jnp.where(strict, kk * decay, 0.0), ii, jj).astype(BF16)
        u = vb + _bmm(e_b, vb.astype(BF16))
        w = kbg + _bmm(e_b, kbg.astype(BF16))
        qkm = jnp.where(tril, qk * decay, 0.0).astype(BF16)
        r = _bmm(qkm, jnp.concatenate([w, u], axis=-1).astype(BF16))
        qt_s[j] = (g3(q * eg) - r[..., :hd]).astype(BF16)
        o0_s[j] = r[..., hd:]
        kd_t = jnp.swapaxes(g3(k * jnp.exp(gl - gc)), 1, 2).astype(BF16)
        rhs = jnp.concatenate([jnp.where(first, w, 0.0), jnp.where(first, 0.0, w),
                               jnp.where(first, u, 0.0), jnp.where(first, 0.0, u)], axis=-1)
        mn = _bmm(kd_t, rhs.astype(BF16))
        mp_a, mp_b = -mn[..., :hd], -mn[..., hd:2 * hd]
        n_a, n_b = mn[..., 2 * hd:3 * hd], mn[..., 3 * hd:]
        x = _bmm(mp_b.astype(BF16), jnp.concatenate([mp_a, n_a], axis=-1).astype(BF16))
        p_s[j] = (dec_b * mp_a + dec_a * mp_b + x[..., :hd]).astype(BF16)
        qn_s[j] = dec_b * n_a + x[..., hd:] + n_b
        e2_s[j] = dec_a * dec_b
        mpa_s[j] = mp_a.astype(BF16)
        na_s[j] = n_a
        ea_s[j] = dec_a

    def scan(n, states):
        out = []
        for j in range(GDN_HPS):
            sb = states[j].astype(BF16)
            seven_s[j, n] = sb
            out.append(e2_s[j, n] * states[j] + _dot(p_s[j, n], sb) + qn_s[j, n])
        return tuple(out)

    lax.fori_loop(0, n_grp, scan, tuple(jnp.zeros((GDN_DK, GDN_DV), F32) for _ in range(GDN_HPS)))

    for j in range(GDN_HPS):
        cols = slice(j * hd, (j + 1) * hd)
        s_even = seven_s[j]
        s_odd = (ea_s[j] * s_even.astype(F32) + _bmm(mpa_s[j], s_even) + na_s[j]).astype(BF16)
        qt = qt_s[j]
        o = jnp.concatenate([_bmm(qt[:, :c_len], s_even), _bmm(qt[:, c_len:], s_odd)], axis=1) + o0_s[j]
        o_ref[:, cols] = o.reshape(s_len, hd)


def _gdn(qkv, small):
    b, s, _ = qkv.shape
    wid = GDN_HPS * GDN_DK
    n_hp = GDN_HEADS // GDN_HPS
    n_grp = s // GDN_GROUP
    per_pair = lambda rows, dt: pltpu.VMEM((GDN_HPS, n_grp, rows, GDN_DV), dt)
    col = lambda off: pl.BlockSpec((None, s, wid), lambda bi, h: (bi, 0, off + h))
    return pl.pallas_call(
        _gdn_kernel,
        grid=(b, n_hp),
        in_specs=[col(0), col(n_hp), col(2 * n_hp),
                  pl.BlockSpec((None, s, LANES), lambda bi, h: (bi, 0, 0))],
        out_specs=col(0),
        out_shape=jax.ShapeDtypeStruct((b, s, GDN_HEADS * GDN_DV), F32),
        scratch_shapes=[per_pair(GDN_DK, BF16), per_pair(GDN_DK, F32), per_pair(1, F32),
                        per_pair(GDN_DK, BF16), per_pair(GDN_DK, F32), per_pair(1, F32),
                        per_pair(GDN_GROUP, BF16), per_pair(GDN_GROUP, F32), per_pair(GDN_DK, BF16)],
        compiler_params=pltpu.CompilerParams(dimension_semantics=("arbitrary", "arbitrary"),
                                             vmem_limit_bytes=VMEM_LIMIT),
        name="gdn",
    )(qkv, qkv, qkv, small)


def _compress_kernel(t_ref, pet_ref, peb_ref, wt_ref, wb_ref, w2_ref, o_ref):
    t = t_ref[...]
    a = _dot((t + pet_ref[...]).astype(BF16), wt_ref[...])
    bm = _dot((t + peb_ref[...]).astype(BF16), wb_ref[...])
    n_rows = t.shape[0]
    hid = a + pltpu.roll(bm, n_rows - 1, axis=0)
    out = _dot(_silu(hid).astype(BF16), w2_ref[...])
    for j in range(2 * NSA_KV):
        blk = out[:, j * LANES:(j + 1) * LANES]
        o_ref[j] = (blk if j < NSA_KV else blk.T).astype(o_ref.dtype)


def _compress(tm, pe_top, pe_bot, w_top, w_bot, w2):
    b, n_rows, width = tm.shape
    nh = 2 * NSA_KV * CMP_HIDDEN
    return pl.pallas_call(
        _compress_kernel,
        grid=(b,),
        in_specs=[pl.BlockSpec((None, n_rows, width), lambda bi: (bi, 0, 0)),
                  _const_spec((1, width)), _const_spec((1, width)),
                  _const_spec((width, nh)), _const_spec((width, nh)),
                  _const_spec((nh, 2 * NSA_KV * LANES))],
        out_specs=pl.BlockSpec((None, 2 * NSA_KV, n_rows, LANES), lambda bi: (bi, 0, 0, 0)),
        out_shape=jax.ShapeDtypeStruct((b, 2 * NSA_KV, n_rows, LANES), BF16),
        compiler_params=pltpu.CompilerParams(dimension_semantics=("arbitrary",),
                                             vmem_limit_bytes=VMEM_LIMIT),
        name="compress",
    )(tm, pe_top, pe_bot, w_top, w_bot, w2)


def _nsa_select_kernel(qp_ref, kc_ref, vct_ref, gt_ref, ocmp_ref, bias_ref):
    grp = pl.program_id(1)
    ts, hpg, dh = qp_ref.shape[2], NSA_HPG, NSA_DH
    q0 = pl.program_id(2) * ts
    n_cmp = kc_ref.shape[0]
    n_blk = n_cmp * CMP_STRIDE // SEL_BLOCK
    per_head = lambda a: jnp.concatenate([a] * hpg, axis=1)

    t_c = q0 + lax.broadcasted_iota(jnp.int32, (n_cmp, ts), 1)
    n_c = lax.broadcasted_iota(jnp.int32, (n_cmp, ts), 0)
    vis = per_head(jnp.where(n_c * CMP_STRIDE + (CMP_BLOCK - 1) <= t_c, 1.0, 0.0))
    pad = jnp.zeros((LANES - dh, ts), BF16)
    q_t = jnp.concatenate([jnp.concatenate([qp_ref[hh], pad], axis=0) for hh in range(hpg)], axis=1)
    s_cmp = jnp.where(vis > 0.5, _dot(kc_ref[...], q_t), NEG)
    p_cmp = jnp.exp(s_cmp - jnp.max(s_cmp, axis=0, keepdims=True)) * vis
    p_cmp = p_cmp * (1.0 / jnp.maximum(jnp.sum(p_cmp, axis=0, keepdims=True), 1e-30))
    o_cmp = _dot(vct_ref[...], p_cmp.astype(BF16))
    ocmp_ref[...] = jnp.concatenate(
        [gt_ref[pl.ds(GATE_LANE0 + (grp * hpg + hh) * 3, 1), :] * o_cmp[:dh, hh * ts:(hh + 1) * ts]
         for hh in range(hpg)], axis=0)

    p_sum = p_cmp[:, :ts]
    for hh in range(1, hpg):
        p_sum = p_sum + p_cmp[:, hh * ts:(hh + 1) * ts]
    oj = lax.broadcasted_iota(jnp.int32, (n_blk, n_cmp), 0) * SEL_BLOCK
    on = lax.broadcasted_iota(jnp.int32, (n_blk, n_cmp), 1) * CMP_STRIDE
    overlap_t = jnp.where(on < oj + SEL_BLOCK, jnp.where(on + CMP_BLOCK > oj, 1.0, 0.0), 0.0)
    imp_t = jnp.dot(overlap_t, p_sum, precision=lax.Precision.HIGHEST,
                    preferred_element_type=F32)
    blk = lax.broadcasted_iota(jnp.int32, (n_blk, ts), 0)
    ahead = (q0 + lax.broadcasted_iota(jnp.int32, (n_blk, ts), 1)) // SEL_BLOCK - blk
    score = jnp.where(ahead < 0, NEG, jnp.where(blk == 0, BIG, jnp.where(ahead <= 1, BIG, imp_t)))

    def rank_and_select(n_live):
        rank = jnp.zeros((n_blk, ts), F32)
        for i in range(n_live):
            ci = jnp.broadcast_to(score[i:i + 1, :], (n_blk, ts))
            tie = jnp.where(blk > i, 1.0, 0.0)
            rank = rank + jnp.where(ci > score, 1.0, jnp.where(ci == score, tie, 0.0))
        bias_ref[...] = jnp.where(rank >= SEL_TOPK, NEG, 0.0).astype(bias_ref.dtype)

    for tile in range(n_blk * SEL_BLOCK // ts):
        pl.when(pl.program_id(2) == tile)(
            functools.partial(rank_and_select, (tile + 1) * ts // SEL_BLOCK))


def _nsa_select(qp_t, kcvc, gates_t):
    b, _, _, s = qp_t.shape
    ts = NSA_TS
    n_cmp = kcvc.shape[2]
    cspec = lambda off: pl.BlockSpec((None, None, n_cmp, LANES), lambda bi, g, i: (bi, off + g, 0, 0))
    return pl.pallas_call(
        _nsa_select_kernel,
        grid=(b, NSA_KV, s // ts),
        in_specs=[pl.BlockSpec((None, NSA_HPG, NSA_DH, ts), lambda bi, g, i: (bi, g, 0, i)),
                  cspec(0), cspec(NSA_KV),
                  pl.BlockSpec((None, NSA_GATE_ROWS, ts), lambda bi, g, i: (bi, 0, i))],
        out_specs=(pl.BlockSpec((None, None, NSA_HPG * NSA_DH, ts), lambda bi, g, i: (bi, g, 0, i)),
                   pl.BlockSpec((None, None, s // SEL_BLOCK, ts), lambda bi, g, i: (bi, g, 0, i))),
        out_shape=(jax.ShapeDtypeStruct((b, NSA_KV, NSA_HPG * NSA_DH, s), F32),
                   jax.ShapeDtypeStruct((b, NSA_KV, s // SEL_BLOCK, s), BF16)),
        compiler_params=pltpu.CompilerParams(
            dimension_semantics=("arbitrary", "arbitrary", "arbitrary"),
            vmem_limit_bytes=VMEM_LIMIT),
        name="nsa_select",
    )(qp_t, kcvc, kcvc, gates_t)


def _nsa_kernel(qt_ref, bias_ref, ks_ref, vst_ref, kw_ref, vwt_ref, gt_ref, ocmp_ref, o_ref,
                *score_bufs):
    grp = pl.program_id(1)
    tq, hpg, dh = NSA_TQ, NSA_HPG, NSA_DH
    bias = bias_ref[...]
    pad = jnp.zeros((LANES - dh - bias.shape[0], tq), BF16)
    q_aug = jnp.concatenate(
        [jnp.concatenate([qt_ref[hh], bias, pad], axis=0) for hh in range(hpg)], axis=1)
    per_head = lambda a: jnp.concatenate([a] * hpg, axis=1)

    def attend(*streams):
        tasks = [(i, chunk) for i, chunks in enumerate(streams) for chunk in chunks]

        def scores(n):
            k, _, mask = tasks[n][1]
            s_t = _dot(k, q_aug)
            score_bufs[n % 2][:k.shape[0], :] = s_t if mask is None else s_t + mask

        state = [(None, None)] * len(streams)
        scores(0)
        for n, (i, (k, v_t, _)) in enumerate(tasks):
            if n + 1 < len(tasks):
                scores(n + 1)
            s_t = score_bufs[n % 2][:k.shape[0], :]
            m, acc = state[i]
            m_new = jnp.max(s_t, axis=0, keepdims=True)
            if m is not None:
                m_new = jnp.maximum(m, m_new)
            part = _dot(v_t, jnp.exp2(s_t - m_new).astype(BF16))
            state[i] = (m_new, part if m is None else acc * jnp.exp2(m - m_new) + part)
        return [acc[:dh, :] * (1.0 / jnp.maximum(acc[dh:dh + 1, :], 1e-30)) for _, acc in state]

    kt = tq
    ahead = (lax.broadcasted_iota(jnp.int32, (kt, tq), 0)
             - lax.broadcasted_iota(jnp.int32, (kt, tq), 1))

    def step(tile):
        t0 = tile * tq
        slc = []
        for c in range(tile + 1):
            rows = slice(c * kt, (c + 1) * kt)
            causal = per_head(jnp.where(ahead <= 0, 0.0, NEG)) if c == tile else None
            slc.append((ks_ref[rows, :], vst_ref[:, rows], causal))
        win = []
        for k0 in range(max(t0 - WINDOW, 0), t0 + tq, kt):
            back = (t0 - k0) - ahead
            crosses = k0 + kt - 1 > t0 or k0 <= t0 + tq - 1 - WINDOW
            band = per_head(jnp.where(back < 0, NEG, jnp.where(back < WINDOW, 0.0, NEG))) if crosses else None
            win.append((kw_ref[k0:k0 + kt, :], vwt_ref[:, k0:k0 + kt], band))
        o_win, o_slc = attend(win, slc)
        mixed = []
        for hh in range(hpg):
            base = GATE_LANE0 + (grp * hpg + hh) * 3
            head = slice(hh * tq, (hh + 1) * tq)
            mixed.append(gt_ref[pl.ds(base + 1, 1), :] * o_slc[:, head]
                         + gt_ref[pl.ds(base + 2, 1), :] * o_win[:, head])
        o_ref[...] = (ocmp_ref[...] + jnp.concatenate(mixed, axis=0)).T.astype(o_ref.dtype)

    for tile in range(ks_ref.shape[0] // tq):
        pl.when(pl.program_id(2) == tile)(functools.partial(step, tile))


def _nsa_attention(qt, sel_bias, ks, vst, kw, vwt, gates_t, o_cmp):
    b, _, _, s = qt.shape
    tq = NSA_TQ
    kspec = pl.BlockSpec((None, None, s, LANES), lambda bi, g, i: (bi, g, 0, 0))
    vspec = pl.BlockSpec((None, None, NSA_VT_ROWS, s), lambda bi, g, i: (bi, g, 0, 0))
    tok = pl.BlockSpec((None, tq, NSA_HPG * NSA_DH), lambda bi, g, i: (bi, i, g))
    return pl.pallas_call(
        _nsa_kernel,
        grid=(b, NSA_KV, s // tq),
        in_specs=[pl.BlockSpec((None, NSA_HPG, NSA_DH, tq), lambda bi, g, i: (bi, g, 0, i)),
                  pl.BlockSpec((None, None, s // SEL_BLOCK, tq), lambda bi, g, i: (bi, g, 0, i)),
                  kspec, vspec, kspec, vspec,
                  pl.BlockSpec((None, NSA_GATE_ROWS, tq), lambda bi, g, i: (bi, 0, i)),
                  pl.BlockSpec((None, None, NSA_HPG * NSA_DH, tq), lambda bi, g, i: (bi, g, 0, i))],
        out_specs=tok,
        out_shape=jax.ShapeDtypeStruct((b, s, W_NSA_Q), BF16),
        scratch_shapes=[pltpu.VMEM((tq, NSA_HPG * tq), F32)] * 2,
        compiler_params=pltpu.CompilerParams(
            dimension_semantics=("arbitrary", "arbitrary", "arbitrary"),
            vmem_limit_bytes=VMEM_LIMIT),
        name="nsa_attn",
    )(qt, sel_bias, ks, vst, kw, vwt, gates_t, o_cmp)


def _out_ffn2_kernel(x_ref, og_ref, z_ref, on_ref, gn_ref, wo_ref, nw_ref, wg_ref, wu_ref, wd_ref, fw_ref,
                     o_ref):
    half = W_GDN_Z
    x = x_ref[...] + _dot(on_ref[...], wo_ref[half:, :])
    gate = _silu(z_ref[...])
    o_gdn = jnp.concatenate(
        [_rms(og_ref[:, hh * GDN_DV:(hh + 1) * GDN_DV], gn_ref[...]) for hh in range(GDN_HEADS)], axis=1)
    x = x + _dot((o_gdn * gate).astype(BF16), wo_ref[:half, :])
    y = _swiglu_half_step(x, nw_ref[...], wg_ref, wu_ref, wd_ref)
    o_ref[...] = _rms(y, fw_ref[...])


def _out_ffn2(x2d, o_gdn, z, o_nsa, gdn_norm, w_out, nw, wg, wu, wd, fw):
    t = x2d.shape[0]
    row = lambda w: pl.BlockSpec((TOKEN_TILE, w), lambda i: (i, 0))
    return pl.pallas_call(
        _out_ffn2_kernel,
        grid=(t // TOKEN_TILE,),
        in_specs=[row(D_MODEL), row(W_GDN_Z), row(W_GDN_Z), row(W_NSA_Q), _const_spec((1, GDN_DV)),
                  _const_spec((D_MODEL, D_MODEL)),
                  _const_spec((1, D_MODEL)), _const_spec((D_MODEL, D_FF)),
                  _const_spec((D_MODEL, D_FF)), _const_spec((D_FF, D_MODEL)),
                  _const_spec((1, D_MODEL))],
        out_specs=row(D_MODEL),
        out_shape=jax.ShapeDtypeStruct((t, D_MODEL), F32),
        compiler_params=pltpu.CompilerParams(dimension_semantics=("arbitrary",),
                                             vmem_limit_bytes=VMEM_LIMIT),
        name="out_ffn2",
    )(x2d, o_gdn, z, o_nsa, gdn_norm, w_out, nw, wg, wu, wd, fw)


def _rope_tables(s_len):
    half = ROPE_DIM // 2
    inv = ROPE_THETA ** (-jnp.arange(half, dtype=F32) / half)
    ang = jnp.arange(s_len, dtype=F32)[:, None] * inv[None, :]
    cos, sin = jnp.cos(ang), jnp.sin(ang)
    pad = jnp.zeros((s_len, NSA_DH - ROPE_DIM), F32)
    zero = jnp.zeros_like(sin)
    cos_h = jnp.concatenate([cos, cos, pad + 1.0], axis=1)
    sa_h = jnp.concatenate([-sin, zero, pad], axis=1)
    sb_h = jnp.concatenate([zero, sin, pad], axis=1)
    rep = lambda a: jnp.tile(a, (1, LANES // NSA_DH))
    return rep(cos_h), rep(sa_h), rep(sb_h)


def _split_w_in(w_in):
    cuts = np.cumsum([W_GDN_QKV, W_GDN_Z, GDN_HEADS, GDN_HEADS, W_NSA_Q,
                      W_NSA_KV, W_NSA_KV, W_NSA_KV]).tolist()
    qkv, z, a, bb, nq, kvc, kvs, kvw, gate = jnp.split(w_in, cuts, axis=1)
    w_main = jnp.concatenate([qkv, z, nq, kvc, kvs, kvw], axis=1).astype(BF16)
    small = jnp.concatenate([a, bb, gate], axis=1)
    w_small = jnp.pad(small, ((0, 0), (0, LANES - small.shape[1]))).astype(BF16)
    return w_main, w_small


def _compress_weights(pe_k, w1_k, w2_k, pe_v, w1_v, w2_v):
    half = CMP_BLOCK // 2
    eye = jnp.eye(2 * NSA_KV, dtype=F32)

    def first_layer(lo):
        w1 = jnp.stack([w1_k, w1_v]).reshape(2, CMP_BLOCK, NSA_DH, CMP_HIDDEN)[:, lo:lo + half]
        w1 = jnp.repeat(w1, NSA_KV, axis=0)
        w = jnp.einsum('crdj,ce->rcdej', w1, eye)
        return w.reshape(half * 2 * NSA_KV * NSA_DH, 2 * NSA_KV * CMP_HIDDEN).astype(BF16)

    def pe_row(lo):
        pe = jnp.stack([pe_k, pe_v])[:, lo:lo + half]
        pe = jnp.repeat(pe, NSA_KV, axis=0)
        return pe.transpose(1, 0, 2).reshape(1, half * 2 * NSA_KV * NSA_DH)

    w2 = jnp.repeat(jnp.stack([w2_k, w2_v]), NSA_KV, axis=0)
    w2 = jnp.pad(w2, ((0, 0), (0, 0), (0, LANES - NSA_DH)))
    w2 = jnp.einsum('cjd,ce->cjed', w2, eye).reshape(2 * NSA_KV * CMP_HIDDEN, 2 * NSA_KV * LANES)
    return pe_row(0), pe_row(half), first_layer(0), first_layer(half), w2.astype(BF16)


def kernel(x, ffn1_norm, ffn1_w_gate, ffn1_w_up, ffn1_w_down, mix_norm, w_in, gdn_conv_w, gdn_a_log,
           gdn_dt_bias, gdn_out_norm, cmp_pe_k, cmp_w1_k, cmp_w2_k, cmp_pe_v, cmp_w1_v, cmp_w2_v,
           w_out, ffn2_norm, ffn2_w_gate, ffn2_w_up, ffn2_w_down, final_norm):
    b, s, d = x.shape
    assert d == D_MODEL and s % TOKEN_TILE == 0 and ffn1_norm.shape[0] == 1
    vec = lambda a: a.reshape(1, -1)
    pad_lanes = lambda a: jnp.pad(a.reshape(1, -1), ((0, 0), (0, LANES - a.size)))
    bf = lambda a: a.astype(BF16)

    x1 = _ffn1(x.reshape(b * s, d), vec(ffn1_norm[0]), bf(ffn1_w_gate[0]), bf(ffn1_w_up[0]),
               bf(ffn1_w_down[0]))

    w_main, w_small = _split_w_in(w_in[0])
    cos_t, sin_a, sin_b = _rope_tables(s)
    qkv, z, small, gates_t, qp, qt, kvc, ks, vst, kw, vwt = _in_proj(
        x1.reshape(b, s, d), vec(mix_norm[0]), w_main, w_small, cos_t, sin_a, sin_b,
        pad_lanes(gdn_a_log[0]), pad_lanes(gdn_dt_bias[0]), gdn_conv_w[0])

    o_gdn = _gdn(qkv, small)

    pe_top, pe_bot, w_top, w_bot, w2 = _compress_weights(
        cmp_pe_k[0], cmp_w1_k[0], cmp_w2_k[0], cmp_pe_v[0], cmp_w1_v[0], cmp_w2_v[0])
    kcvc = _compress(kvc, pe_top, pe_bot, w_top, w_bot, w2)

    o_cmp, sel_bias = _nsa_select(qp, kcvc, gates_t)
    o_nsa = _nsa_attention(qt, sel_bias, ks, vst, kw, vwt, gates_t, o_cmp).reshape(b * s, W_NSA_Q)

    out = _out_ffn2(x1, o_gdn.reshape(b * s, W_GDN_Z), z.reshape(b * s, W_GDN_Z), o_nsa,
                    vec(gdn_out_norm[0]), bf(w_out[0]), vec(ffn2_norm[0]),
                    bf(ffn2_w_gate[0]), bf(ffn2_w_up[0]), bf(ffn2_w_down[0]), vec(final_norm))
    return out.reshape(b, s, d)
```
